```python
import jax, jax.numpy as jnp
from jax import lax
import numpy as np

D_MODEL = 1024
BATCH = 4
SEQ = 8192
DEPTH = 1

CHUNK = 64
N_MEM = 256
NORM_EPS = 1e-6

RW_HEAD = 64
RW_HEADS = D_MODEL // RW_HEAD
RW_WIDTH = RW_HEADS * RW_HEAD
RW_DECAY_RANK = 64
RW_ICLR_RANK = 64
RW_GATE_RANK = 160
RW_LN_EPS = 64e-5
RW_COLS = 3 * RW_WIDTH + RW_DECAY_RANK + RW_ICLR_RANK + RW_GATE_RANK
RW_SPLITS = (RW_WIDTH, 2 * RW_WIDTH, 3 * RW_WIDTH,
             3 * RW_WIDTH + RW_DECAY_RANK,
             3 * RW_WIDTH + RW_DECAY_RANK + RW_ICLR_RANK)

HG_KEY = 128
HG_HEADS = D_MODEL // HG_KEY
HG_VAL = D_MODEL // HG_HEADS
HG_WIDTH = HG_HEADS * HG_KEY
HG_VWIDTH = HG_HEADS * HG_VAL
HG_COLS = 2 * HG_WIDTH + 2 * HG_VWIDTH
HG_SPLITS = (HG_WIDTH, 2 * HG_WIDTH, 2 * HG_WIDTH + HG_VWIDTH)

GATE_COLS = 2 * D_MODEL
IN_COLS = RW_COLS + HG_COLS + GATE_COLS

XA_HEADS = 4
XA_HEAD = D_MODEL // XA_HEADS

D_FF = ((8 * D_MODEL + 3 * 256 - 1) // (3 * 256)) * 256

kernel_name = 'rwkv7_hgrn2_griffin_merge_block'


def rmsnorm(x, g, eps=NORM_EPS):
    xf = x.astype(jnp.float32)
    y = xf * lax.rsqrt(jnp.mean(xf * xf, axis=-1, keepdims=True) + eps)
    return (y * g.astype(jnp.float32)).astype(x.dtype)


def token_shift(u):
    return jnp.pad(u[:, :-1], ((0, 0), (1, 0), (0, 0)))


def rwkv7_branch(cols, mu, w0, w2, a0, a2, g2, k_k, k_a, r_k, ln_w, ln_b):
    B, S, _ = cols.shape
    f32 = jnp.float32
    cols = cols + mu * (token_shift(cols) - cols)
    r, k, v, w_lo, a_lo, g_lo = jnp.split(cols, RW_SPLITS, axis=-1)
    w_log = -jax.nn.softplus(-(w0 + jnp.tanh(w_lo) @ w2)) - 0.5
    decay = jnp.exp(-jnp.exp(w_log.astype(f32)))
    a = jax.nn.sigmoid(a0 + a_lo @ a2)
    g = jax.nn.sigmoid(g_lo) @ g2

    def heads(t):
        return t.astype(f32).reshape(B, S, RW_HEADS, RW_HEAD)

    kk = heads(k * k_k)
    kk = kk * lax.rsqrt(jnp.maximum(jnp.sum(kk * kk, axis=-1, keepdims=True), 1e-24))
    k_mod = heads(k * (1 + (a - 1) * k_a))
    r_h, v_h, a_h, w_h = heads(r), heads(v), heads(a), heads(decay)

    def step(state, inp):
        r_t, w_t, k_t, v_t, a_t, b_t = inp
        sa = jnp.einsum('bhvk,bhk->bhv', state, a_t)
        state = (state * w_t[:, :, None, :] + sa[..., None] * b_t[:, :, None, :]
                 + v_t[..., None] * k_t[:, :, None, :])
        return state, jnp.einsum('bhvk,bhk->bhv', state, r_t)

    tm = lambda t: jnp.swapaxes(t, 0, 1)
    s0 = jnp.zeros((B, RW_HEADS, RW_HEAD, RW_HEAD), f32)
    _, o = lax.scan(step, s0, (tm(r_h), tm(w_h), tm(k_mod), tm(v_h), tm(-kk), tm(kk * a_h)))
    o = jnp.swapaxes(o, 0, 1)
    mean = jnp.mean(o, axis=-1, keepdims=True)
    var = jnp.mean(jnp.square(o - mean), axis=-1, keepdims=True)
    o = ((o - mean) * lax.rsqrt(var + RW_LN_EPS)).reshape(B, S, RW_WIDTH) * ln_w + ln_b
    bonus = jnp.sum(r_h * k_mod * r_k, axis=-1, keepdims=True) * v_h
    o = o + bonus.reshape(B, S, RW_WIDTH)
    return (o * g).astype(cols.dtype)


def hgrn2_branch(cols, lb, norm_g):
    B, S, _ = cols.shape
    f32 = jnp.float32
    q, f_raw, i, g = jnp.split(cols, HG_SPLITS, axis=-1)
    f = lb + (1 - lb) * jax.nn.sigmoid(f_raw.astype(f32))
    log_f = jnp.log(f)
    k = 1 - f
    q = jax.nn.silu(q.astype(f32))
    n_chunks = S // CHUNK

    def chunked(t, d):
        return t.astype(f32).reshape(B, n_chunks, CHUNK, HG_HEADS, d).transpose(1, 0, 3, 2, 4)

    causal = jnp.tril(jnp.ones((CHUNK, CHUNK), dtype=bool))[:, :, None]

    def chunk_step(state, inp):
        q_c, k_c, i_c, lf_c = inp
        b = jnp.cumsum(lf_c, axis=2)
        rel = jnp.exp(jnp.where(causal, b[:, :, :, None, :] - b[:, :, None, :, :], -jnp.inf))
        scores = jnp.einsum('bhtk,bhtsk,bhsk->bhts', q_c, rel, k_c)
        o = (jnp.einsum('bhts,bhsv->bhtv', scores, i_c)
             + jnp.einsum('bhtk,bhkv->bhtv', q_c * jnp.exp(b), state))
        b_end = b[:, :, -1:, :]
        state = (jnp.exp(b_end[:, :, 0, :])[..., None] * state
                 + jnp.einsum('bhsk,bhsv->bhkv', k_c * jnp.exp(b_end - b), i_c))
        return state, o

    s0 = jnp.zeros((B, HG_HEADS, HG_KEY, HG_VAL), f32)
    _, o = lax.scan(chunk_step, s0, (chunked(q, HG_KEY), chunked(k, HG_KEY),
                                     chunked(i, HG_VAL), chunked(log_f, HG_KEY)))
    o = o.transpose(1, 0, 3, 2, 4).reshape(B, S, HG_HEADS, HG_VAL)
    o = o * lax.rsqrt(jnp.mean(o * o, axis=-1, keepdims=True) + NORM_EPS) * norm_g
    o = o.reshape(B, S, HG_VWIDTH) * jax.nn.silu(g.astype(f32))
    return o.astype(cols.dtype)


def setup_inputs(seed: int = 0) -> dict:
    key = jax.random.key(seed)
    ks = iter(jax.random.split(key, 32))
    L, D = DEPTH, D_MODEL

    def nrm(shape, scale):
        return scale * jax.random.normal(next(ks), shape, jnp.float32)

    def gain(shape):
        return 1.0 + nrm(shape, 0.02)

    return {
        'x': nrm((BATCH, SEQ, D), 1.0),
        'mem': nrm((BATCH, N_MEM, D), 1.0),
        'norm_mix_g': gain((L, D)),
        'w_in': nrm((L, D, IN_COLS), D ** -0.5),
        'rw_mu': jax.random.uniform(next(ks), (L, RW_COLS), jnp.float32),
        'rw_w0': -1.0 + nrm((L, RW_WIDTH), 0.5),
        'rw_w2': nrm((L, RW_DECAY_RANK, RW_WIDTH), 0.5 * RW_DECAY_RANK ** -0.5),
        'rw_a0': nrm((L, RW_WIDTH), 0.1),
        'rw_a2': nrm((L, RW_ICLR_RANK, RW_WIDTH), 0.5 * RW_ICLR_RANK ** -0.5),
        'rw_g2': nrm((L, RW_GATE_RANK, RW_WIDTH), RW_GATE_RANK ** -0.5),
        'rw_k_k': 0.85 + nrm((L, RW_WIDTH), 0.02),
        'rw_k_a': gain((L, RW_WIDTH)),
        'rw_r_k': nrm((L, RW_HEADS, RW_HEAD), 0.1),
        'rw_ln_w': gain((L, RW_WIDTH)),
        'rw_ln_b': nrm((L, RW_WIDTH), 0.02),
        'hg_lb_logits': nrm((L + 1, HG_WIDTH), 0.5),
        'hg_norm_g': gain((L, HG_VAL)),
        'w_out': nrm((L, D, D), D ** -0.5),
        'norm_xa_g': gain((L, D)),
        'norm_mem_g': gain((L, D)),
        'xa_wq': nrm((L, D, D), D ** -0.5),
        'xa_wk': nrm((L, D, D), D ** -0.5),
        'xa_wv': nrm((L, D, D), D ** -0.5),
        'xa_wo': nrm((L, D, D), D ** -0.5),
        'norm_ffn_g': gain((L, D)),
        'ffn_w1': nrm((L, D, D_FF), D ** -0.5),
        'ffn_w3': nrm((L, D, D_FF), D ** -0.5),
        'ffn_w2': nrm((L, D_FF, D), D_FF ** -0.5),
        'norm_final_g': gain((D,)),
    }


def reference(x, mem, norm_mix_g, w_in, rw_mu, rw_w0, rw_w2, rw_a0, rw_a2, rw_g2,
              rw_k_k, rw_k_a, rw_r_k, rw_ln_w, rw_ln_b, hg_lb_logits, hg_norm_g, w_out,
              norm_xa_g, norm_mem_g, xa_wq, xa_wk, xa_wv, xa_wo,
              norm_ffn_g, ffn_w1, ffn_w3, ffn_w2, norm_final_g):
    B, S, _ = x.shape
    n_mem = mem.shape[1]
    lb_all = jnp.cumsum(jax.nn.softmax(hg_lb_logits.astype(jnp.float32), axis=0), axis=0)
    h = x
    for l in range(DEPTH):
        u = rmsnorm(h, norm_mix_g[l])
        cols = u @ w_in[l]
        rw_cols, hg_cols, gate_cols = jnp.split(cols, (RW_COLS, RW_COLS + HG_COLS), axis=-1)
        y_a = rwkv7_branch(rw_cols, rw_mu[l], rw_w0[l], rw_w2[l], rw_a0[l], rw_a2[l], rw_g2[l],
                           rw_k_k[l], rw_k_a[l], rw_r_k[l], rw_ln_w[l], rw_ln_b[l])
        y_b = hgrn2_branch(hg_cols, lb_all[l], hg_norm_g[l])
        gate_a, gate_b = jnp.split(jax.nn.sigmoid(gate_cols), 2, axis=-1)
        h = h + (gate_a * y_a + gate_b * y_b) @ w_out[l]

        u = rmsnorm(h, norm_xa_g[l])
        m = rmsnorm(mem, norm_mem_g[l])
        q = (u @ xa_wq[l]).reshape(B, S, XA_HEADS, XA_HEAD)
        k = (m @ xa_wk[l]).reshape(B, n_mem, XA_HEADS, XA_HEAD)
        v = (m @ xa_wv[l]).reshape(B, n_mem, XA_HEADS, XA_HEAD)
        s = jnp.einsum('bqhd,bmhd->bhqm', q, k).astype(jnp.float32) * (XA_HEAD ** -0.5)
        p = jax.nn.softmax(s, axis=-1).astype(v.dtype)
        o = jnp.einsum('bhqm,bmhd->bqhd', p, v).reshape(B, S, D_MODEL)
        h = h + o @ xa_wo[l]

        u = rmsnorm(h, norm_ffn_g[l])
        h = h + (jax.nn.silu(u @ ffn_w1[l]) * (u @ ffn_w3[l])) @ ffn_w2[l]
    return rmsnorm(h, norm_final_g)
```

```python
import functools

import numpy as np
import jax
import jax.numpy as jnp
from jax import lax
from jax.experimental import pallas as pl
from jax.experimental.pallas import tpu as pltpu

F32 = jnp.float32
BF16 = jnp.bfloat16
HIGHEST = lax.Precision.HIGHEST

D_MODEL = 1024
NORM_EPS = 1e-6
LANES = 128

RW_HEAD = 64
RW_WIDTH = D_MODEL
RW_DECAY_RANK = 64
RW_ICLR_RANK = 64
RW_GATE_RANK = 160
RW_LN_EPS = 64e-5
RW_LO = RW_DECAY_RANK + RW_ICLR_RANK + RW_GATE_RANK
RW_LO_PAD = 512
RW_CHUNK = 64
RW_SUB = 16

HG_KEY = 128
HG_HEADS = D_MODEL // HG_KEY
HG_CHUNK = 64
HG_LEVELS = 6

XA_HEADS = 4
XA_HEAD = D_MODEL // XA_HEADS
D_FF = 2816

COL_R = 0
COL_K = COL_R + RW_WIDTH
COL_V = COL_K + RW_WIDTH
COL_LO = COL_V + RW_WIDTH
COL_HQ = COL_LO + RW_LO_PAD
COL_HF = COL_HQ + D_MODEL
COL_HI = COL_HF + D_MODEL
COL_HG = COL_HI + D_MODEL
COL_GA = COL_HG + D_MODEL
COL_GB = COL_GA + D_MODEL
N_COLS = COL_GB + D_MODEL

VMEM_LIMIT = 56 * 1024 * 1024


def _cparams(sem):
    return pltpu.CompilerParams(dimension_semantics=sem, vmem_limit_bytes=VMEM_LIMIT)


def _sigmoid(x):
    return 1.0 / (1.0 + jnp.exp(-x))


def _softplus(x):
    return jnp.maximum(x, 0.0) + jnp.log(1.0 + jnp.exp(-jnp.abs(x)))


def _rms(x, g):
    ms = jnp.mean(x * x, axis=-1, keepdims=True)
    return x * lax.rsqrt(ms + NORM_EPS) * g


def _dot(a, b, prec=None):
    if prec is None:
        a = a.astype(BF16)
        b = b.astype(BF16)
    return jnp.dot(a, b, preferred_element_type=F32, precision=prec)


def _dot_nt(a, b, prec=None):
    if prec is None:
        a = a.astype(BF16)
        b = b.astype(BF16)
    return lax.dot_general(a, b, (((1,), (1,)), ((), ())),
                           preferred_element_type=F32, precision=prec)


def _dot_tn(a, b, prec=None):
    if prec is None:
        a = a.astype(BF16)
        b = b.astype(BF16)
    return lax.dot_general(a, b, (((0,), (0,)), ((), ())),
                           preferred_element_type=F32, precision=prec)


def _pick(n, prefs):
    for p in prefs:
        if n % p == 0:
            return p
    return n


def _inproj_kernel(x_ref, g_ref, w_ref, o_ref, u_ref):
    @pl.when(pl.program_id(1) == 0)
    def _():
        u_ref[...] = _rms(x_ref[...], g_ref[...]).astype(BF16)

    o_ref[...] = jnp.dot(u_ref[...], w_ref[...], preferred_element_type=F32)


def _inproj(x2, g, w_packed):
    T = x2.shape[0]
    tm = _pick(T, (1024, 512, 256))
    tn = 512
    return pl.pallas_call(
        _inproj_kernel,
        grid=(T // tm, N_COLS // tn),
        in_specs=[
            pl.BlockSpec((tm, D_MODEL), lambda i, j: (i, 0)),
            pl.BlockSpec((1, D_MODEL), lambda i, j: (0, 0)),
            pl.BlockSpec((D_MODEL, tn), lambda i, j: (0, j)),
        ],
        out_specs=pl.BlockSpec((tm, tn), lambda i, j: (i, j)),
        out_shape=jax.ShapeDtypeStruct((T, N_COLS), F32),
        scratch_shapes=[pltpu.VMEM((tm, D_MODEL), BF16)],
        compiler_params=_cparams(("parallel", "arbitrary")),
        name="inproj",
    )(x2, g, w_packed)


def _shift_mix(raw, carry_row, mu):
    rolled = pltpu.roll(raw, 1, axis=0)
    row = lax.broadcasted_iota(jnp.int32, raw.shape, 0)
    prev = jnp.where(row == 0, carry_row, rolled)
    return raw + mu * (prev - raw)


def _pair_sum(x):
    lane = lax.broadcasted_iota(jnp.int32, x.shape, 1)
    first = lane < RW_HEAD
    s0 = jnp.sum(jnp.where(first, x, 0.0), axis=-1, keepdims=True)
    s1 = jnp.sum(jnp.where(first, 0.0, x), axis=-1, keepdims=True)
    return jnp.where(first, s0, s1)


def _unit_lower_inverse(n_strict):
    c = RW_CHUNK
    ri = lax.broadcasted_iota(jnp.int32, (c, c), 0)
    ci = lax.broadcasted_iota(jnp.int32, (c, c), 1)
    eye = (ri == ci).astype(F32)
    same = (ri // RW_SUB) == (ci // RW_SUB)
    dg = jnp.where(same, n_strict, 0.0)
    off = n_strict - dg
    d2 = _dot(dg, dg)
    d4 = _dot(d2, d2)
    d8 = _dot(d4, d4)
    x = eye + dg
    x = x + _dot(x, d2)
    x = x + _dot(x, d4)
    x = x + _dot(x, d8)
    q = _dot(x, off)
    q2 = _dot(q, q)
    left = eye + q + q2 + _dot(q, q2)
    return _dot(left, x)


def _rwkv_kernel(r_ref, k_ref, v_ref, lo_ref, ga_ref,
                 mur_ref, muk_ref, muv_ref, mulo_ref,
                 w0_ref, w2_ref, a0_ref, a2_ref, g2_ref,
                 kk_ref, ka_ref, rk_ref, lnw_ref, lnb_ref, tri_ref,
                 o_ref, state_ref, carry_ref, y_ref):
    ts = r_ref.shape[0]
    n_chunks = ts // RW_CHUNK
    c = RW_CHUNK

    @pl.when(pl.program_id(2) == 0)
    def _():
        state_ref[...] = jnp.zeros_like(state_ref)
        carry_ref[...] = jnp.zeros_like(carry_ref)

    r_raw = r_ref[...]
    k_raw = k_ref[...]
    v_raw = v_ref[...]
    lo_raw = lo_ref[...]
    r = _shift_mix(r_raw, carry_ref[0:1, 0:128], mur_ref[...])
    k = _shift_mix(k_raw, carry_ref[0:1, 128:256], muk_ref[...])
    v = _shift_mix(v_raw, carry_ref[0:1, 256:384], muv_ref[...])
    lo = _shift_mix(lo_raw, carry_ref[0:1, 384:384 + RW_LO_PAD], mulo_ref[...])
    carry_ref[0:1, 0:128] = r_raw[ts - 1:ts, :]
    carry_ref[0:1, 128:256] = k_raw[ts - 1:ts, :]
    carry_ref[0:1, 256:384] = v_raw[ts - 1:ts, :]
    carry_ref[0:1, 384:384 + RW_LO_PAD] = lo_raw[ts - 1:ts, :]

    w_lo = lo[:, 0:RW_DECAY_RANK]
    a_lo = lo[:, RW_DECAY_RANK:RW_DECAY_RANK + RW_ICLR_RANK]
    g_lo = lo[:, 128:128 + RW_GATE_RANK]
    wl = w0_ref[...] + _dot(jnp.tanh(w_lo), w2_ref[...])
    lw = -jnp.exp(-_softplus(-wl) - 0.5)
    a = _sigmoid(a0_ref[...] + _dot(a_lo, a2_ref[...]))
    g = _dot(_sigmoid(g_lo), g2_ref[...])

    kk = k * kk_ref[...]
    kkn = kk * lax.rsqrt(jnp.maximum(_pair_sum(kk * kk), 1e-24))
    kmod = k * (1.0 + (a - 1.0) * ka_ref[...])
    alpha = -kkn
    beta = kkn * a

    ri = lax.broadcasted_iota(jnp.int32, (c, c), 0)
    ci = lax.broadcasted_iota(jnp.int32, (c, c), 1)
    strict = ri > ci
    incl = ri >= ci
    tri = tri_ref[...]

    for ch in range(n_chunks):
        rows = slice(ch * c, (ch + 1) * c)
        lwc = lw[rows]
        cum = _dot(tri, lwc, HIGHEST)
        cume = cum - lwc
        c_end = cum[c - 1:c, :]
        e_pos = jnp.exp(cum)
        e_neg = jnp.exp(-cum)
        e_end = jnp.exp(c_end - cum)
        rt = r[rows] * e_pos
        at = alpha[rows] * jnp.exp(cume)
        bt = beta[rows] * e_neg
        kt = kmod[rows] * e_neg
        bh = beta[rows] * e_end
        kh = kmod[rows] * e_end
        w_end = jnp.exp(c_end)
        vc = v[rows]
        ys = []
        for h in range(2):
            hs = slice(h * RW_HEAD, (h + 1) * RW_HEAD)
            at_h, rt_h, bt_h, kt_h = at[:, hs], rt[:, hs], bt[:, hs], kt[:, hs]
            v_h = vc[:, hs]
            n_ab = jnp.where(strict, _dot_nt(at_h, bt_h), 0.0)
            a_ak = jnp.where(strict, _dot_nt(at_h, kt_h), 0.0)
            a_rb = jnp.where(incl, _dot_nt(rt_h, bt_h), 0.0)
            a_rk = jnp.where(incl, _dot_nt(rt_h, kt_h), 0.0)
            t_inv = _unit_lower_inverse(n_ab)
            s = state_ref[h]
            rhs = _dot_nt(at_h, s) + _dot(a_ak, v_h)
            p = _dot(t_inv, rhs)
            ys.append(_dot_nt(rt_h, s) + _dot(a_rb, p) + _dot(a_rk, v_h))
            state_ref[h] = (s * w_end[:, hs] + _dot_tn(p, bh[:, hs])
                            + _dot_tn(v_h, kh[:, hs]))
        y_ref[rows, :] = jnp.concatenate(ys, axis=-1)

    y = y_ref[...]
    mean = _pair_sum(y) * (1.0 / RW_HEAD)
    dlt = y - mean
    var = _pair_sum(dlt * dlt) * (1.0 / RW_HEAD)
    on = dlt * lax.rsqrt(var + RW_LN_EPS) * lnw_ref[...] + lnb_ref[...]
    bonus = _pair_sum(r * kmod * rk_ref[...]) * v
    o_ref[...] = ((on + bonus) * g) * _sigmoid(ga_ref[...])


def _rwkv(cols, B, S, p):
    ts = _pick(S, (256, 128, 64))
    nt = S // ts
    n_hp = RW_WIDTH // LANES
    cb = lambda col: col // LANES

    def colspec(col0):
        return pl.BlockSpec((ts, LANES), lambda b, h, t: (b * nt + t, cb(col0) + h))

    def vec(width=LANES):
        return pl.BlockSpec((1, width), lambda b, h, t: (0, h))

    def mat(rows):
        return pl.BlockSpec((rows, LANES), lambda b, h, t: (0, h))

    full = lambda shape: pl.BlockSpec(shape, lambda b, h, t: (0, 0))
    tri = jnp.asarray(np.tril(np.ones((RW_CHUNK, RW_CHUNK), np.float32)))
    return pl.pallas_call(
        _rwkv_kernel,
        grid=(B, n_hp, nt),
        in_specs=[
            colspec(COL_R), colspec(COL_K), colspec(COL_V),
            pl.BlockSpec((ts, RW_LO_PAD), lambda b, h, t: (b * nt + t, COL_LO // RW_LO_PAD)),
            colspec(COL_GA),
            vec(), vec(), vec(), full((1, RW_LO_PAD)),
            vec(), mat(RW_DECAY_RANK), vec(), mat(RW_ICLR_RANK), mat(RW_GATE_RANK),
            vec(), vec(), vec(), vec(), vec(),
            full((RW_CHUNK, RW_CHUNK)),
        ],
        out_specs=pl.BlockSpec((ts, LANES), lambda b, h, t: (b * nt + t, h)),
        out_shape=jax.ShapeDtypeStruct((B * S, RW_WIDTH), F32),
        scratch_shapes=[
            pltpu.VMEM((2, RW_HEAD, RW_HEAD), F32),
            pltpu.VMEM((8, 3 * LANES + RW_LO_PAD), F32),
            pltpu.VMEM((ts, LANES), F32),
        ],
        compiler_params=_cparams(("parallel", "parallel", "arbitrary")),
        name="rwkv7",
    )(cols, cols, cols, cols, cols,
      p["mu_r"], p["mu_k"], p["mu_v"], p["mu_lo"],
      p["w0"], p["w2"], p["a0"], p["a2"], p["g2"],
      p["k_k"], p["k_a"], p["r_k"], p["ln_w"], p["ln_b"], tri)


def _hg_level_matrix():
    c = HG_CHUNK
    m = np.zeros((HG_LEVELS + 1, c, c), np.float32)
    for l in range(HG_LEVELS):
        bs = c >> l
        half = bs // 2
        for t in range(c):
            mid = (t // bs) * bs + half
            if t % bs >= half:
                m[l, t, mid:t + 1] = 1.0
            else:
                m[l, t, t + 1:mid] = 1.0
    m[HG_LEVELS] = np.tril(np.ones((c, c), np.float32))
    return m.reshape((HG_LEVELS + 1) * c, c)


def _hgrn_kernel(q_ref, f_ref, i_ref, g_ref, gb_ref, lbl_ref, ng_ref, lvl_ref,
                 o_ref, state_ref):
    ts = q_ref.shape[0]
    c = HG_CHUNK
    n_chunks = ts // c

    @pl.when(pl.program_id(2) == 0)
    def _():
        state_ref[...] = jnp.zeros_like(state_ref)

    logits = lbl_ref[...]
    mx = jnp.max(logits, axis=0, keepdims=True)
    ex = jnp.exp(logits - mx)
    lb = ex[0:1, :] / jnp.sum(ex, axis=0, keepdims=True)

    f = lb + (1.0 - lb) * _sigmoid(f_ref[...])
    lf = jnp.log(f)
    kx = 1.0 - f
    qraw = q_ref[...]
    qs = qraw * _sigmoid(qraw)
    iv = i_ref[...]
    lvl = lvl_ref[...]

    ri = lax.broadcasted_iota(jnp.int32, (c, c), 0)
    ci = lax.broadcasted_iota(jnp.int32, (c, c), 1)
    rowi = lax.broadcasted_iota(jnp.int32, (c, LANES), 0)

    outs = []
    for ch in range(n_chunks):
        rows = slice(ch * c, (ch + 1) * c)
        q_c, k_c, i_c = qs[rows], kx[rows], iv[rows]
        part = _dot(lvl, lf[rows], HIGHEST)
        dsum = jnp.sum(q_c * k_c, axis=-1, keepdims=True)
        scores = jnp.where(ri == ci, dsum, 0.0)
        for l in range(HG_LEVELS):
            bs = c >> l
            e = jnp.exp(part[l * c:(l + 1) * c])
            second = (rowi & (bs - 1)) >= (bs // 2)
            qh = jnp.where(second, q_c * e, 0.0)
            kh = jnp.where(second, 0.0, k_c * e)
            sc = _dot_nt(qh, kh)
            if l > 0:
                sc = jnp.where((ri // bs) == (ci // bs), sc, 0.0)
            scores = scores + sc
        b = part[HG_LEVELS * c:(HG_LEVELS + 1) * c]
        b_end = b[c - 1:c, :]
        st = state_ref[...]
        o_c = _dot(scores, i_c) + _dot_nt(q_c * jnp.exp(b), st)
        state_ref[...] = st * jnp.exp(b_end) + _dot_tn(i_c, k_c * jnp.exp(b_end - b))
        outs.append(o_c)

    o = jnp.concatenate(outs, axis=0) if n_chunks > 1 else outs[0]
    o = o * lax.rsqrt(jnp.mean(o * o, axis=-1, keepdims=True) + NORM_EPS) * ng_ref[...]
    graw = g_ref[...]
    o_ref[...] = o * (graw * _sigmoid(graw)) * _sigmoid(gb_ref[...])


def _hgrn(cols, B, S, lb_logits, norm_g):
    ts = _pick(S, (256, 128, 64))
    nt = S // ts
    cb = lambda col: col // LANES

    def colspec(col0):
        return pl.BlockSpec((ts, LANES), lambda b, h, t: (b * nt + t, cb(col0) + h))

    n_slots = lb_logits.shape[0]
    lvl = jnp.asarray(_hg_level_matrix())
    return pl.pallas_call(
        _hgrn_kernel,
        grid=(B, HG_HEADS, nt),
        in_specs=[
            colspec(COL_HQ), colspec(COL_HF), colspec(COL_HI), colspec(COL_HG), colspec(COL_GB),
            pl.BlockSpec((n_slots, LANES), lambda b, h, t: (0, h)),
            pl.BlockSpec((1, LANES), lambda b, h, t: (0, 0)),
            pl.BlockSpec(lvl.shape, lambda b, h, t: (0, 0)),
        ],
        out_specs=pl.BlockSpec((ts, LANES), lambda b, h, t: (b * nt + t, h)),
        out_shape=jax.ShapeDtypeStruct((B * S, D_MODEL), F32),
        scratch_shapes=[pltpu.VMEM((HG_KEY, HG_KEY), F32)],
        compiler_params=_cparams(("parallel", "parallel", "arbitrary")),
        name="hgrn2",
    )(cols, cols, cols, cols, cols, lb_logits, norm_g, lvl)


def _outproj_kernel(x_ref, ya_ref, yb_ref, w_ref, o_ref):
    y = ya_ref[...] + yb_ref[...]
    o_ref[...] = x_ref[...] + _dot(y, w_ref[...])


def _outproj(x2, ya, yb, w):
    T = x2.shape[0]
    tm = _pick(T, (512, 256))
    row = pl.BlockSpec((tm, D_MODEL), lambda i: (i, 0))
    return pl.pallas_call(
        _outproj_kernel,
        grid=(T // tm,),
        in_specs=[row, row, row, pl.BlockSpec((D_MODEL, D_MODEL), lambda i: (0, 0))],
        out_specs=row,
        out_shape=jax.ShapeDtypeStruct((T, D_MODEL), F32),
        compiler_params=_cparams(("parallel",)),
        name="outproj",
    )(x2, ya, yb, w)


def _memkv_kernel(m_ref, g_ref, wk_ref, wv_ref, k_ref, v_ref):
    m = _rms(m_ref[...], g_ref[...]).astype(BF16)
    k_ref[...] = jnp.dot(m, wk_ref[...], preferred_element_type=F32).astype(BF16)
    v_ref[...] = jnp.dot(m, wv_ref[...], preferred_element_type=F32).astype(BF16)


def _memkv(mem2, g, wk, wv, B, n_mem):
    row = pl.BlockSpec((n_mem, D_MODEL), lambda b: (b, 0))
    wspec = pl.BlockSpec((D_MODEL, D_MODEL), lambda b: (0, 0))
    sds = jax.ShapeDtypeStruct((B * n_mem, D_MODEL), BF16)
    return pl.pallas_call(
        _memkv_kernel,
        grid=(B,),
        in_specs=[row, pl.BlockSpec((1, D_MODEL), lambda b: (0, 0)), wspec, wspec],
        out_specs=[row, row],
        out_shape=[sds, sds],
        compiler_params=_cparams(("parallel",)),
        name="memkv",
    )(mem2, g, wk, wv)


def _xattn_kernel(h_ref, g_ref, wq_ref, k_ref, v_ref, wo_ref, o_ref):
    h = h_ref[...]
    u = _rms(h, g_ref[...]).astype(BF16)
    q = jnp.dot(u, wq_ref[...], preferred_element_type=F32)
    k = k_ref[...]
    v = v_ref[...]
    heads = []
    for hd in range(XA_HEADS):
        sl = slice(hd * XA_HEAD, (hd + 1) * XA_HEAD)
        s = _dot_nt(q[:, sl], k[:, sl]) * (XA_HEAD ** -0.5)
        s = s - jnp.max(s, axis=-1, keepdims=True)
        e = jnp.exp(s)
        p = e / jnp.sum(e, axis=-1, keepdims=True)
        heads.append(_dot(p, v[:, sl]))
    o = jnp.concatenate(heads, axis=-1)
    o_ref[...] = h + _dot(o, wo_ref[...])


def _xattn(h1, g, wq, km, vm, wo, B, S, n_mem):
    tm = _pick(S, (512, 256))
    nt = S // tm
    row = pl.BlockSpec((tm, D_MODEL), lambda b, t: (b * nt + t, 0))
    wspec = pl.BlockSpec((D_MODEL, D_MODEL), lambda b, t: (0, 0))
    kv = pl.BlockSpec((n_mem, D_MODEL), lambda b, t: (b, 0))
    return pl.pallas_call(
        _xattn_kernel,
        grid=(B, nt),
        in_specs=[row, pl.BlockSpec((1, D_MODEL), lambda b, t: (0, 0)), wspec, kv, kv, wspec],
        out_specs=row,
        out_shape=jax.ShapeDtypeStruct((B * S, D_MODEL), F32),
        compiler_params=_cparams(("parallel", "parallel")),
        name="xattn",
    )(h1, g, wq, km, vm, wo)


def _ffn_kernel(h_ref, g_ref, w1_ref, w3_ref, w2_ref, gf_ref, o_ref, u_ref, acc_ref):
    j = pl.program_id(1)

    @pl.when(j == 0)
    def _():
        u_ref[...] = _rms(h_ref[...], g_ref[...]).astype(BF16)
        acc_ref[...] = h_ref[...]

    u = u_ref[...]
    a = jnp.dot(u, w1_ref[...], preferred_element_type=F32)
    b = jnp.dot(u, w3_ref[...], preferred_element_type=F32)
    mid = (a * _sigmoid(a)) * b
    acc_ref[...] += _dot(mid, w2_ref[...])

    @pl.when(j == pl.num_programs(1) - 1)
    def _():
        o_ref[...] = _rms(acc_ref[...], gf_ref[...])


def _ffn(h2, g, w1, w3, w2, gf):
    T = h2.shape[0]
    tm = _pick(T, (512, 256))
    tf = D_FF // 2
    row = pl.BlockSpec((tm, D_MODEL), lambda i, j: (i, 0))
    vec = pl.BlockSpec((1, D_MODEL), lambda i, j: (0, 0))
    return pl.pallas_call(
        _ffn_kernel,
        grid=(T // tm, D_FF // tf),
        in_specs=[row, vec,
                  pl.BlockSpec((D_MODEL, tf), lambda i, j: (0, j)),
                  pl.BlockSpec((D_MODEL, tf), lambda i, j: (0, j)),
                  pl.BlockSpec((tf, D_MODEL), lambda i, j: (j, 0)),
                  vec],
        out_specs=row,
        out_shape=jax.ShapeDtypeStruct((T, D_MODEL), F32),
        scratch_shapes=[pltpu.VMEM((tm, D_MODEL), BF16), pltpu.VMEM((tm, D_MODEL), F32)],
        compiler_params=_cparams(("parallel", "arbitrary")),
        name="ffn",
    )(h2, g, w1, w3, w2, gf)


def _pack_in_weights(w_in):
    rw_cols = 3 * RW_WIDTH + RW_LO
    hg0 = rw_cols
    ga0 = hg0 + 4 * D_MODEL
    pad = jnp.zeros((D_MODEL, RW_LO_PAD - RW_LO), w_in.dtype)
    lo = w_in[:, 3 * RW_WIDTH:rw_cols]
    lo = jnp.concatenate(
        [lo[:, :2 * RW_DECAY_RANK],
         lo[:, 2 * RW_DECAY_RANK:], pad], axis=1)
    packed = jnp.concatenate(
        [w_in[:, :3 * RW_WIDTH], lo, w_in[:, hg0:ga0], w_in[:, ga0:]], axis=1)
    return packed.astype(BF16)


def _pack_mu_lo(mu):
    lo = mu[3 * RW_WIDTH:]
    return jnp.concatenate([lo, jnp.zeros((RW_LO_PAD - RW_LO,), mu.dtype)])[None, :]


def kernel(x, mem, norm_mix_g, w_in, rw_mu, rw_w0, rw_w2, rw_a0, rw_a2, rw_g2, rw_k_k, rw_k_a, rw_r_k, rw_ln_w, rw_ln_b, hg_lb_logits, hg_norm_g, w_out, norm_xa_g, norm_mem_g, xa_wq, xa_wk, xa_wv, xa_wo, norm_ffn_g, ffn_w1, ffn_w3, ffn_w2, norm_final_g):
    B, S, _ = x.shape
    n_mem = mem.shape[1]
    depth = w_in.shape[0]
    assert depth == 1, "single-layer block"
    assert S % RW_CHUNK == 0 and S % HG_CHUNK == 0
    l = 0
    T = B * S
    row = lambda a: a.reshape(1, -1).astype(F32)

    x2 = x.reshape(T, D_MODEL)
    cols = _inproj(x2, row(norm_mix_g[l]), _pack_in_weights(w_in[l]))

    mu = rw_mu[l]
    rw = dict(
        mu_r=row(mu[COL_R:COL_K]), mu_k=row(mu[COL_K:COL_V]), mu_v=row(mu[COL_V:COL_LO]),
        mu_lo=_pack_mu_lo(mu),
        w0=row(rw_w0[l]), w2=rw_w2[l].astype(BF16), a0=row(rw_a0[l]), a2=rw_a2[l].astype(BF16),
        g2=rw_g2[l].astype(BF16), k_k=row(rw_k_k[l]), k_a=row(rw_k_a[l]), r_k=row(rw_r_k[l]),
        ln_w=row(rw_ln_w[l]), ln_b=row(rw_ln_b[l]))
    ya = _rwkv(cols, B, S, rw)
    yb = _hgrn(cols, B, S, hg_lb_logits.astype(F32), row(hg_norm_g[l]))

    h1 = _outproj(x2, ya, yb, w_out[l].astype(BF16))

    km, vm = _memkv(mem.reshape(B * n_mem, D_MODEL), row(norm_mem_g[l]),
                    xa_wk[l].astype(BF16), xa_wv[l].astype(BF16), B, n_mem)
    h2 = _xattn(h1, row(norm_xa_g[l]), xa_wq[l].astype(BF16), km, vm,
                xa_wo[l].astype(BF16), B, S, n_mem)

    out = _ffn(h2, row(norm_ffn_g[l]), ffn_w1[l].astype(BF16), ffn_w3[l].astype(BF16),
               ffn_w2[l].astype(BF16), row(norm_final_g))
    return out.reshape(B, S, D_MODEL)
```

```python
import numpy as np
import jax
import jax.numpy as jnp
from jax import lax
from jax.experimental import pallas as pl
from jax.experimental.pallas import tpu as pltpu

F32 = jnp.float32
BF16 = jnp.bfloat16
HIGHEST = lax.Precision.HIGHEST

D_MODEL = 1024
NORM_EPS = 1e-6
LANES = 128

RW_HEAD = 64
RW_WIDTH = D_MODEL
RW_DECAY_RANK = 64
RW_ICLR_RANK = 64
RW_GATE_RANK = 160
RW_LN_EPS = 64e-5
RW_LO = RW_DECAY_RANK + RW_ICLR_RANK + RW_GATE_RANK
RW_LO_PAD = 512
RW_CHUNK = 64
RW_SUB = 16

HG_KEY = 128
HG_HEADS = D_MODEL // HG_KEY
HG_CHUNK = 64
HG_LEVELS = 6

XA_HEADS = 4
XA_HEAD = D_MODEL // XA_HEADS
D_FF = 2816

COL_R = 0
COL_K = COL_R + RW_WIDTH
COL_V = COL_K + RW_WIDTH
COL_LO = COL_V + RW_WIDTH
COL_HQ = COL_LO + RW_LO_PAD
COL_HF = COL_HQ + D_MODEL
COL_HI = COL_HF + D_MODEL
COL_HG = COL_HI + D_MODEL
COL_GA = COL_HG + D_MODEL
COL_GB = COL_GA + D_MODEL
N_COLS = COL_GB + D_MODEL

VMEM_LIMIT = 56 * 1024 * 1024


def _cparams(sem):
    return pltpu.CompilerParams(dimension_semantics=sem, vmem_limit_bytes=VMEM_LIMIT)


def _sigmoid(x):
    return 1.0 / (1.0 + jnp.exp(-x))


def _softplus(x):
    return jnp.maximum(x, 0.0) + jnp.log(1.0 + jnp.exp(-jnp.abs(x)))


def _rms(x, g):
    ms = jnp.mean(x * x, axis=-1, keepdims=True)
    return x * lax.rsqrt(ms + NORM_EPS) * g


def _dot(a, b, prec=None):
    if prec is None:
        a = a.astype(BF16)
        b = b.astype(BF16)
    return jnp.dot(a, b, preferred_element_type=F32, precision=prec)


def _dot_nt(a, b, prec=None):
    if prec is None:
        a = a.astype(BF16)
        b = b.astype(BF16)
    return lax.dot_general(a, b, (((1,), (1,)), ((), ())),
                           preferred_element_type=F32, precision=prec)


def _dot_tn(a, b, prec=None):
    if prec is None:
        a = a.astype(BF16)
        b = b.astype(BF16)
    return lax.dot_general(a, b, (((0,), (0,)), ((), ())),
                           preferred_element_type=F32, precision=prec)


def _pick(n, prefs):
    for p in prefs:
        if n % p == 0:
            return p
    return n


def _inproj_kernel(x_ref, g_ref, w_ref, o_ref, u_ref):
    @pl.when(pl.program_id(1) == 0)
    def _():
        u_ref[...] = _rms(x_ref[...], g_ref[...]).astype(BF16)

    o_ref[...] = jnp.dot(u_ref[...], w_ref[...], preferred_element_type=F32)


def _inproj(x2, g, w_packed):
    T = x2.shape[0]
    tm = _pick(T, (1024, 512, 256))
    tn = 512
    return pl.pallas_call(
        _inproj_kernel,
        grid=(T // tm, N_COLS // tn),
        in_specs=[
            pl.BlockSpec((tm, D_MODEL), lambda i, j: (i, 0)),
            pl.BlockSpec((1, D_MODEL), lambda i, j: (0, 0)),
            pl.BlockSpec((D_MODEL, tn), lambda i, j: (0, j)),
        ],
        out_specs=pl.BlockSpec((tm, tn), lambda i, j: (i, j)),
        out_shape=jax.ShapeDtypeStruct((T, N_COLS), F32),
        scratch_shapes=[pltpu.VMEM((tm, D_MODEL), BF16)],
        compiler_params=_cparams(("parallel", "arbitrary")),
        name="inproj",
    )(x2, g, w_packed)


def _shift_mix(raw, carry_row, mu):
    rolled = pltpu.roll(raw, 1, axis=0)
    row = lax.broadcasted_iota(jnp.int32, raw.shape, 0)
    prev = jnp.where(row == 0, carry_row, rolled)
    return raw + mu * (prev - raw)


def _pair_sum(x):
    lane = lax.broadcasted_iota(jnp.int32, x.shape, 1)
    first = lane < RW_HEAD
    s0 = jnp.sum(jnp.where(first, x, 0.0), axis=-1, keepdims=True)
    s1 = jnp.sum(jnp.where(first, 0.0, x), axis=-1, keepdims=True)
    return jnp.where(first, s0, s1)


def _rwkv_kernel(r_ref, k_ref, v_ref, lo_ref, ga_ref,
                 mur_ref, muk_ref, muv_ref, mulo_ref,
                 w0_ref, w2_ref, a0_ref, a2_ref, g2_ref,
                 kk_ref, ka_ref, rk_ref, lnw_ref, lnb_ref, cs_ref,
                 o_ref, state_ref, carry_ref):
    nb, ts, _ = r_ref.shape
    c = RW_CHUNK
    n_chunks = ts // c
    items = [(b, ch) for b in range(nb) for ch in range(n_chunks)]

    @pl.when(pl.program_id(1) == 0)
    def _():
        state_ref[...] = jnp.zeros_like(state_ref)
        carry_ref[...] = jnp.zeros_like(carry_ref)

    t_i = lax.broadcasted_iota(jnp.int32, (c, LANES), 0)
    s_i = lax.broadcasted_iota(jnp.int32, (c, LANES), 1) & (RW_HEAD - 1)
    strict = t_i > s_i
    incl = t_i >= s_i
    eye = (t_i == s_i).astype(F32)
    same_sub = (t_i // RW_SUB) == (s_i // RW_SUB)
    rr = lax.broadcasted_iota(jnp.int32, (LANES, LANES), 0)
    cc = lax.broadcasted_iota(jnp.int32, (LANES, LANES), 1)
    bd_mask = (rr // RW_HEAD) == (cc // RW_HEAD)

    def bd(x):
        return jnp.where(bd_mask, jnp.concatenate([x, x], axis=0), 0.0).astype(BF16)

    def pmul(a, bmat):
        return _dot(a, bd(bmat))

    pre = []
    for b in range(nb):
        r_raw, k_raw, v_raw, lo_raw = r_ref[b], k_ref[b], v_ref[b], lo_ref[b]
        r = _shift_mix(r_raw, carry_ref[b, 0:1, 0:128], mur_ref[...])
        k = _shift_mix(k_raw, carry_ref[b, 0:1, 128:256], muk_ref[...])
        v = _shift_mix(v_raw, carry_ref[b, 0:1, 256:384], muv_ref[...])
        lo = _shift_mix(lo_raw, carry_ref[b, 0:1, 384:384 + RW_LO_PAD], mulo_ref[...])
        carry_ref[b, 0:1, 0:128] = r_raw[ts - 1:ts, :]
        carry_ref[b, 0:1, 128:256] = k_raw[ts - 1:ts, :]
        carry_ref[b, 0:1, 256:384] = v_raw[ts - 1:ts, :]
        carry_ref[b, 0:1, 384:384 + RW_LO_PAD] = lo_raw[ts - 1:ts, :]

        w_lo = lo[:, 0:RW_DECAY_RANK]
        a_lo = lo[:, RW_DECAY_RANK:RW_DECAY_RANK + RW_ICLR_RANK]
        g_lo = lo[:, 128:128 + RW_GATE_RANK]
        wl = w0_ref[...] + _dot(jnp.tanh(w_lo), w2_ref[...])
        lw = -jnp.exp(-_softplus(-wl) - 0.5)
        a = _sigmoid(a0_ref[...] + _dot(a_lo, a2_ref[...]))
        g = _dot(_sigmoid(g_lo), g2_ref[...])

        kk = k * kk_ref[...]
        kkn = kk * lax.rsqrt(jnp.maximum(_pair_sum(kk * kk), 1e-24))
        kmod = k * (1.0 + (a - 1.0) * ka_ref[...])
        beta = kkn * a

        cs = _dot(cs_ref[...], lw, HIGHEST)
        cum, c_end = cs[0:ts], cs[ts:2 * ts]
        e_neg = jnp.exp(-cum)
        e_end = jnp.exp(c_end - cum)
        pre.append(dict(
            r=r, v=v, kmod=kmod, g=g,
            rt=r * jnp.exp(cum), at=-kkn * jnp.exp(cum - lw),
            bt=beta * e_neg, kt=kmod * e_neg, bh=beta * e_end, kh=kmod * e_end,
            w_end=jnp.exp(c_end)))

    def rows(name, it):
        b, ch = it
        return pre[b][name][ch * c:(ch + 1) * c]

    lhs2 = {it: jnp.concatenate([rows("at", it), rows("rt", it)], axis=0) for it in items}
    amat = {it: _dot_nt(lhs2[it], jnp.concatenate([bd(rows("bt", it)), bd(rows("kt", it))], axis=0))
            for it in items}
    n_ab = {it: jnp.where(strict, amat[it][0:c, 0:LANES], 0.0) for it in items}
    a_ak = {it: jnp.where(strict, amat[it][0:c, LANES:2 * LANES], 0.0) for it in items}
    a_r2 = {it: jnp.concatenate(
        [jnp.where(incl, amat[it][c:2 * c, 0:LANES], 0.0),
         jnp.where(incl, amat[it][c:2 * c, LANES:2 * LANES], 0.0)], axis=1) for it in items}
    bdv = {it: bd(rows("v", it)) for it in items}
    u0 = {it: _dot(a_ak[it], bdv[it]) for it in items}

    dg = {it: jnp.where(same_sub, n_ab[it], 0.0) for it in items}
    off = {it: n_ab[it] - dg[it] for it in items}
    d2 = {it: pmul(dg[it], dg[it]) for it in items}
    x = {it: eye + dg[it] for it in items}
    x = {it: x[it] + pmul(x[it], d2[it]) for it in items}
    d4 = {it: pmul(d2[it], d2[it]) for it in items}
    x = {it: x[it] + pmul(x[it], d4[it]) for it in items}
    d8 = {it: pmul(d4[it], d4[it]) for it in items}
    x = {it: x[it] + pmul(x[it], d8[it]) for it in items}
    q = {it: pmul(x[it], off[it]) for it in items}
    q2 = {it: pmul(q[it], q[it]) for it in items}
    z = {it: x[it] + pmul(q[it], x[it]) for it in items}
    t_inv = {it: z[it] + pmul(q2[it], z[it]) for it in items}

    ys = {}
    for ch in range(n_chunks):
        its = [(b, ch) for b in range(nb)]
        st = {it: state_ref[it[0]] for it in its}
        sar = {it: _dot_nt(lhs2[it], st[it]) for it in its}
        p = {it: pmul(t_inv[it], sar[it][0:c] + u0[it]) for it in its}
        zz = {it: _dot_tn(jnp.concatenate([p[it], rows("v", it)], axis=0),
                          jnp.concatenate([rows("bh", it), rows("kh", it)], axis=0)) for it in its}
        for it in its:
            state_ref[it[0]] = (st[it] * pre[it[0]]["w_end"][ch * c:ch * c + 1]
                                + jnp.where(bd_mask, zz[it], 0.0))
        for it in its:
            ys[it] = sar[it][c:2 * c] + _dot(
                a_r2[it], jnp.concatenate([bd(p[it]), bdv[it]], axis=0))

    for b in range(nb):
        y = jnp.concatenate([ys[(b, ch)] for ch in range(n_chunks)], axis=0)
        pb = pre[b]
        mean = _pair_sum(y) * (1.0 / RW_HEAD)
        dlt = y - mean
        var = _pair_sum(dlt * dlt) * (1.0 / RW_HEAD)
        on = dlt * lax.rsqrt(var + RW_LN_EPS) * lnw_ref[...] + lnb_ref[...]
        bonus = _pair_sum(pb["r"] * pb["kmod"] * rk_ref[...]) * pb["v"]
        o_ref[b] = ((on + bonus) * pb["g"]) * _sigmoid(ga_ref[b])


def _chunk_sum_matrix(ts, c):
    blk = np.arange(ts) // c
    same = blk[:, None] == blk[None, :]
    tri = same & (np.arange(ts)[:, None] >= np.arange(ts)[None, :])
    return np.concatenate([tri, same], axis=0).astype(np.float32)


def _rwkv(cols3, p):
    B, S, _ = cols3.shape
    ts = _pick(S, (256, 128, 64))
    nt = S // ts
    n_hp = RW_WIDTH // LANES
    cb = lambda col: col // LANES

    def colspec(col0):
        return pl.BlockSpec((B, ts, LANES), lambda h, t: (0, t, cb(col0) + h))

    vec = pl.BlockSpec((1, LANES), lambda h, t: (0, h))
    mat = lambda nrows: pl.BlockSpec((nrows, LANES), lambda h, t: (0, h))
    full = lambda shape: pl.BlockSpec(shape, lambda h, t: (0, 0))
    cs = jnp.asarray(_chunk_sum_matrix(ts, RW_CHUNK))
    return pl.pallas_call(
        _rwkv_kernel,
        grid=(n_hp, nt),
        in_specs=[
            colspec(COL_R), colspec(COL_K), colspec(COL_V),
            pl.BlockSpec((B, ts, RW_LO_PAD), lambda h, t: (0, t, COL_LO // RW_LO_PAD)),
            colspec(COL_GA),
            vec, vec, vec, full((1, RW_LO_PAD)),
            vec, mat(RW_DECAY_RANK), vec, mat(RW_ICLR_RANK), mat(RW_GATE_RANK),
            vec, vec, vec, vec, vec,
            full(cs.shape),
        ],
        out_specs=pl.BlockSpec((B, ts, LANES), lambda h, t: (0, t, h)),
        out_shape=jax.ShapeDtypeStruct((B, S, RW_WIDTH), F32),
        scratch_shapes=[
            pltpu.VMEM((B, LANES, LANES), F32),
            pltpu.VMEM((B, 8, 3 * LANES + RW_LO_PAD), F32),
        ],
        compiler_params=_cparams(("parallel", "arbitrary")),
        name="rwkv7",
    )(cols3, cols3, cols3, cols3, cols3,
      p["mu_r"], p["mu_k"], p["mu_v"], p["mu_lo"],
      p["w0"], p["w2"], p["a0"], p["a2"], p["g2"],
      p["k_k"], p["k_a"], p["r_k"], p["ln_w"], p["ln_b"], cs)


def _hg_level_matrix():
    c = HG_CHUNK
    m = np.zeros((HG_LEVELS + 1, c, c), np.float32)
    for l in range(HG_LEVELS):
        bs = c >> l
        half = bs // 2
        for t in range(c):
            mid = (t // bs) * bs + half
            if t % bs >= half:
                m[l, t, mid:t + 1] = 1.0
            else:
                m[l, t, t + 1:mid] = 1.0
    m[HG_LEVELS] = np.tril(np.ones((c, c), np.float32))
    return m.reshape((HG_LEVELS + 1) * c, c)


def _hgrn_kernel(q_ref, f_ref, i_ref, g_ref, gb_ref, lbl_ref, ng_ref, lvl_ref,
                 o_ref, state_ref):
    nb, ts, _ = q_ref.shape
    c = HG_CHUNK
    n_chunks = ts // c
    items = [(b, ch) for b in range(nb) for ch in range(n_chunks)]

    @pl.when(pl.program_id(1) == 0)
    def _():
        state_ref[...] = jnp.zeros_like(state_ref)

    logits = lbl_ref[...]
    mx = jnp.max(logits, axis=0, keepdims=True)
    ex = jnp.exp(logits - mx)
    lb = ex[0:1, :] / jnp.sum(ex, axis=0, keepdims=True)

    ri = lax.broadcasted_iota(jnp.int32, (c, c), 0)
    ci = lax.broadcasted_iota(jnp.int32, (c, c), 1)
    rowi = lax.broadcasted_iota(jnp.int32, (c, LANES), 0)
    lvl = lvl_ref[...]

    pre = []
    for b in range(nb):
        f = lb + (1.0 - lb) * _sigmoid(f_ref[b])
        qraw = q_ref[b]
        pre.append(dict(lf=jnp.log(f), kx=1.0 - f, qs=qraw * _sigmoid(qraw), iv=i_ref[b]))

    def rows(name, it):
        b, ch = it
        return pre[b][name][ch * c:(ch + 1) * c]

    part = {it: _dot(lvl, rows("lf", it), HIGHEST) for it in items}
    scores = {}
    for it in items:
        q_c, k_c = rows("qs", it), rows("kx", it)
        dsum = jnp.sum(q_c * k_c, axis=-1, keepdims=True)
        sc_sum = jnp.where(ri == ci, dsum, 0.0)
        for l in range(HG_LEVELS):
            bs = c >> l
            e = jnp.exp(part[it][l * c:(l + 1) * c])
            second = (rowi & (bs - 1)) >= (bs // 2)
            qh = jnp.where(second, q_c * e, 0.0)
            kh = jnp.where(second, 0.0, k_c * e)
            sc = _dot_nt(qh, kh)
            if l > 0:
                sc = jnp.where((ri // bs) == (ci // bs), sc, 0.0)
            sc_sum = sc_sum + sc
        scores[it] = sc_sum
    o_intra = {it: _dot(scores[it], rows("iv", it)) for it in items}
    bcum = {it: part[it][HG_LEVELS * c:(HG_LEVELS + 1) * c] for it in items}
    zc = {it: _dot_tn(rows("iv", it),
                      rows("kx", it) * jnp.exp(bcum[it][c - 1:c, :] - bcum[it])) for it in items}
    qb = {it: rows("qs", it) * jnp.exp(bcum[it]) for it in items}

    outs = {}
    for b in range(nb):
        st = state_ref[b]
        for ch in range(n_chunks):
            it = (b, ch)
            outs[it] = o_intra[it] + _dot_nt(qb[it], st)
            st = st * jnp.exp(bcum[it][c - 1:c, :]) + zc[it]
        state_ref[b] = st

    ng = ng_ref[...]
    for b in range(nb):
        o = jnp.concatenate([outs[(b, ch)] for ch in range(n_chunks)], axis=0)
        o = o * lax.rsqrt(jnp.mean(o * o, axis=-1, keepdims=True) + NORM_EPS) * ng
        graw = g_ref[b]
        o_ref[b] = o * (graw * _sigmoid(graw)) * _sigmoid(gb_ref[b])


def _hgrn(cols3, lb_logits, norm_g):
    B, S, _ = cols3.shape
    ts = _pick(S, (256, 128, 64))
    nt = S // ts
    cb = lambda col: col // LANES

    def colspec(col0):
        return pl.BlockSpec((B, ts, LANES), lambda h, t: (0, t, cb(col0) + h))

    n_slots = lb_logits.shape[0]
    lvl = jnp.asarray(_hg_level_matrix())
    return pl.pallas_call(
        _hgrn_kernel,
        grid=(HG_HEADS, nt),
        in_specs=[
            colspec(COL_HQ), colspec(COL_HF), colspec(COL_HI), colspec(COL_HG), colspec(COL_GB),
            pl.BlockSpec((n_slots, LANES), lambda h, t: (0, h)),
            pl.BlockSpec((1, LANES), lambda h, t: (0, 0)),
            pl.BlockSpec(lvl.shape, lambda h, t: (0, 0)),
        ],
        out_specs=pl.BlockSpec((B, ts, LANES), lambda h, t: (0, t, h)),
        out_shape=jax.ShapeDtypeStruct((B, S, D_MODEL), F32),
        scratch_shapes=[pltpu.VMEM((B, HG_KEY, HG_KEY), F32)],
        compiler_params=_cparams(("parallel", "arbitrary")),
        name="hgrn2",
    )(cols3, cols3, cols3, cols3, cols3, lb_logits, norm_g, lvl)


def _outproj_kernel(x_ref, ya_ref, yb_ref, w_ref, o_ref):
    y = ya_ref[...] + yb_ref[...]
    o_ref[...] = x_ref[...] + _dot(y, w_ref[...])


def _outproj(x2, ya, yb, w):
    T = x2.shape[0]
    tm = _pick(T, (512, 256))
    row = pl.BlockSpec((tm, D_MODEL), lambda i: (i, 0))
    return pl.pallas_call(
        _outproj_kernel,
        grid=(T // tm,),
        in_specs=[row, row, row, pl.BlockSpec((D_MODEL, D_MODEL), lambda i: (0, 0))],
        out_specs=row,
        out_shape=jax.ShapeDtypeStruct((T, D_MODEL), F32),
        compiler_params=_cparams(("parallel",)),
        name="outproj",
    )(x2, ya, yb, w)


def _memkv_kernel(m_ref, g_ref, wk_ref, wv_ref, k_ref, v_ref):
    m = _rms(m_ref[...], g_ref[...]).astype(BF16)
    k_ref[...] = jnp.dot(m, wk_ref[...], preferred_element_type=F32).astype(BF16)
    v_ref[...] = jnp.dot(m, wv_ref[...], preferred_element_type=F32).astype(BF16)


def _memkv(mem2, g, wk, wv, B, n_mem):
    row = pl.BlockSpec((n_mem, D_MODEL), lambda b: (b, 0))
    wspec = pl.BlockSpec((D_MODEL, D_MODEL), lambda b: (0, 0))
    sds = jax.ShapeDtypeStruct((B * n_mem, D_MODEL), BF16)
    return pl.pallas_call(
        _memkv_kernel,
        grid=(B,),
        in_specs=[row, pl.BlockSpec((1, D_MODEL), lambda b: (0, 0)), wspec, wspec],
        out_specs=[row, row],
        out_shape=[sds, sds],
        compiler_params=_cparams(("parallel",)),
        name="memkv",
    )(mem2, g, wk, wv)


def _xattn_kernel(h_ref, g_ref, wq_ref, k_ref, v_ref, wo_ref, o_ref):
    h = h_ref[...]
    u = _rms(h, g_ref[...]).astype(BF16)
    q = jnp.dot(u, wq_ref[...], preferred_element_type=F32)
    k = k_ref[...]
    v = v_ref[...]
    heads = []
    for hd in range(XA_HEADS):
        sl = slice(hd * XA_HEAD, (hd + 1) * XA_HEAD)
        s = _dot_nt(q[:, sl], k[:, sl]) * (XA_HEAD ** -0.5)
        s = s - jnp.max(s, axis=-1, keepdims=True)
        e = jnp.exp(s)
        p = e / jnp.sum(e, axis=-1, keepdims=True)
        heads.append(_dot(p, v[:, sl]))
    o = jnp.concatenate(heads, axis=-1)
    o_ref[...] = h + _dot(o, wo_ref[...])


def _xattn(h1, g, wq, km, vm, wo, B, S, n_mem):
    tm = _pick(S, (512, 256))
    nt = S // tm
    row = pl.BlockSpec((tm, D_MODEL), lambda b, t: (b * nt + t, 0))
    wspec = pl.BlockSpec((D_MODEL, D_MODEL), lambda b, t: (0, 0))
    kv = pl.BlockSpec((n_mem, D_MODEL), lambda b, t: (b, 0))
    return pl.pallas_call(
        _xattn_kernel,
        grid=(B, nt),
        in_specs=[row, pl.BlockSpec((1, D_MODEL), lambda b, t: (0, 0)), wspec, kv, kv, wspec],
        out_specs=row,
        out_shape=jax.ShapeDtypeStruct((B * S, D_MODEL), F32),
        compiler_params=_cparams(("parallel", "parallel")),
        name="xattn",
    )(h1, g, wq, km, vm, wo)


def _ffn_kernel(h_ref, g_ref, w1_ref, w3_ref, w2_ref, gf_ref, o_ref, u_ref, acc_ref):
    j = pl.program_id(1)

    @pl.when(j == 0)
    def _():
        u_ref[...] = _rms(h_ref[...], g_ref[...]).astype(BF16)
        acc_ref[...] = h_ref[...]

    u = u_ref[...]
    a = jnp.dot(u, w1_ref[...], preferred_element_type=F32)
    b = jnp.dot(u, w3_ref[...], preferred_element_type=F32)
    mid = (a * _sigmoid(a)) * b
    acc_ref[...] += _dot(mid, w2_ref[...])

    @pl.when(j == pl.num_programs(1) - 1)
    def _():
        o_ref[...] = _rms(acc_ref[...], gf_ref[...])


def _ffn(h2, g, w1, w3, w2, gf):
    T = h2.shape[0]
    tm = _pick(T, (512, 256))
    tf = D_FF // 2
    row = pl.BlockSpec((tm, D_MODEL), lambda i, j: (i, 0))
    vec = pl.BlockSpec((1, D_MODEL), lambda i, j: (0, 0))
    return pl.pallas_call(
        _ffn_kernel,
        grid=(T // tm, D_FF // tf),
        in_specs=[row, vec,
                  pl.BlockSpec((D_MODEL, tf), lambda i, j: (0, j)),
                  pl.BlockSpec((D_MODEL, tf), lambda i, j: (0, j)),
                  pl.BlockSpec((tf, D_MODEL), lambda i, j: (j, 0)),
                  vec],
        out_specs=row,
        out_shape=jax.ShapeDtypeStruct((T, D_MODEL), F32),
        scratch_shapes=[pltpu.VMEM((tm, D_MODEL), BF16), pltpu.VMEM((tm, D_MODEL), F32)],
        compiler_params=_cparams(("parallel", "arbitrary")),
        name="ffn",
    )(h2, g, w1, w3, w2, gf)


def _pack_in_weights(w_in):
    rw_cols = 3 * RW_WIDTH + RW_LO
    pad = jnp.zeros((D_MODEL, RW_LO_PAD - RW_LO), w_in.dtype)
    packed = jnp.concatenate([w_in[:, :rw_cols], pad, w_in[:, rw_cols:]], axis=1)
    return packed.astype(BF16)


def _pack_mu_lo(mu):
    lo = mu[3 * RW_WIDTH:]
    return jnp.concatenate([lo, jnp.zeros((RW_LO_PAD - RW_LO,), mu.dtype)])[None, :]


def kernel(x, mem, norm_mix_g, w_in, rw_mu, rw_w0, rw_w2, rw_a0, rw_a2, rw_g2, rw_k_k, rw_k_a, rw_r_k, rw_ln_w, rw_ln_b, hg_lb_logits, hg_norm_g, w_out, norm_xa_g, norm_mem_g, xa_wq, xa_wk, xa_wv, xa_wo, norm_ffn_g, ffn_w1, ffn_w3, ffn_w2, norm_final_g):
    B, S, _ = x.shape
    n_mem = mem.shape[1]
    depth = w_in.shape[0]
    assert depth == 1, "single-layer block"
    assert S % RW_CHUNK == 0 and S % HG_CHUNK == 0
    l = 0
    T = B * S
    row = lambda a: a.reshape(1, -1).astype(F32)

    x2 = x.reshape(T, D_MODEL)
    cols = _inproj(x2, row(norm_mix_g[l]), _pack_in_weights(w_in[l]))
    cols3 = cols.reshape(B, S, N_COLS)

    mu = rw_mu[l]
    rw = dict(
        mu_r=row(mu[COL_R:COL_K]), mu_k=row(mu[COL_K:COL_V]), mu_v=row(mu[COL_V:COL_LO]),
        mu_lo=_pack_mu_lo(mu),
        w0=row(rw_w0[l]), w2=rw_w2[l].astype(BF16), a0=row(rw_a0[l]), a2=rw_a2[l].astype(BF16),
        g2=rw_g2[l].astype(BF16), k_k=row(rw_k_k[l]), k_a=row(rw_k_a[l]), r_k=row(rw_r_k[l]),
        ln_w=row(rw_ln_w[l]), ln_b=row(rw_ln_b[l]))
    ya = _rwkv(cols3, rw).reshape(T, D_MODEL)
    yb = _hgrn(cols3, hg_lb_logits.astype(F32), row(hg_norm_g[l])).reshape(T, D_MODEL)

    h1 = _outproj(x2, ya, yb, w_out[l].astype(BF16))

    km, vm = _memkv(mem.reshape(B * n_mem, D_MODEL), row(norm_mem_g[l]),
                    xa_wk[l].astype(BF16), xa_wv[l].astype(BF16), B, n_mem)
    h2 = _xattn(h1, row(norm_xa_g[l]), xa_wq[l].astype(BF16), km, vm,
                xa_wo[l].astype(BF16), B, S, n_mem)

    out = _ffn(h2, row(norm_ffn_g[l]), ffn_w1[l].astype(BF16), ffn_w3[l].astype(BF16),
               ffn_w2[l].astype(BF16), row(norm_final_g))
    return out.reshape(B, S, D_MODEL)
```

```python
import numpy as np
import jax
import jax.numpy as jnp
from jax import lax
from jax.experimental import pallas as pl
from jax.experimental.pallas import tpu as pltpu

F32 = jnp.float32
BF16 = jnp.bfloat16

D_MODEL = 1024
NORM_EPS = 1e-6
LANES = 128

RW_HEAD = 64
RW_WIDTH = D_MODEL
RW_DECAY_RANK = 64
RW_ICLR_RANK = 64
RW_GATE_RANK = 160
RW_LN_EPS = 64e-5
RW_LO = RW_DECAY_RANK + RW_ICLR_RANK + RW_GATE_RANK
RW_LO_PAD = 512
RW_CHUNK = 64
RW_SUB = 16

HG_KEY = 128
HG_HEADS = D_MODEL // HG_KEY
HG_CHUNK = 64
HG_LEVELS = 6

XA_HEADS = 4
XA_HEAD = D_MODEL // XA_HEADS
D_FF = 2816

COL_R = 0
COL_K = COL_R + RW_WIDTH
COL_V = COL_K + RW_WIDTH
COL_LO = COL_V + RW_WIDTH
COL_HQ = COL_LO + RW_LO_PAD
COL_HF = COL_HQ + D_MODEL
COL_HI = COL_HF + D_MODEL
COL_HG = COL_HI + D_MODEL
COL_GA = COL_HG + D_MODEL
COL_GB = COL_GA + D_MODEL
N_COLS = COL_GB + D_MODEL

COLS_DTYPE = BF16

VMEM_LIMIT = 56 * 1024 * 1024


def _cparams(sem):
    return pltpu.CompilerParams(dimension_semantics=sem, vmem_limit_bytes=VMEM_LIMIT)


def _sigmoid(x):
    return 1.0 / (1.0 + jnp.exp(-x))


def _softplus(x):
    return jnp.maximum(x, 0.0) + jnp.log(1.0 + jnp.exp(-jnp.abs(x)))


def _rms(x, g):
    ms = jnp.mean(x * x, axis=-1, keepdims=True)
    return x * lax.rsqrt(ms + NORM_EPS) * g


def _dot(a, b, prec=None):
    if prec is None:
        a = a.astype(BF16)
        b = b.astype(BF16)
    return jnp.dot(a, b, preferred_element_type=F32, precision=prec)


def _dot_nt(a, b, prec=None):
    if prec is None:
        a = a.astype(BF16)
        b = b.astype(BF16)
    return lax.dot_general(a, b, (((1,), (1,)), ((), ())),
                           preferred_element_type=F32, precision=prec)


def _dot_tn(a, b, prec=None):
    if prec is None:
        a = a.astype(BF16)
        b = b.astype(BF16)
    return lax.dot_general(a, b, (((0,), (0,)), ((), ())),
                           preferred_element_type=F32, precision=prec)


def _split3(x):
    hi = x.astype(BF16)
    r1 = x - hi.astype(F32)
    mid = r1.astype(BF16)
    lo = (r1 - mid.astype(F32)).astype(BF16)
    return jnp.concatenate([hi, mid, lo], axis=1)


def _exact_sums(pattern_bf16, x):
    y = jnp.dot(pattern_bf16, _split3(x), preferred_element_type=F32)
    return y[:, 0:LANES] + y[:, LANES:2 * LANES] + y[:, 2 * LANES:3 * LANES]


def _pick(n, prefs):
    for p in prefs:
        if n % p == 0:
            return p
    return n


def _inproj_kernel(x_ref, g_ref, w_ref, o_ref, u_ref):
    @pl.when(pl.program_id(1) == 0)
    def _():
        u_ref[...] = _rms(x_ref[...], g_ref[...]).astype(BF16)

    o_ref[...] = jnp.dot(u_ref[...], w_ref[...], preferred_element_type=F32).astype(o_ref.dtype)


def _inproj(x2, g, w_packed):
    T = x2.shape[0]
    tm = _pick(T, (1024, 512, 256))
    tn = N_COLS // 4
    return pl.pallas_call(
        _inproj_kernel,
        grid=(T // tm, N_COLS // tn),
        in_specs=[
            pl.BlockSpec((tm, D_MODEL), lambda i, j: (i, 0)),
            pl.BlockSpec((1, D_MODEL), lambda i, j: (0, 0)),
            pl.BlockSpec((D_MODEL, tn), lambda i, j: (0, j)),
        ],
        out_specs=pl.BlockSpec((tm, tn), lambda i, j: (i, j)),
        out_shape=jax.ShapeDtypeStruct((T, N_COLS), COLS_DTYPE),
        scratch_shapes=[pltpu.VMEM((tm, D_MODEL), BF16)],
        compiler_params=_cparams(("parallel", "arbitrary")),
        name="inproj",
    )(x2, g, w_packed)


def _shift_mix(raw, carry_row, mu):
    rolled = pltpu.roll(raw, 1, axis=0)
    row = lax.broadcasted_iota(jnp.int32, raw.shape, 0)
    prev = jnp.where(row == 0, carry_row, rolled)
    return raw + mu * (prev - raw)


def _pair_sum(x):
    lane = lax.broadcasted_iota(jnp.int32, x.shape, 1)
    first = lane < RW_HEAD
    s0 = jnp.sum(jnp.where(first, x, 0.0), axis=-1, keepdims=True)
    s1 = jnp.sum(jnp.where(first, 0.0, x), axis=-1, keepdims=True)
    return jnp.where(first, s0, s1)


def _rwkv_kernel(r_ref, k_ref, v_ref, lo_ref, ga_ref,
                 mur_ref, muk_ref, muv_ref, mulo_ref,
                 w0_ref, w2_ref, a0_ref, a2_ref, g2_ref,
                 kk_ref, ka_ref, rk_ref, lnw_ref, lnb_ref, tri_ref,
                 o_ref, state_ref, carry_ref):
    nb, ts, _ = r_ref.shape
    c = RW_CHUNK
    n_chunks = ts // c
    items = [(b, ch) for b in range(nb) for ch in range(n_chunks)]

    @pl.when(pl.program_id(1) == 0)
    def _():
        state_ref[...] = jnp.zeros_like(state_ref)
        carry_ref[...] = jnp.zeros_like(carry_ref)

    t_i = lax.broadcasted_iota(jnp.int32, (c, LANES), 0)
    s_i = lax.broadcasted_iota(jnp.int32, (c, LANES), 1) & (RW_HEAD - 1)
    strict = t_i > s_i
    incl = t_i >= s_i
    eye = (t_i == s_i).astype(F32)
    same_sub = (t_i // RW_SUB) == (s_i // RW_SUB)
    rr = lax.broadcasted_iota(jnp.int32, (LANES, LANES), 0)
    cc = lax.broadcasted_iota(jnp.int32, (LANES, LANES), 1)
    bd_mask = (rr // RW_HEAD) == (cc // RW_HEAD)

    def bd(x):
        return jnp.where(bd_mask, jnp.concatenate([x, x], axis=0), 0.0).astype(BF16)

    def pmul(a, bmat):
        return _dot(a, bd(bmat))

    pre = []
    for b in range(nb):
        r_raw, k_raw, v_raw, lo_raw = (ref[b].astype(F32) for ref in (r_ref, k_ref, v_ref, lo_ref))
        r = _shift_mix(r_raw, carry_ref[b, 0:1, 0:128], mur_ref[...])
        k = _shift_mix(k_raw, carry_ref[b, 0:1, 128:256], muk_ref[...])
        v = _shift_mix(v_raw, carry_ref[b, 0:1, 256:384], muv_ref[...])
        lo = _shift_mix(lo_raw, carry_ref[b, 0:1, 384:384 + RW_LO_PAD], mulo_ref[...])
        carry_ref[b, 0:1, 0:128] = r_raw[ts - 1:ts, :]
        carry_ref[b, 0:1, 128:256] = k_raw[ts - 1:ts, :]
        carry_ref[b, 0:1, 256:384] = v_raw[ts - 1:ts, :]
        carry_ref[b, 0:1, 384:384 + RW_LO_PAD] = lo_raw[ts - 1:ts, :]

        w_lo = lo[:, 0:RW_DECAY_RANK]
        a_lo = lo[:, RW_DECAY_RANK:RW_DECAY_RANK + RW_ICLR_RANK]
        g_lo = lo[:, 128:128 + RW_GATE_RANK]
        wl = w0_ref[...] + _dot(jnp.tanh(w_lo), w2_ref[...])
        lw = -jnp.exp(-_softplus(-wl) - 0.5)
        a = _sigmoid(a0_ref[...] + _dot(a_lo, a2_ref[...]))
        g = _dot(_sigmoid(g_lo), g2_ref[...])

        kk = k * kk_ref[...]
        kkn = kk * lax.rsqrt(jnp.maximum(_pair_sum(kk * kk), 1e-24))
        kmod = k * (1.0 + (a - 1.0) * ka_ref[...])
        beta = kkn * a

        pre.append(dict(r=r, v=v, kmod=kmod, g=g, lw=lw, kkn=kkn, beta=beta))

    def rows(name, it):
        b, ch = it
        return pre[b][name][ch * c:(ch + 1) * c]

    tri = tri_ref[...]
    pc = {}
    for it in items:
        lw_c = rows("lw", it)
        cum = _exact_sums(tri, lw_c)
        c_end = cum[c - 1:c, :]
        e_neg = jnp.exp(-cum)
        e_end = jnp.exp(c_end - cum)
        beta_c, kmod_c = rows("beta", it), rows("kmod", it)
        pc[it] = dict(
            rt=rows("r", it) * jnp.exp(cum), at=-rows("kkn", it) * jnp.exp(cum - lw_c),
            bt=beta_c * e_neg, kt=kmod_c * e_neg, bh=beta_c * e_end, kh=kmod_c * e_end,
            w_end=jnp.exp(c_end))

    lhs2 = {it: jnp.concatenate([pc[it]["at"], pc[it]["rt"]], axis=0) for it in items}
    amat = {it: _dot_nt(lhs2[it], jnp.concatenate([bd(pc[it]["bt"]), bd(pc[it]["kt"])], axis=0))
            for it in items}
    n_ab = {it: jnp.where(strict, amat[it][0:c, 0:LANES], 0.0) for it in items}
    a_ak = {it: jnp.where(strict, amat[it][0:c, LANES:2 * LANES], 0.0) for it in items}
    a_r2 = {it: jnp.concatenate(
        [jnp.where(incl, amat[it][c:2 * c, 0:LANES], 0.0),
         jnp.where(incl, amat[it][c:2 * c, LANES:2 * LANES], 0.0)], axis=1) for it in items}
    bdv = {it: bd(rows("v", it)) for it in items}
    u0 = {it: _dot(a_ak[it], bdv[it]) for it in items}

    dg = {it: jnp.where(same_sub, n_ab[it], 0.0) for it in items}
    off = {it: n_ab[it] - dg[it] for it in items}
    d2 = {it: pmul(dg[it], dg[it]) for it in items}
    x = {it: eye + dg[it] for it in items}
    x = {it: x[it] + pmul(x[it], d2[it]) for it in items}
    d4 = {it: pmul(d2[it], d2[it]) for it in items}
    x = {it: x[it] + pmul(x[it], d4[it]) for it in items}
    d8 = {it: pmul(d4[it], d4[it]) for it in items}
    x = {it: x[it] + pmul(x[it], d8[it]) for it in items}
    q = {it: pmul(x[it], off[it]) for it in items}
    q2 = {it: pmul(q[it], q[it]) for it in items}
    z = {it: x[it] + pmul(q[it], x[it]) for it in items}
    t_inv = {it: z[it] + pmul(q2[it], z[it]) for it in items}

    ys = {}
    for ch in range(n_chunks):
        its = [(b, ch) for b in range(nb)]
        st = {it: state_ref[it[0]] for it in its}
        sar = {it: _dot_nt(lhs2[it], st[it]) for it in its}
        p = {it: pmul(t_inv[it], sar[it][0:c] + u0[it]) for it in its}
        zz = {it: _dot_tn(jnp.concatenate([p[it], rows("v", it)], axis=0),
                          jnp.concatenate([pc[it]["bh"], pc[it]["kh"]], axis=0)) for it in its}
        for it in its:
            state_ref[it[0]] = (st[it] * pc[it]["w_end"]
                                + jnp.where(bd_mask, zz[it], 0.0))
        for it in its:
            ys[it] = sar[it][c:2 * c] + _dot(
                a_r2[it], jnp.concatenate([bd(p[it]), bdv[it]], axis=0))

    for b in range(nb):
        y = jnp.concatenate([ys[(b, ch)] for ch in range(n_chunks)], axis=0)
        pb = pre[b]
        mean = _pair_sum(y) * (1.0 / RW_HEAD)
        dlt = y - mean
        var = _pair_sum(dlt * dlt) * (1.0 / RW_HEAD)
        on = dlt * lax.rsqrt(var + RW_LN_EPS) * lnw_ref[...] + lnb_ref[...]
        bonus = _pair_sum(pb["r"] * pb["kmod"] * rk_ref[...]) * pb["v"]
        o_ref[b] = ((on + bonus) * pb["g"]) * _sigmoid(ga_ref[b].astype(F32))


def _rwkv(cols3, p):
    B, S, _ = cols3.shape
    ts = _pick(S, (256, 128, 64))
    nt = S // ts
    n_hp = RW_WIDTH // LANES
    cb = lambda col: col // LANES

    def colspec(col0):
        return pl.BlockSpec((B, ts, LANES), lambda h, t: (0, t, cb(col0) + h))

    vec = pl.BlockSpec((1, LANES), lambda h, t: (0, h))
    mat = lambda nrows: pl.BlockSpec((nrows, LANES), lambda h, t: (0, h))
    full = lambda shape: pl.BlockSpec(shape, lambda h, t: (0, 0))
    tri = jnp.asarray(np.tril(np.ones((RW_CHUNK, RW_CHUNK), np.float32)), dtype=BF16)
    return pl.pallas_call(
        _rwkv_kernel,
        grid=(n_hp, nt),
        in_specs=[
            colspec(COL_R), colspec(COL_K), colspec(COL_V),
            pl.BlockSpec((B, ts, RW_LO_PAD), lambda h, t: (0, t, COL_LO // RW_LO_PAD)),
            colspec(COL_GA),
            vec, vec, vec, full((1, RW_LO_PAD)),
            vec, mat(RW_DECAY_RANK), vec, mat(RW_ICLR_RANK), mat(RW_GATE_RANK),
            vec, vec, vec, vec, vec,
            full(tri.shape),
        ],
        out_specs=pl.BlockSpec((B, ts, LANES), lambda h, t: (0, t, h)),
        out_shape=jax.ShapeDtypeStruct((B, S, RW_WIDTH), F32),
        scratch_shapes=[
            pltpu.VMEM((B, LANES, LANES), F32),
            pltpu.VMEM((B, 8, 3 * LANES + RW_LO_PAD), F32),
        ],
        compiler_params=_cparams(("parallel", "arbitrary")),
        name="rwkv7",
    )(cols3, cols3, cols3, cols3, cols3,
      p["mu_r"], p["mu_k"], p["mu_v"], p["mu_lo"],
      p["w0"], p["w2"], p["a0"], p["a2"], p["g2"],
      p["k_k"], p["k_a"], p["r_k"], p["ln_w"], p["ln_b"], tri)


def _hg_level_matrix():
    c = HG_CHUNK
    m = np.zeros((HG_LEVELS + 1, c, c), np.float32)
    for l in range(HG_LEVELS):
        bs = c >> l
        half = bs // 2
        for t in range(c):
            mid = (t // bs) * bs + half
            if t % bs >= half:
                m[l, t, mid:t + 1] = 1.0
            else:
                m[l, t, t + 1:mid] = 1.0
    m[HG_LEVELS] = np.tril(np.ones((c, c), np.float32))
    return m.reshape((HG_LEVELS + 1) * c, c)


def _hgrn_kernel(q_ref, f_ref, i_ref, g_ref, gb_ref, lbl_ref, ng_ref, lvl_ref,
                 o_ref, state_ref):
    nb, ts, _ = q_ref.shape
    c = HG_CHUNK
    n_chunks = ts // c
    items = [(b, ch) for b in range(nb) for ch in range(n_chunks)]

    @pl.when(pl.program_id(1) == 0)
    def _():
        state_ref[...] = jnp.zeros_like(state_ref)

    logits = lbl_ref[...]
    mx = jnp.max(logits, axis=0, keepdims=True)
    ex = jnp.exp(logits - mx)
    lb = ex[0:1, :] / jnp.sum(ex, axis=0, keepdims=True)

    ri = lax.broadcasted_iota(jnp.int32, (c, c), 0)
    ci = lax.broadcasted_iota(jnp.int32, (c, c), 1)
    rowi = lax.broadcasted_iota(jnp.int32, (c, LANES), 0)
    lvl = lvl_ref[...]

    pre = []
    for b in range(nb):
        f = lb + (1.0 - lb) * _sigmoid(f_ref[b].astype(F32))
        qraw = q_ref[b].astype(F32)
        pre.append(dict(lf=jnp.log(f), kx=1.0 - f, qs=qraw * _sigmoid(qraw), iv=i_ref[b]))

    def rows(name, it):
        b, ch = it
        return pre[b][name][ch * c:(ch + 1) * c]

    part = {it: _exact_sums(lvl, rows("lf", it)) for it in items}
    scores = {}
    for it in items:
        q_c, k_c = rows("qs", it), rows("kx", it)
        dsum = jnp.sum(q_c * k_c, axis=-1, keepdims=True)
        sc_sum = jnp.where(ri == ci, dsum, 0.0)
        for l in range(HG_LEVELS):
            bs = c >> l
            e = jnp.exp(part[it][l * c:(l + 1) * c])
            second = (rowi & (bs - 1)) >= (bs // 2)
            qh = jnp.where(second, q_c * e, 0.0)
            kh = jnp.where(second, 0.0, k_c * e)
            sc = _dot_nt(qh, kh)
            if l > 0:
                sc = jnp.where((ri // bs) == (ci // bs), sc, 0.0)
            sc_sum = sc_sum + sc
        scores[it] = sc_sum
    o_intra = {it: _dot(scores[it], rows("iv", it)) for it in items}
    bcum = {it: part[it][HG_LEVELS * c:(HG_LEVELS + 1) * c] for it in items}
    zc = {it: _dot_tn(rows("iv", it),
                      rows("kx", it) * jnp.exp(bcum[it][c - 1:c, :] - bcum[it])) for it in items}
    qb = {it: rows("qs", it) * jnp.exp(bcum[it]) for it in items}

    outs = {}
    for b in range(nb):
        st = state_ref[b]
        for ch in range(n_chunks):
            it = (b, ch)
            outs[it] = o_intra[it] + _dot_nt(qb[it], st)
            st = st * jnp.exp(bcum[it][c - 1:c, :]) + zc[it]
        state_ref[b] = st

    ng = ng_ref[...]
    for b in range(nb):
        o = jnp.concatenate([outs[(b, ch)] for ch in range(n_chunks)], axis=0)
        o = o * lax.rsqrt(jnp.mean(o * o, axis=-1, keepdims=True) + NORM_EPS) * ng
        graw = g_ref[b].astype(F32)
        o_ref[b] = o * (graw * _sigmoid(graw)) * _sigmoid(gb_ref[b].astype(F32))


def _hgrn(cols3, lb_logits, norm_g):
    B, S, _ = cols3.shape
    ts = _pick(S, (256, 128, 64))
    nt = S // ts
    cb = lambda col: col // LANES

    def colspec(col0):
        return pl.BlockSpec((B, ts, LANES), lambda h, t: (0, t, cb(col0) + h))

    n_slots = lb_logits.shape[0]
    lvl = jnp.asarray(_hg_level_matrix(), dtype=BF16)
    return pl.pallas_call(
        _hgrn_kernel,
        grid=(HG_HEADS, nt),
        in_specs=[
            colspec(COL_HQ), colspec(COL_HF), colspec(COL_HI), colspec(COL_HG), colspec(COL_GB),
            pl.BlockSpec((n_slots, LANES), lambda h, t: (0, h)),
            pl.BlockSpec((1, LANES), lambda h, t: (0, 0)),
            pl.BlockSpec(lvl.shape, lambda h, t: (0, 0)),
        ],
        out_specs=pl.BlockSpec((B, ts, LANES), lambda h, t: (0, t, h)),
        out_shape=jax.ShapeDtypeStruct((B, S, D_MODEL), F32),
        scratch_shapes=[pltpu.VMEM((B, HG_KEY, HG_KEY), F32)],
        compiler_params=_cparams(("parallel", "arbitrary")),
        name="hgrn2",
    )(cols3, cols3, cols3, cols3, cols3, lb_logits, norm_g, lvl)


def _outproj_kernel(x_ref, ya_ref, yb_ref, w_ref, o_ref):
    y = ya_ref[...] + yb_ref[...]
    o_ref[...] = x_ref[...] + _dot(y, w_ref[...])


def _outproj(x2, ya, yb, w):
    T = x2.shape[0]
    tm = _pick(T, (512, 256))
    row = pl.BlockSpec((tm, D_MODEL), lambda i: (i, 0))
    return pl.pallas_call(
        _outproj_kernel,
        grid=(T // tm,),
        in_specs=[row, row, row, pl.BlockSpec((D_MODEL, D_MODEL), lambda i: (0, 0))],
        out_specs=row,
        out_shape=jax.ShapeDtypeStruct((T, D_MODEL), F32),
        compiler_params=_cparams(("parallel",)),
        name="outproj",
    )(x2, ya, yb, w)


def _memkv_kernel(m_ref, g_ref, wk_ref, wv_ref, k_ref, v_ref):
    m = _rms(m_ref[...], g_ref[...]).astype(BF16)
    k_ref[...] = jnp.dot(m, wk_ref[...], preferred_element_type=F32).astype(BF16)
    v_ref[...] = jnp.dot(m, wv_ref[...], preferred_element_type=F32).astype(BF16)


def _memkv(mem2, g, wk, wv, B, n_mem):
    row = pl.BlockSpec((n_mem, D_MODEL), lambda b: (b, 0))
    wspec = pl.BlockSpec((D_MODEL, D_MODEL), lambda b: (0, 0))
    sds = jax.ShapeDtypeStruct((B * n_mem, D_MODEL), BF16)
    return pl.pallas_call(
        _memkv_kernel,
        grid=(B,),
        in_specs=[row, pl.BlockSpec((1, D_MODEL), lambda b: (0, 0)), wspec, wspec],
        out_specs=[row, row],
        out_shape=[sds, sds],
        compiler_params=_cparams(("parallel",)),
        name="memkv",
    )(mem2, g, wk, wv)


def _xattn_kernel(h_ref, g_ref, wq_ref, k_ref, v_ref, wo_ref, o_ref):
    h = h_ref[...]
    u = _rms(h, g_ref[...]).astype(BF16)
    q = jnp.dot(u, wq_ref[...], preferred_element_type=F32)
    k = k_ref[...]
    v = v_ref[...]
    heads = []
    for hd in range(XA_HEADS):
        sl = slice(hd * XA_HEAD, (hd + 1) * XA_HEAD)
        s = _dot_nt(q[:, sl], k[:, sl]) * (XA_HEAD ** -0.5)
        s = s - jnp.max(s, axis=-1, keepdims=True)
        e = jnp.exp(s)
        p = e / jnp.sum(e, axis=-1, keepdims=True)
        heads.append(_dot(p, v[:, sl]))
    o = jnp.concatenate(heads, axis=-1)
    o_ref[...] = h + _dot(o, wo_ref[...])


def _xattn(h1, g, wq, km, vm, wo, B, S, n_mem):
    tm = _pick(S, (512, 256))
    nt = S // tm
    row = pl.BlockSpec((tm, D_MODEL), lambda b, t: (b * nt + t, 0))
    wspec = pl.BlockSpec((D_MODEL, D_MODEL), lambda b, t: (0, 0))
    kv = pl.BlockSpec((n_mem, D_MODEL), lambda b, t: (b, 0))
    return pl.pallas_call(
        _xattn_kernel,
        grid=(B, nt),
        in_specs=[row, pl.BlockSpec((1, D_MODEL), lambda b, t: (0, 0)), wspec, kv, kv, wspec],
        out_specs=row,
        out_shape=jax.ShapeDtypeStruct((B * S, D_MODEL), F32),
        compiler_params=_cparams(("parallel", "parallel")),
        name="xattn",
    )(h1, g, wq, km, vm, wo)


def _ffn_kernel(h_ref, g_ref, w1_ref, w3_ref, w2_ref, gf_ref, o_ref, u_ref, acc_ref):
    j = pl.program_id(1)

    @pl.when(j == 0)
    def _():
        u_ref[...] = _rms(h_ref[...], g_ref[...]).astype(BF16)
        acc_ref[...] = h_ref[...]

    u = u_ref[...]
    a = jnp.dot(u, w1_ref[...], preferred_element_type=F32)
    b = jnp.dot(u, w3_ref[...], preferred_element_type=F32)
    mid = (a * _sigmoid(a)) * b
    acc_ref[...] += _dot(mid, w2_ref[...])

    @pl.when(j == pl.num_programs(1) - 1)
    def _():
        o_ref[...] = _rms(acc_ref[...], gf_ref[...])


def _ffn(h2, g, w1, w3, w2, gf):
    T = h2.shape[0]
    tm = _pick(T, (512, 256))
    tf = D_FF // 2
    row = pl.BlockSpec((tm, D_MODEL), lambda i, j: (i, 0))
    vec = pl.BlockSpec((1, D_MODEL), lambda i, j: (0, 0))
    return pl.pallas_call(
        _ffn_kernel,
        grid=(T // tm, D_FF // tf),
        in_specs=[row, vec,
                  pl.BlockSpec((D_MODEL, tf), lambda i, j: (0, j)),
                  pl.BlockSpec((D_MODEL, tf), lambda i, j: (0, j)),
                  pl.BlockSpec((tf, D_MODEL), lambda i, j: (j, 0)),
                  vec],
        out_specs=row,
        out_shape=jax.ShapeDtypeStruct((T, D_MODEL), F32),
        scratch_shapes=[pltpu.VMEM((tm, D_MODEL), BF16), pltpu.VMEM((tm, D_MODEL), F32)],
        compiler_params=_cparams(("parallel", "arbitrary")),
        name="ffn",
    )(h2, g, w1, w3, w2, gf)


def _pack_in_weights(w_in):
    rw_cols = 3 * RW_WIDTH + RW_LO
    pad = jnp.zeros((D_MODEL, RW_LO_PAD - RW_LO), w_in.dtype)
    packed = jnp.concatenate([w_in[:, :rw_cols], pad, w_in[:, rw_cols:]], axis=1)
    return packed.astype(BF16)


def _pack_mu_lo(mu):
    lo = mu[3 * RW_WIDTH:]
    return jnp.concatenate([lo, jnp.zeros((RW_LO_PAD - RW_LO,), mu.dtype)])[None, :]


def kernel(x, mem, norm_mix_g, w_in, rw_mu, rw_w0, rw_w2, rw_a0, rw_a2, rw_g2, rw_k_k, rw_k_a, rw_r_k, rw_ln_w, rw_ln_b, hg_lb_logits, hg_norm_g, w_out, norm_xa_g, norm_mem_g, xa_wq, xa_wk, xa_wv, xa_wo, norm_ffn_g, ffn_w1, ffn_w3, ffn_w2, norm_final_g):
    B, S, _ = x.shape
    n_mem = mem.shape[1]
    depth = w_in.shape[0]
    assert depth == 1, "single-layer block"
    assert S % RW_CHUNK == 0 and S % HG_CHUNK == 0
    l = 0
    T = B * S
    row = lambda a: a.reshape(1, -1).astype(F32)

    x2 = x.reshape(T, D_MODEL)
    cols = _inproj(x2, row(norm_mix_g[l]), _pack_in_weights(w_in[l]))
    cols3 = cols.reshape(B, S, N_COLS)

    mu = rw_mu[l]
    rw = dict(
        mu_r=row(mu[COL_R:COL_K]), mu_k=row(mu[COL_K:COL_V]), mu_v=row(mu[COL_V:COL_LO]),
        mu_lo=_pack_mu_lo(mu),
        w0=row(rw_w0[l]), w2=rw_w2[l].astype(BF16), a0=row(rw_a0[l]), a2=rw_a2[l].astype(BF16),
        g2=rw_g2[l].astype(BF16), k_k=row(rw_k_k[l]), k_a=row(rw_k_a[l]), r_k=row(rw_r_k[l]),
        ln_w=row(rw_ln_w[l]), ln_b=row(rw_ln_b[l]))
    ya = _rwkv(cols3, rw).reshape(T, D_MODEL)
    yb = _hgrn(cols3, hg_lb_logits.astype(F32), row(hg_norm_g[l])).reshape(T, D_MODEL)

    h1 = _outproj(x2, ya, yb, w_out[l].astype(BF16))

    km, vm = _memkv(mem.reshape(B * n_mem, D_MODEL), row(norm_mem_g[l]),
                    xa_wk[l].astype(BF16), xa_wv[l].astype(BF16), B, n_mem)
    h2 = _xattn(h1, row(norm_xa_g[l]), xa_wq[l].astype(BF16), km, vm,
                xa_wo[l].astype(BF16), B, S, n_mem)

    out = _ffn(h2, row(norm_ffn_g[l]), ffn_w1[l].astype(BF16), ffn_w3[l].astype(BF16),
               ffn_w2[l].astype(BF16), row(norm_final_g))
    return out.reshape(B, S, D_MODEL)
```

```python
import functools

import numpy as np
import jax
import jax.numpy as jnp
from jax import lax
from jax.experimental import pallas as pl
from jax.experimental.pallas import tpu as pltpu

F32 = jnp.float32
BF16 = jnp.bfloat16

D_MODEL = 1024
NORM_EPS = 1e-6
LANES = 128

RW_HEAD = 64
RW_WIDTH = D_MODEL
RW_DECAY_RANK = 64
RW_ICLR_RANK = 64
RW_GATE_RANK = 160
RW_LN_EPS = 64e-5
RW_LO = RW_DECAY_RANK + RW_ICLR_RANK + RW_GATE_RANK
RW_LO_PAD = 512
RW_CHUNK = 64
RW_SUB = 16
LOG2E = 1.4426950408889634

HG_KEY = 128
HG_HEADS = D_MODEL // HG_KEY
HG_CHUNK = 64
HG_LEVELS = 6

XA_HEADS = 4
XA_HEAD = D_MODEL // XA_HEADS
D_FF = 2816

COL_R = 0
COL_K = COL_R + RW_WIDTH
COL_V = COL_K + RW_WIDTH
COL_LO = COL_V + RW_WIDTH
COL_HQ = COL_LO + RW_LO_PAD
COL_HF = COL_HQ + D_MODEL
COL_HI = COL_HF + D_MODEL
COL_HG = COL_HI + D_MODEL
COL_GA = COL_HG + D_MODEL
COL_GB = COL_GA + D_MODEL
N_COLS = COL_GB + D_MODEL

COLS_DTYPE = BF16

VMEM_LIMIT = 56 * 1024 * 1024


def _cparams(sem):
    return pltpu.CompilerParams(dimension_semantics=sem, vmem_limit_bytes=VMEM_LIMIT)


def _sigmoid(x):
    return 1.0 / (1.0 + jnp.exp(-x))


def _softplus(x):
    return jnp.maximum(x, 0.0) + jnp.log(1.0 + jnp.exp(-jnp.abs(x)))


def _rms(x, g):
    ms = jnp.mean(x * x, axis=-1, keepdims=True)
    return x * lax.rsqrt(ms + NORM_EPS) * g


def _dot(a, b, prec=None):
    if prec is None:
        a = a.astype(BF16)
        b = b.astype(BF16)
    return jnp.dot(a, b, preferred_element_type=F32, precision=prec)


def _dot_nt(a, b, prec=None):
    if prec is None:
        a = a.astype(BF16)
        b = b.astype(BF16)
    return lax.dot_general(a, b, (((1,), (1,)), ((), ())),
                           preferred_element_type=F32, precision=prec)


def _dot_tn(a, b, prec=None):
    if prec is None:
        a = a.astype(BF16)
        b = b.astype(BF16)
    return lax.dot_general(a, b, (((0,), (0,)), ((), ())),
                           preferred_element_type=F32, precision=prec)


def _split3(x):
    hi = x.astype(BF16)
    r1 = x - hi.astype(F32)
    mid = r1.astype(BF16)
    lo = (r1 - mid.astype(F32)).astype(BF16)
    return jnp.concatenate([hi, mid, lo], axis=1)


def _exact_sums(pattern_bf16, x):
    y = jnp.dot(pattern_bf16, _split3(x), preferred_element_type=F32)
    return y[:, 0:LANES] + y[:, LANES:2 * LANES] + y[:, 2 * LANES:3 * LANES]


def _pick(n, prefs):
    for p in prefs:
        if n % p == 0:
            return p
    return n


def _shift_mix(raw, carry_row, mu):
    rolled = pltpu.roll(raw, 1, axis=0)
    row = lax.broadcasted_iota(jnp.int32, raw.shape, 0)
    prev = jnp.where(row == 0, carry_row, rolled)
    return raw + mu * (prev - raw)


def _inproj_kernel(tiles_per_seq, x_ref, g_ref, w_ref, mu_ref, o_ref, u_ref, carry_ref):
    i, j = pl.program_id(0), pl.program_id(1)
    tm, tn = o_ref.shape

    @pl.when(j == 0)
    def _():
        u_ref[...] = _rms(x_ref[...], g_ref[...]).astype(BF16)

    raw = jnp.dot(u_ref[...], w_ref[...], preferred_element_type=F32)
    has_shift = j * tn < COL_HQ

    @pl.when(has_shift)
    def _():
        first = (i % tiles_per_seq) == 0
        carry = jnp.where(first, 0.0, carry_ref[j, 0:1, :])
        o_ref[...] = _shift_mix(raw, carry, mu_ref[...]).astype(o_ref.dtype)
        carry_ref[j, 0:1, :] = raw[tm - 1:tm, :]

    @pl.when(jnp.logical_not(has_shift))
    def _():
        o_ref[...] = raw.astype(o_ref.dtype)


def _inproj(x2, g, w_packed, mu_packed, S):
    T = x2.shape[0]
    tm = _pick(S, (1024, 512, 256))
    n_j = 4
    tn = N_COLS // n_j
    return pl.pallas_call(
        functools.partial(_inproj_kernel, S // tm),
        grid=(T // tm, n_j),
        in_specs=[
            pl.BlockSpec((tm, D_MODEL), lambda i, j: (i, 0)),
            pl.BlockSpec((1, D_MODEL), lambda i, j: (0, 0)),
            pl.BlockSpec((D_MODEL, tn), lambda i, j: (0, j)),
            pl.BlockSpec((1, tn), lambda i, j: (0, j)),
        ],
        out_specs=pl.BlockSpec((tm, tn), lambda i, j: (i, j)),
        out_shape=jax.ShapeDtypeStruct((T, N_COLS), COLS_DTYPE),
        scratch_shapes=[pltpu.VMEM((tm, D_MODEL), BF16), pltpu.VMEM((n_j, 8, tn), F32)],
        compiler_params=_cparams(("arbitrary", "arbitrary")),
        name="inproj",
    )(x2, g, w_packed, mu_packed)


def _pair_sum(x):
    lane = lax.broadcasted_iota(jnp.int32, x.shape, 1)
    first = lane < RW_HEAD
    s0 = jnp.sum(jnp.where(first, x, 0.0), axis=-1, keepdims=True)
    s1 = jnp.sum(jnp.where(first, 0.0, x), axis=-1, keepdims=True)
    return jnp.where(first, s0, s1)


def _rwkv_kernel(r_ref, k_ref, v_ref, lo_ref, ga_ref,
                 w0_ref, w2_ref, a0_ref, a2_ref, g2_ref,
                 kk_ref, ka_ref, rk_ref, lnw_ref, lnb_ref, tri_ref,
                 o_ref, state_ref):
    nb, ts, _ = r_ref.shape
    c = RW_CHUNK
    n_chunks = ts // c
    items = [(b, ch) for b in range(nb) for ch in range(n_chunks)]

    @pl.when(pl.program_id(1) == 0)
    def _():
        state_ref[...] = jnp.zeros_like(state_ref)

    t_i = lax.broadcasted_iota(jnp.int32, (c, LANES), 0)
    s_i = lax.broadcasted_iota(jnp.int32, (c, LANES), 1) & (RW_HEAD - 1)
    strict = t_i > s_i
    incl = t_i >= s_i
    eye = (t_i == s_i).astype(F32)
    same_sub = (t_i // RW_SUB) == (s_i // RW_SUB)
    rr = lax.broadcasted_iota(jnp.int32, (LANES, LANES), 0)
    cc = lax.broadcasted_iota(jnp.int32, (LANES, LANES), 1)
    bd_mask = (rr // RW_HEAD) == (cc // RW_HEAD)

    def bd(x):
        return jnp.where(bd_mask, jnp.concatenate([x, x], axis=0), 0.0).astype(BF16)

    def pmul(a, bmat):
        return _dot(a, bd(bmat))

    pre = []
    for b in range(nb):
        r, k, v = (ref[b].astype(F32) for ref in (r_ref, k_ref, v_ref))
        lo = lo_ref[b]
        w_lo = lo[:, 0:RW_DECAY_RANK].astype(F32)
        a_lo = lo[:, RW_DECAY_RANK:RW_DECAY_RANK + RW_ICLR_RANK]
        g_lo = lo[:, 128:128 + RW_GATE_RANK].astype(F32)
        wl = w0_ref[...] + _dot(jnp.tanh(w_lo), w2_ref[...])
        lw = -LOG2E * jnp.exp(-_softplus(-wl) - 0.5)
        a = _sigmoid(a0_ref[...] + _dot(a_lo, a2_ref[...]))
        g = _dot(_sigmoid(g_lo), g2_ref[...])

        kk = k * kk_ref[...]
        kkn = kk * lax.rsqrt(jnp.maximum(_pair_sum(kk * kk), 1e-24))
        kmod = k * (1.0 + (a - 1.0) * ka_ref[...])
        beta = kkn * a

        pre.append(dict(r=r, v=v, kmod=kmod, g=g, lw=lw, kkn=kkn, beta=beta))

    def rows(name, it):
        b, ch = it
        return pre[b][name][ch * c:(ch + 1) * c]

    tri = tri_ref[...]
    pc = {}
    for it in items:
        lw_c = rows("lw", it)
        cum = _exact_sums(tri, lw_c)
        c_end = cum[c - 1:c, :]
        e_neg = jnp.exp2(-cum)
        e_end = jnp.exp2(c_end - cum)
        beta_c, kmod_c = rows("beta", it), rows("kmod", it)
        pc[it] = dict(
            rt=rows("r", it) * jnp.exp2(cum), at=-rows("kkn", it) * jnp.exp2(cum - lw_c),
            bt=beta_c * e_neg, kt=kmod_c * e_neg, bh=beta_c * e_end, kh=kmod_c * e_end,
            w_end=jnp.exp2(c_end))

    lhs2 = {it: jnp.concatenate([pc[it]["at"], pc[it]["rt"]], axis=0) for it in items}
    amat = {it: _dot_nt(lhs2[it], jnp.concatenate([bd(pc[it]["bt"]), bd(pc[it]["kt"])], axis=0))
            for it in items}
    n_ab = {it: jnp.where(strict, amat[it][0:c, 0:LANES], 0.0) for it in items}
    a_ak = {it: jnp.where(strict, amat[it][0:c, LANES:2 * LANES], 0.0) for it in items}
    a_r2 = {it: jnp.concatenate(
        [jnp.where(incl, amat[it][c:2 * c, 0:LANES], 0.0),
         jnp.where(incl, amat[it][c:2 * c, LANES:2 * LANES], 0.0)], axis=1) for it in items}
    bdv = {it: bd(rows("v", it)) for it in items}
    u0 = {it: _dot(a_ak[it], bdv[it]) for it in items}

    dg = {it: jnp.where(same_sub, n_ab[it], 0.0) for it in items}
    off = {it: n_ab[it] - dg[it] for it in items}
    d2 = {it: pmul(dg[it], dg[it]) for it in items}
    x = {it: eye + dg[it] for it in items}
    x = {it: x[it] + pmul(x[it], d2[it]) for it in items}
    d4 = {it: pmul(d2[it], d2[it]) for it in items}
    x = {it: x[it] + pmul(x[it], d4[it]) for it in items}
    d8 = {it: pmul(d4[it], d4[it]) for it in items}
    x = {it: x[it] + pmul(x[it], d8[it]) for it in items}
    q = {it: pmul(x[it], off[it]) for it in items}
    q2 = {it: pmul(q[it], q[it]) for it in items}
    z = {it: x[it] + pmul(q[it], x[it]) for it in items}
    t_inv = {it: z[it] + pmul(q2[it], z[it]) for it in items}

    ys = {}
    for ch in range(n_chunks):
        its = [(b, ch) for b in range(nb)]
        st = {it: state_ref[it[0]] for it in its}
        sar = {it: _dot_nt(lhs2[it], st[it]) for it in its}
        p = {it: pmul(t_inv[it], sar[it][0:c] + u0[it]) for it in its}
        zz = {it: _dot_tn(jnp.concatenate([p[it], rows("v", it)], axis=0),
                          jnp.concatenate([pc[it]["bh"], pc[it]["kh"]], axis=0)) for it in its}
        for it in its:
            state_ref[it[0]] = (st[it] * pc[it]["w_end"]
                                + jnp.where(bd_mask, zz[it], 0.0))
        for it in its:
            ys[it] = sar[it][c:2 * c] + _dot(
                a_r2[it], jnp.concatenate([bd(p[it]), bdv[it]], axis=0))

    for b in range(nb):
        y = jnp.concatenate([ys[(b, ch)] for ch in range(n_chunks)], axis=0)
        pb = pre[b]
        mean = _pair_sum(y) * (1.0 / RW_HEAD)
        dlt = y - mean
        var = _pair_sum(dlt * dlt) * (1.0 / RW_HEAD)
        on = dlt * lax.rsqrt(var + RW_LN_EPS) * lnw_ref[...] + lnb_ref[...]
        bonus = _pair_sum(pb["r"] * pb["kmod"] * rk_ref[...]) * pb["v"]
        o_ref[b] = ((on + bonus) * pb["g"]) * _sigmoid(ga_ref[b].astype(F32))


def _rwkv(cols3, p):
    B, S, _ = cols3.shape
    ts = _pick(S, (256, 128, 64))
    nt = S // ts
    n_hp = RW_WIDTH // LANES
    cb = lambda col: col // LANES

    def colspec(col0):
        return pl.BlockSpec((B, ts, LANES), lambda h, t: (0, t, cb(col0) + h))

    vec = pl.BlockSpec((1, LANES), lambda h, t: (0, h))
    mat = lambda nrows: pl.BlockSpec((nrows, LANES), lambda h, t: (0, h))
    full = lambda shape: pl.BlockSpec(shape, lambda h, t: (0, 0))
    tri = jnp.asarray(np.tril(np.ones((RW_CHUNK, RW_CHUNK), np.float32)), dtype=BF16)
    return pl.pallas_call(
        _rwkv_kernel,
        grid=(n_hp, nt),
        in_specs=[
            colspec(COL_R), colspec(COL_K), colspec(COL_V),
            pl.BlockSpec((B, ts, RW_LO_PAD), lambda h, t: (0, t, COL_LO // RW_LO_PAD)),
            colspec(COL_GA),
            vec, mat(RW_DECAY_RANK), vec, mat(RW_ICLR_RANK), mat(RW_GATE_RANK),
            vec, vec, vec, vec, vec,
            full(tri.shape),
        ],
        out_specs=pl.BlockSpec((B, ts, LANES), lambda h, t: (0, t, h)),
        out_shape=jax.ShapeDtypeStruct((B, S, RW_WIDTH), F32),
        scratch_shapes=[pltpu.VMEM((B, LANES, LANES), F32)],
        compiler_params=_cparams(("parallel", "arbitrary")),
        name="rwkv7",
    )(cols3, cols3, cols3, cols3, cols3,
      p["w0"], p["w2"], p["a0"], p["a2"], p["g2"],
      p["k_k"], p["k_a"], p["r_k"], p["ln_w"], p["ln_b"], tri)


def _hg_level_matrix():
    c = HG_CHUNK
    m = np.zeros((HG_LEVELS + 1, c, c), np.float32)
    for l in range(HG_LEVELS):
        bs = c >> l
        half = bs // 2
        for t in range(c):
            mid = (t // bs) * bs + half
            if t % bs >= half:
                m[l, t, mid:t + 1] = 1.0
            else:
                m[l, t, t + 1:mid] = 1.0
    m[HG_LEVELS] = np.tril(np.ones((c, c), np.float32))
    return m.reshape((HG_LEVELS + 1) * c, c)


def _hgrn_kernel(q_ref, f_ref, i_ref, g_ref, gb_ref, lbl_ref, ng_ref, lvl_ref,
                 o_ref, state_ref):
    nb, ts, _ = q_ref.shape
    c = HG_CHUNK
    n_chunks = ts // c
    items = [(b, ch) for b in range(nb) for ch in range(n_chunks)]

    @pl.when(pl.program_id(1) == 0)
    def _():
        state_ref[...] = jnp.zeros_like(state_ref)

    logits = lbl_ref[...]
    mx = jnp.max(logits, axis=0, keepdims=True)
    ex = jnp.exp(logits - mx)
    lb = ex[0:1, :] / jnp.sum(ex, axis=0, keepdims=True)

    ri = lax.broadcasted_iota(jnp.int32, (c, c), 0)
    ci = lax.broadcasted_iota(jnp.int32, (c, c), 1)
    rowi = lax.broadcasted_iota(jnp.int32, (c, LANES), 0)
    lvl = lvl_ref[...]
    second, valid = [], []
    for l in range(HG_LEVELS):
        bs = c >> l
        second.append((rowi & (bs - 1)) >= (bs // 2))
        valid.append(((ri // bs) == (ci // bs)) & ((ri & (bs - 1)) >= (bs // 2))
                     & ((ci & (bs - 1)) < (bs // 2)))

    pre = []
    for b in range(nb):
        f = lb + (1.0 - lb) * _sigmoid(f_ref[b].astype(F32))
        qraw = q_ref[b].astype(F32)
        pre.append(dict(lf=LOG2E * jnp.log(f), kx=1.0 - f, qs=qraw * _sigmoid(qraw), iv=i_ref[b]))

    def rows(name, it):
        b, ch = it
        return pre[b][name][ch * c:(ch + 1) * c]

    part = {it: _exact_sums(lvl, rows("lf", it)) for it in items}
    scores = {}
    for it in items:
        q_c, k_c = rows("qs", it), rows("kx", it)
        dsum = jnp.sum(q_c * k_c, axis=-1, keepdims=True)
        sc_sum = jnp.where(ri == ci, dsum, 0.0)
        for l in range(HG_LEVELS):
            qk = (jnp.where(second[l], q_c, k_c) * jnp.exp2(part[it][l * c:(l + 1) * c])).astype(BF16)
            sc_sum = jnp.where(valid[l], _dot_nt(qk, qk), sc_sum)
        scores[it] = sc_sum
    o_intra = {it: _dot(scores[it], rows("iv", it)) for it in items}
    bcum = {it: part[it][HG_LEVELS * c:(HG_LEVELS + 1) * c] for it in items}
    zc = {it: _dot_tn(rows("iv", it),
                      rows("kx", it) * jnp.exp2(bcum[it][c - 1:c, :] - bcum[it])) for it in items}
    qb = {it: rows("qs", it) * jnp.exp2(bcum[it]) for it in items}

    outs = {}
    for b in range(nb):
        st = state_ref[b]
        for ch in range(n_chunks):
            it = (b, ch)
            outs[it] = o_intra[it] + _dot_nt(qb[it], st)
            st = st * jnp.exp2(bcum[it][c - 1:c, :]) + zc[it]
        state_ref[b] = st

    ng = ng_ref[...]
    for b in range(nb):
        o = jnp.concatenate([outs[(b, ch)] for ch in range(n_chunks)], axis=0)
        o = o * lax.rsqrt(jnp.mean(o * o, axis=-1, keepdims=True) + NORM_EPS) * ng
        graw = g_ref[b].astype(F32)
        o_ref[b] = o * (graw * _sigmoid(graw)) * _sigmoid(gb_ref[b].astype(F32))


def _hgrn(cols3, lb_logits, norm_g):
    B, S, _ = cols3.shape
    ts = _pick(S, (256, 128, 64))
    nt = S // ts
    cb = lambda col: col // LANES

    def colspec(col0):
        return pl.BlockSpec((B, ts, LANES), lambda h, t: (0, t, cb(col0) + h))

    n_slots = lb_logits.shape[0]
    lvl = jnp.asarray(_hg_level_matrix(), dtype=BF16)
    return pl.pallas_call(
        _hgrn_kernel,
        grid=(HG_HEADS, nt),
        in_specs=[
            colspec(COL_HQ), colspec(COL_HF), colspec(COL_HI), colspec(COL_HG), colspec(COL_GB),
            pl.BlockSpec((n_slots, LANES), lambda h, t: (0, h)),
            pl.BlockSpec((1, LANES), lambda h, t: (0, 0)),
            pl.BlockSpec(lvl.shape, lambda h, t: (0, 0)),
        ],
        out_specs=pl.BlockSpec((B, ts, LANES), lambda h, t: (0, t, h)),
        out_shape=jax.ShapeDtypeStruct((B, S, D_MODEL), F32),
        scratch_shapes=[pltpu.VMEM((B, HG_KEY, HG_KEY), F32)],
        compiler_params=_cparams(("parallel", "arbitrary")),
        name="hgrn2",
    )(cols3, cols3, cols3, cols3, cols3, lb_logits, norm_g, lvl)


def _outproj_kernel(x_ref, ya_ref, yb_ref, w_ref, o_ref):
    y = ya_ref[...] + yb_ref[...]
    o_ref[...] = x_ref[...] + _dot(y, w_ref[...])


def _outproj(x2, ya, yb, w):
    T = x2.shape[0]
    tm = _pick(T, (512, 256))
    row = pl.BlockSpec((tm, D_MODEL), lambda i: (i, 0))
    return pl.pallas_call(
        _outproj_kernel,
        grid=(T // tm,),
        in_specs=[row, row, row, pl.BlockSpec((D_MODEL, D_MODEL), lambda i: (0, 0))],
        out_specs=row,
        out_shape=jax.ShapeDtypeStruct((T, D_MODEL), F32),
        compiler_params=_cparams(("parallel",)),
        name="outproj",
    )(x2, ya, yb, w)


def _memkv_kernel(m_ref, g_ref, wk_ref, wv_ref, k_ref, v_ref):
    m = _rms(m_ref[...], g_ref[...]).astype(BF16)
    k_ref[...] = jnp.dot(m, wk_ref[...], preferred_element_type=F32).astype(BF16)
    v_ref[...] = jnp.dot(m, wv_ref[...], preferred_element_type=F32).astype(BF16)


def _memkv(mem2, g, wk, wv, B, n_mem):
    row = pl.BlockSpec((n_mem, D_MODEL), lambda b: (b, 0))
    wspec = pl.BlockSpec((D_MODEL, D_MODEL), lambda b: (0, 0))
    sds = jax.ShapeDtypeStruct((B * n_mem, D_MODEL), BF16)
    return pl.pallas_call(
        _memkv_kernel,
        grid=(B,),
        in_specs=[row, pl.BlockSpec((1, D_MODEL), lambda b: (0, 0)), wspec, wspec],
        out_specs=[row, row],
        out_shape=[sds, sds],
        compiler_params=_cparams(("parallel",)),
        name="memkv",
    )(mem2, g, wk, wv)


def _xattn_kernel(h_ref, g_ref, wq_ref, k_ref, v_ref, wo_ref, o_ref):
    h = h_ref[...]
    u = _rms(h, g_ref[...]).astype(BF16)
    q = jnp.dot(u, wq_ref[...], preferred_element_type=F32)
    k = k_ref[...]
    v = v_ref[...]
    heads = []
    for hd in range(XA_HEADS):
        sl = slice(hd * XA_HEAD, (hd + 1) * XA_HEAD)
        s = _dot_nt(q[:, sl], k[:, sl]) * (XA_HEAD ** -0.5)
        s = s - jnp.max(s, axis=-1, keepdims=True)
        e = jnp.exp(s)
        p = e / jnp.sum(e, axis=-1, keepdims=True)
        heads.append(_dot(p, v[:, sl]))
    o = jnp.concatenate(heads, axis=-1)
    o_ref[...] = h + _dot(o, wo_ref[...])


def _xattn(h1, g, wq, km, vm, wo, B, S, n_mem):
    tm = _pick(S, (512, 256))
    nt = S // tm
    row = pl.BlockSpec((tm, D_MODEL), lambda b, t: (b * nt + t, 0))
    wspec = pl.BlockSpec((D_MODEL, D_MODEL), lambda b, t: (0, 0))
    kv = pl.BlockSpec((n_mem, D_MODEL), lambda b, t: (b, 0))
    return pl.pallas_call(
        _xattn_kernel,
        grid=(B, nt),
        in_specs=[row, pl.BlockSpec((1, D_MODEL), lambda b, t: (0, 0)), wspec, kv, kv, wspec],
        out_specs=row,
        out_shape=jax.ShapeDtypeStruct((B * S, D_MODEL), F32),
        compiler_params=_cparams(("parallel", "parallel")),
        name="xattn",
    )(h1, g, wq, km, vm, wo)


def _ffn_kernel(h_ref, g_ref, w1_ref, w3_ref, w2_ref, gf_ref, o_ref, u_ref, acc_ref):
    j = pl.program_id(1)

    @pl.when(j == 0)
    def _():
        u_ref[...] = _rms(h_ref[...], g_ref[...]).astype(BF16)
        acc_ref[...] = h_ref[...]

    u = u_ref[...]
    a = jnp.dot(u, w1_ref[...], preferred_element_type=F32)
    b = jnp.dot(u, w3_ref[...], preferred_element_type=F32)
    mid = (a * _sigmoid(a)) * b
    acc_ref[...] += _dot(mid, w2_ref[...])

    @pl.when(j == pl.num_programs(1) - 1)
    def _():
        o_ref[...] = _rms(acc_ref[...], gf_ref[...])


def _ffn(h2, g, w1, w3, w2, gf):
    T = h2.shape[0]
    tm = _pick(T, (512, 256))
    tf = D_FF // 2
    row = pl.BlockSpec((tm, D_MODEL), lambda i, j: (i, 0))
    vec = pl.BlockSpec((1, D_MODEL), lambda i, j: (0, 0))
    return pl.pallas_call(
        _ffn_kernel,
        grid=(T // tm, D_FF // tf),
        in_specs=[row, vec,
                  pl.BlockSpec((D_MODEL, tf), lambda i, j: (0, j)),
                  pl.BlockSpec((D_MODEL, tf), lambda i, j: (0, j)),
                  pl.BlockSpec((tf, D_MODEL), lambda i, j: (j, 0)),
                  vec],
        out_specs=row,
        out_shape=jax.ShapeDtypeStruct((T, D_MODEL), F32),
        scratch_shapes=[pltpu.VMEM((tm, D_MODEL), BF16), pltpu.VMEM((tm, D_MODEL), F32)],
        compiler_params=_cparams(("parallel", "arbitrary")),
        name="ffn",
    )(h2, g, w1, w3, w2, gf)


def _pack_in_weights(w_in):
    rw_cols = 3 * RW_WIDTH + RW_LO
    pad = jnp.zeros((D_MODEL, RW_LO_PAD - RW_LO), w_in.dtype)
    packed = jnp.concatenate([w_in[:, :rw_cols], pad, w_in[:, rw_cols:]], axis=1)
    return packed.astype(BF16)


def _pack_mu(mu):
    return jnp.concatenate([mu, jnp.zeros((N_COLS - mu.shape[0],), mu.dtype)])[None, :].astype(F32)


def kernel(x, mem, norm_mix_g, w_in, rw_mu, rw_w0, rw_w2, rw_a0, rw_a2, rw_g2, rw_k_k, rw_k_a, rw_r_k, rw_ln_w, rw_ln_b, hg_lb_logits, hg_norm_g, w_out, norm_xa_g, norm_mem_g, xa_wq, xa_wk, xa_wv, xa_wo, norm_ffn_g, ffn_w1, ffn_w3, ffn_w2, norm_final_g):
    B, S, _ = x.shape
    n_mem = mem.shape[1]
    depth = w_in.shape[0]
    assert depth == 1, "single-layer block"
    assert S % RW_CHUNK == 0 and S % HG_CHUNK == 0
    l = 0
    T = B * S
    row = lambda a: a.reshape(1, -1).astype(F32)

    x2 = x.reshape(T, D_MODEL)
    cols = _inproj(x2, row(norm_mix_g[l]), _pack_in_weights(w_in[l]), _pack_mu(rw_mu[l]), S)
    cols3 = cols.reshape(B, S, N_COLS)

    rw = dict(
        w0=row(rw_w0[l]), w2=rw_w2[l].astype(BF16), a0=row(rw_a0[l]), a2=rw_a2[l].astype(BF16),
        g2=rw_g2[l].astype(BF16), k_k=row(rw_k_k[l]), k_a=row(rw_k_a[l]), r_k=row(rw_r_k[l]),
        ln_w=row(rw_ln_w[l]), ln_b=row(rw_ln_b[l]))
    ya = _rwkv(cols3, rw).reshape(T, D_MODEL)
    yb = _hgrn(cols3, hg_lb_logits.astype(F32), row(hg_norm_g[l])).reshape(T, D_MODEL)

    h1 = _outproj(x2, ya, yb, w_out[l].astype(BF16))

    km, vm = _memkv(mem.reshape(B * n_mem, D_MODEL), row(norm_mem_g[l]),
                    xa_wk[l].astype(BF16), xa_wv[l].astype(BF16), B, n_mem)
    h2 = _xattn(h1, row(norm_xa_g[l]), xa_wq[l].astype(BF16), km, vm,
                xa_wo[l].astype(BF16), B, S, n_mem)

    out = _ffn(h2, row(norm_ffn_g[l]), ffn_w1[l].astype(BF16), ffn_w3[l].astype(BF16),
               ffn_w2[l].astype(BF16), row(norm_final_g))
    return out.reshape(B, S, D_MODEL)
```

```python
import functools

import numpy as np
import jax
import jax.numpy as jnp
from jax import lax
from jax.experimental import pallas as pl
from jax.experimental.pallas import tpu as pltpu

F32 = jnp.float32
BF16 = jnp.bfloat16

D_MODEL = 1024
NORM_EPS = 1e-6
LANES = 128

RW_HEAD = 64
RW_WIDTH = D_MODEL
RW_DECAY_RANK = 64
RW_ICLR_RANK = 64
RW_GATE_RANK = 160
RW_LN_EPS = 64e-5
RW_LO = RW_DECAY_RANK + RW_ICLR_RANK + RW_GATE_RANK
RW_LO_PAD = 512
RW_LO_FEAT = 384
RW_CHUNK = 64
RW_SUB = 16
RW_GROUP = 4
LOG2E = 1.4426950408889634

HG_KEY = 128
HG_HEADS = D_MODEL // HG_KEY
HG_CHUNK = 64
HG_LEVELS = 6

XA_HEADS = 4
XA_HEAD = D_MODEL // XA_HEADS
D_FF = 2816

COL_R = 0
COL_K = COL_R + RW_WIDTH
COL_V = COL_K + RW_WIDTH
COL_LO = COL_V + RW_WIDTH
COL_HQ = COL_LO + RW_LO_PAD
COL_HF = COL_HQ + D_MODEL
COL_HI = COL_HF + D_MODEL
COL_HG = COL_HI + D_MODEL
COL_GA = COL_HG + D_MODEL
COL_GB = COL_GA + D_MODEL
N_COLS = COL_GB + D_MODEL

COLS_DTYPE = BF16

VMEM_LIMIT = 56 * 1024 * 1024


def _cparams(sem):
    return pltpu.CompilerParams(dimension_semantics=sem, vmem_limit_bytes=VMEM_LIMIT)


def _sigmoid(x):
    return 1.0 / (1.0 + jnp.exp(-x))


def _softplus(x):
    return jnp.maximum(x, 0.0) + jnp.log(1.0 + jnp.exp(-jnp.abs(x)))


def _rms(x, g):
    ms = jnp.mean(x * x, axis=-1, keepdims=True)
    return x * lax.rsqrt(ms + NORM_EPS) * g


def _dot(a, b):
    return jnp.dot(a.astype(BF16), b.astype(BF16), preferred_element_type=F32)


def _dot_nt(a, b):
    return lax.dot_general(a.astype(BF16), b.astype(BF16), (((1,), (1,)), ((), ())),
                           preferred_element_type=F32)


def _dot_tn(a, b):
    return lax.dot_general(a.astype(BF16), b.astype(BF16), (((0,), (0,)), ((), ())),
                           preferred_element_type=F32)


def _split3(x):
    hi = x.astype(BF16)
    r1 = x - hi.astype(F32)
    mid = r1.astype(BF16)
    lo = (r1 - mid.astype(F32)).astype(BF16)
    return jnp.concatenate([hi, mid, lo], axis=1)


def _split2(x):
    hi = x.astype(BF16)
    mid = (x - hi.astype(F32)).astype(BF16)
    return jnp.concatenate([hi, mid], axis=1)


def _sums2(pattern_bf16, x):
    y = jnp.dot(pattern_bf16, _split2(x), preferred_element_type=F32)
    return y[:, 0:LANES] + y[:, LANES:2 * LANES]


def _exact_sums(pattern_bf16, x):
    y = jnp.dot(pattern_bf16, _split3(x), preferred_element_type=F32)
    return y[:, 0:LANES] + y[:, LANES:2 * LANES] + y[:, 2 * LANES:3 * LANES]


def _pick(n, prefs):
    for p in prefs:
        if n % p == 0:
            return p
    return n


def _inproj_kernel(x_ref, g_ref, w_ref, o_ref, u_ref):
    @pl.when(pl.program_id(1) == 0)
    def _():
        u_ref[...] = _rms(x_ref[...], g_ref[...]).astype(BF16)

    o_ref[...] = jnp.dot(u_ref[...], w_ref[...], preferred_element_type=F32).astype(o_ref.dtype)


def _inproj(x2, g, w_packed):
    T = x2.shape[0]
    tm = _pick(T, (1024, 512, 256))
    tn = N_COLS // 4
    return pl.pallas_call(
        _inproj_kernel,
        grid=(T // tm, N_COLS // tn),
        in_specs=[
            pl.BlockSpec((tm, D_MODEL), lambda i, j: (i, 0)),
            pl.BlockSpec((1, D_MODEL), lambda i, j: (0, 0)),
            pl.BlockSpec((D_MODEL, tn), lambda i, j: (0, j)),
        ],
        out_specs=pl.BlockSpec((tm, tn), lambda i, j: (i, j)),
        out_shape=jax.ShapeDtypeStruct((T, N_COLS), COLS_DTYPE),
        scratch_shapes=[pltpu.VMEM((tm, D_MODEL), BF16)],
        compiler_params=_cparams(("parallel", "arbitrary")),
        name="inproj",
    )(x2, g, w_packed)


def _shift_mix(raw, carry_row, mu):
    rolled = pltpu.roll(raw, 1, axis=0)
    row = lax.broadcasted_iota(jnp.int32, raw.shape, 0)
    prev = jnp.where(row == 0, carry_row, rolled)
    return raw + mu * (prev - raw)


def _lofeat_kernel(lo_ref, mu_ref, o_ref, carry_ref):
    ts = lo_ref.shape[1]

    @pl.when(pl.program_id(1) == 0)
    def _():
        carry_ref[...] = jnp.zeros_like(carry_ref)

    raw = lo_ref[0].astype(F32)
    lo = _shift_mix(raw, carry_ref[0:1, :], mu_ref[...])
    carry_ref[0:1, :] = raw[ts - 1:ts, :]
    lane = lax.broadcasted_iota(jnp.int32, (ts, RW_LO_FEAT), 1)
    x = lo[:, 0:RW_LO_FEAT]
    feat = jnp.where(lane < RW_DECAY_RANK, jnp.tanh(x),
                     jnp.where(lane < LANES, x, _sigmoid(x)))
    o_ref[0] = feat.astype(o_ref.dtype)


def _lofeat(cols3, mu_lo):
    B, S, _ = cols3.shape
    ts = _pick(S, (1024, 512, 256, 128, 64))
    return pl.pallas_call(
        _lofeat_kernel,
        grid=(B, S // ts),
        in_specs=[
            pl.BlockSpec((1, ts, RW_LO_PAD), lambda b, t: (b, t, COL_LO // RW_LO_PAD)),
            pl.BlockSpec((1, RW_LO_PAD), lambda b, t: (0, 0)),
        ],
        out_specs=pl.BlockSpec((1, ts, RW_LO_FEAT), lambda b, t: (b, t, 0)),
        out_shape=jax.ShapeDtypeStruct((B, S, RW_LO_FEAT), BF16),
        scratch_shapes=[pltpu.VMEM((8, RW_LO_PAD), F32)],
        compiler_params=_cparams(("parallel", "arbitrary")),
        name="lofeat",
    )(cols3, mu_lo)


def _pair_sum(x):
    lane = lax.broadcasted_iota(jnp.int32, x.shape, 1)
    first = lane < RW_HEAD
    s0 = jnp.sum(jnp.where(first, x, 0.0), axis=-1, keepdims=True)
    s1 = jnp.sum(jnp.where(first, 0.0, x), axis=-1, keepdims=True)
    return jnp.where(first, s0, s1)


def _rwkv_kernel(r_ref, k_ref, v_ref, lo_ref, ga_ref, mu_ref,
                 w0_ref, w2_ref, a0_ref, a2_ref, g2_ref,
                 kk_ref, ka_ref, rk_ref, lnw_ref, lnb_ref, tri_ref,
                 o_ref, state_ref, carry_ref):
    nb, ts, _ = r_ref.shape
    c = RW_CHUNK
    n_chunks = ts // c

    @pl.when(pl.program_id(1) == 0)
    def _():
        state_ref[...] = jnp.zeros_like(state_ref)
        carry_ref[...] = jnp.zeros_like(carry_ref)

    t_i = lax.broadcasted_iota(jnp.int32, (c, LANES), 0)
    s_i = lax.broadcasted_iota(jnp.int32, (c, LANES), 1) & (RW_HEAD - 1)
    strict = t_i > s_i
    incl = t_i >= s_i
    eye = (t_i == s_i).astype(F32)
    same_sub = (t_i // RW_SUB) == (s_i // RW_SUB)
    rr = lax.broadcasted_iota(jnp.int32, (LANES, LANES), 0)
    cc = lax.broadcasted_iota(jnp.int32, (LANES, LANES), 1)
    bd_mask = (rr // RW_HEAD) == (cc // RW_HEAD)

    def bd(x):
        return jnp.where(bd_mask, jnp.concatenate([x, x], axis=0), 0.0).astype(BF16)

    def pmul(a, bmat):
        return _dot(a, bd(bmat))

    tri = tri_ref[...]
    pre, pc, d, ys = {}, {}, {}, {}

    gl = RW_GROUP * c
    n_groups = ts // gl
    last_raw = {}

    def prologue(b, g):
        rs = slice(g * gl, (g + 1) * gl)
        rkv = []
        for n, ref in enumerate((r_ref, k_ref, v_ref)):
            raw = ref[b, rs, :].astype(F32)
            carry = carry_ref[b, n:n + 1, :] if g == 0 else last_raw[b, n]
            rkv.append(_shift_mix(raw, carry, mu_ref[n:n + 1, :]))
            last_raw[b, n] = raw[gl - 1:gl, :]
            if g == n_groups - 1:
                carry_ref[b, n:n + 1, :] = last_raw[b, n]
        r, k, v = rkv
        lo = lo_ref[b, rs, :]
        wl = w0_ref[...] + _dot(lo[:, 0:RW_DECAY_RANK], w2_ref[...])
        lw = -LOG2E * jnp.exp(-_softplus(-wl) - 0.5)
        a = _sigmoid(a0_ref[...] + _dot(lo[:, RW_DECAY_RANK:LANES], a2_ref[...]))
        gate = _dot(lo[:, LANES:LANES + RW_GATE_RANK], g2_ref[...])
        kk = k * kk_ref[...]
        kkn = kk * lax.rsqrt(jnp.maximum(_pair_sum(kk * kk), 1e-24))
        kmod = k * (1.0 + (a - 1.0) * ka_ref[...])
        pre[b, g] = dict(r=r, v=v, kmod=kmod, g=gate, lw=lw, kkn=kkn, beta=kkn * a)

    def rows(name, it):
        b, ch = it
        lc = ch % RW_GROUP
        return pre[b, ch // RW_GROUP][name][lc * c:(lc + 1) * c]

    def prep(it):
        lw_c = rows("lw", it)
        cum = _exact_sums(tri, lw_c)
        c_end = cum[c - 1:c, :]
        e_neg = jnp.exp2(-cum)
        e_end = jnp.exp2(c_end - cum)
        beta_c, kmod_c = rows("beta", it), rows("kmod", it)
        pc[it] = dict(
            rt=rows("r", it) * jnp.exp2(cum), at=-rows("kkn", it) * jnp.exp2(cum - lw_c),
            bt=beta_c * e_neg, kt=kmod_c * e_neg, bh=beta_c * e_end, kh=kmod_c * e_end,
            w_end=jnp.exp2(c_end))

    def s_amat(its):
        for it in its:
            d["lhs2", it] = jnp.concatenate([pc[it]["at"], pc[it]["rt"]], axis=0)
            d["amat", it] = _dot_nt(d["lhs2", it], jnp.concatenate(
                [bd(pc[it]["bt"]), bd(pc[it]["kt"])], axis=0))

    def s_split(its):
        for it in its:
            am = d["amat", it]
            n_ab = jnp.where(strict, am[0:c, 0:LANES], 0.0)
            d["a_r2", it] = jnp.concatenate(
                [jnp.where(incl, am[c:2 * c, 0:LANES], 0.0),
                 jnp.where(incl, am[c:2 * c, LANES:2 * LANES], 0.0)], axis=1)
            d["bdv", it] = bd(rows("v", it))
            d["u0", it] = _dot(jnp.where(strict, am[0:c, LANES:2 * LANES], 0.0), d["bdv", it])
            d["dg", it] = jnp.where(same_sub, n_ab, 0.0)
            d["off", it] = n_ab - d["dg", it]

    def s_d2(its):
        for it in its:
            d["d2", it] = pmul(d["dg", it], d["dg", it])
            d["x", it] = eye + d["dg", it]

    def s_x(pw, nxt):
        def stage(its):
            for it in its:
                d["x", it] = d["x", it] + pmul(d["x", it], d[pw, it])
                if nxt:
                    d[nxt, it] = pmul(d[pw, it], d[pw, it])
        return stage

    def s_q(its):
        for it in its:
            d["q", it] = pmul(d["x", it], d["off", it])

    def s_q2(its):
        for it in its:
            d["q2", it] = pmul(d["q", it], d["q", it])
            d["z", it] = d["x", it] + pmul(d["q", it], d["x", it])

    def s_tinv(its):
        for it in its:
            d["t_inv", it] = d["z", it] + pmul(d["q2", it], d["z", it])

    def s_au(its):
        for it in its:
            au = _dot(d["t_inv", it], jnp.concatenate([bd(pc[it]["at"]), bd(d["u0", it])], axis=1))
            d["ahat", it], d["uu", it] = au[:, 0:LANES], au[:, LANES:2 * LANES]

    def s_trans(its):
        for it in its:
            d["mmat", it] = jnp.where(bd_mask, _dot_tn(d["ahat", it], pc[it]["bh"]), 0.0).astype(BF16)
            d["gmat", it] = jnp.where(bd_mask, _dot_tn(
                jnp.concatenate([d["uu", it], rows("v", it)], axis=0),
                jnp.concatenate([pc[it]["bh"], pc[it]["kh"]], axis=0)), 0.0)
            d["rhat", it] = pc[it]["rt"] + pmul(d["a_r2", it][:, 0:LANES], d["ahat", it])
            d["y0", it] = _dot(d["a_r2", it], jnp.concatenate([bd(d["uu", it]), d["bdv", it]], axis=0))

    stages = [s_amat, s_split, s_d2, s_x("d2", "d4"), s_x("d4", "d8"), s_x("d8", None),
              s_q, s_q2, s_tinv, s_au, s_trans]

    def chain_step(bs, ch, st):
        st_bf = {b: st[b].astype(BF16) for b in bs}
        for b in bs:
            ys[b, ch] = d["y0", (b, ch)] + _dot_nt(d["rhat", (b, ch)], st_bf[b])
        for b in bs:
            st[b] = (st[b] * pc[b, ch]["w_end"] + _dot(st_bf[b], d["mmat", (b, ch)])
                     + d["gmat", (b, ch)])

    def epilogue(b, g):
        rs = slice(g * gl, (g + 1) * gl)
        y = jnp.concatenate([ys[b, ch] for ch in range(g * RW_GROUP, (g + 1) * RW_GROUP)], axis=0)
        pb = pre[b, g]
        mean = _pair_sum(y) * (1.0 / RW_HEAD)
        dlt = y - mean
        var = _pair_sum(dlt * dlt) * (1.0 / RW_HEAD)
        on = dlt * lax.rsqrt(var + RW_LN_EPS) * lnw_ref[...] + lnb_ref[...]
        bonus = _pair_sum(pb["r"] * pb["kmod"] * rk_ref[...]) * pb["v"]
        o_ref[b, rs, :] = ((on + bonus) * pb["g"]) * _sigmoid(ga_ref[b, rs, :].astype(F32))

    batches = list(range(nb))
    group_items = [[(b, ch) for ch in range(g * RW_GROUP, (g + 1) * RW_GROUP) for b in batches]
                   for g in range(n_groups)]
    st = {}

    def prep_units(g):
        return ([functools.partial(prologue, b, g) for b in batches]
                + [functools.partial(prep, it) for it in group_items[g]])

    def tail_units(g):
        units = [functools.partial(chain_step, batches, ch, st)
                 for ch in range(g * RW_GROUP, (g + 1) * RW_GROUP)]
        return units + [functools.partial(epilogue, b, g) for b in batches]

    def interleave(stage_items, units):
        per = -(-len(units) // len(stages)) if units else 0
        for n, stage in enumerate(stages):
            stage(stage_items)
            for u in units[n * per:(n + 1) * per]:
                u()

    for b in batches:
        st[b] = state_ref[b]
    for u in prep_units(0):
        u()
    for g in range(n_groups):
        units = (prep_units(g + 1) if g + 1 < n_groups else []) + (tail_units(g - 1) if g > 0 else [])
        interleave(group_items[g], units)
    for u in tail_units(n_groups - 1):
        u()
    for b in batches:
        state_ref[b] = st[b]


def _rwkv(cols3, lo_feat, p):
    B, S, _ = cols3.shape
    ts = _pick(S, (2 * RW_GROUP * RW_CHUNK, RW_GROUP * RW_CHUNK))
    nt = S // ts
    n_hp = RW_WIDTH // LANES
    cb = lambda col: col // LANES

    def colspec(col0):
        return pl.BlockSpec((B, ts, LANES), lambda h, t: (0, t, cb(col0) + h))

    vec = pl.BlockSpec((1, LANES), lambda h, t: (0, h))
    mat = lambda nrows: pl.BlockSpec((nrows, LANES), lambda h, t: (0, h))
    full = lambda shape: pl.BlockSpec(shape, lambda h, t: (0, 0))
    tri = jnp.asarray(np.tril(np.ones((RW_CHUNK, RW_CHUNK), np.float32)), dtype=BF16)
    return pl.pallas_call(
        _rwkv_kernel,
        grid=(n_hp, nt),
        in_specs=[
            colspec(COL_R), colspec(COL_K), colspec(COL_V),
            pl.BlockSpec((B, ts, RW_LO_FEAT), lambda h, t: (0, t, 0)),
            colspec(COL_GA),
            mat(3),
            vec, mat(RW_DECAY_RANK), vec, mat(RW_ICLR_RANK), mat(RW_GATE_RANK),
            vec, vec, vec, vec, vec,
            full(tri.shape),
        ],
        out_specs=pl.BlockSpec((B, ts, LANES), lambda h, t: (0, t, h)),
        out_shape=jax.ShapeDtypeStruct((B, S, RW_WIDTH), F32),
        scratch_shapes=[pltpu.VMEM((B, LANES, LANES), F32), pltpu.VMEM((B, 8, LANES), F32)],
        compiler_params=_cparams(("parallel", "arbitrary")),
        name="rwkv7",
    )(cols3, cols3, cols3, lo_feat, cols3, p["mu_rkv"],
      p["w0"], p["w2"], p["a0"], p["a2"], p["g2"],
      p["k_k"], p["k_a"], p["r_k"], p["ln_w"], p["ln_b"], tri)


def _hg_level_matrix():
    c = HG_CHUNK
    m = np.zeros((HG_LEVELS + 1, c, c), np.float32)
    for l in range(HG_LEVELS):
        bs = c >> l
        half = bs // 2
        for t in range(c):
            mid = (t // bs) * bs + half
            if t % bs >= half:
                m[l, t, mid:t + 1] = 1.0
            else:
                m[l, t, t + 1:mid] = 1.0
    m[HG_LEVELS] = np.tril(np.ones((c, c), np.float32))
    return m.reshape((HG_LEVELS + 1) * c, c)


def _hgrn_kernel(q_ref, f_ref, i_ref, g_ref, gb_ref, lbl_ref, ng_ref, lvl_ref,
                 o_ref, state_ref):
    nb, ts, _ = q_ref.shape
    c = HG_CHUNK
    n_chunks = ts // c
    items = [(b, ch) for b in range(nb) for ch in range(n_chunks)]

    @pl.when(pl.program_id(1) == 0)
    def _():
        state_ref[...] = jnp.zeros_like(state_ref)

    logits = lbl_ref[...]
    mx = jnp.max(logits, axis=0, keepdims=True)
    ex = jnp.exp(logits - mx)
    lb = ex[0:1, :] / jnp.sum(ex, axis=0, keepdims=True)

    ri = lax.broadcasted_iota(jnp.int32, (c, c), 0)
    ci = lax.broadcasted_iota(jnp.int32, (c, c), 1)
    rowi = lax.broadcasted_iota(jnp.int32, (c, LANES), 0)
    lvl = lvl_ref[...]
    second, valid = [], []
    for l in range(HG_LEVELS):
        bs = c >> l
        second.append((rowi & (bs - 1)) >= (bs // 2))
        valid.append(((ri // bs) == (ci // bs)) & ((ri & (bs - 1)) >= (bs // 2))
                     & ((ci & (bs - 1)) < (bs // 2)))

    pre = []
    for b in range(nb):
        f = lb + (1.0 - lb) * _sigmoid(f_ref[b].astype(F32))
        qraw = q_ref[b].astype(F32)
        pre.append(dict(lf=LOG2E * jnp.log(f), kx=1.0 - f, qs=qraw * _sigmoid(qraw), iv=i_ref[b]))

    def rows(name, it):
        b, ch = it
        return pre[b][name][ch * c:(ch + 1) * c]

    part = {it: _sums2(lvl, rows("lf", it)) for it in items}
    scores = {}
    for it in items:
        q_c, k_c = rows("qs", it), rows("kx", it)
        dsum = jnp.sum(q_c * k_c, axis=-1, keepdims=True)
        sc_sum = jnp.where(ri == ci, dsum, 0.0)
        for l in range(HG_LEVELS):
            qk = (jnp.where(second[l], q_c, k_c) * jnp.exp2(part[it][l * c:(l + 1) * c])).astype(BF16)
            sc_sum = jnp.where(valid[l], _dot_nt(qk, qk), sc_sum)
        scores[it] = sc_sum
    o_intra = {it: _dot(scores[it], rows("iv", it)) for it in items}
    bcum = {it: part[it][HG_LEVELS * c:(HG_LEVELS + 1) * c] for it in items}
    zc = {it: _dot_tn(rows("iv", it),
                      rows("kx", it) * jnp.exp2(bcum[it][c - 1:c, :] - bcum[it])) for it in items}
    qb = {it: rows("qs", it) * jnp.exp2(bcum[it]) for it in items}

    outs = {}
    for b in range(nb):
        st = state_ref[b]
        for ch in range(n_chunks):
            it = (b, ch)
            outs[it] = o_intra[it] + _dot_nt(qb[it], st)
            st = st * jnp.exp2(bcum[it][c - 1:c, :]) + zc[it]
        state_ref[b] = st

    ng = ng_ref[...]
    for b in range(nb):
        o = jnp.concatenate([outs[(b, ch)] for ch in range(n_chunks)], axis=0)
        o = o * lax.rsqrt(jnp.mean(o * o, axis=-1, keepdims=True) + NORM_EPS) * ng
        graw = g_ref[b].astype(F32)
        o_ref[b] = o * (graw * _sigmoid(graw)) * _sigmoid(gb_ref[b].astype(F32))


def _hgrn(cols3, lb_logits, norm_g):
    B, S, _ = cols3.shape
    ts = _pick(S, (256, 128, 64))
    nt = S // ts
    cb = lambda col: col // LANES

    def colspec(col0):
        return pl.BlockSpec((B, ts, LANES), lambda h, t: (0, t, cb(col0) + h))

    n_slots = lb_logits.shape[0]
    lvl = jnp.asarray(_hg_level_matrix(), dtype=BF16)
    return pl.pallas_call(
        _hgrn_kernel,
        grid=(HG_HEADS, nt),
        in_specs=[
            colspec(COL_HQ), colspec(COL_HF), colspec(COL_HI), colspec(COL_HG), colspec(COL_GB),
            pl.BlockSpec((n_slots, LANES), lambda h, t: (0, h)),
            pl.BlockSpec((1, LANES), lambda h, t: (0, 0)),
            pl.BlockSpec(lvl.shape, lambda h, t: (0, 0)),
        ],
        out_specs=pl.BlockSpec((B, ts, LANES), lambda h, t: (0, t, h)),
        out_shape=jax.ShapeDtypeStruct((B, S, D_MODEL), F32),
        scratch_shapes=[pltpu.VMEM((B, HG_KEY, HG_KEY), F32)],
        compiler_params=_cparams(("parallel", "arbitrary")),
        name="hgrn2",
    )(cols3, cols3, cols3, cols3, cols3, lb_logits, norm_g, lvl)


def _outproj_kernel(x_ref, ya_ref, yb_ref, w_ref, o_ref):
    y = ya_ref[...] + yb_ref[...]
    o_ref[...] = x_ref[...] + _dot(y, w_ref[...])


def _outproj(x2, ya, yb, w):
    T = x2.shape[0]
    tm = _pick(T, (512, 256))
    row = pl.BlockSpec((tm, D_MODEL), lambda i: (i, 0))
    return pl.pallas_call(
        _outproj_kernel,
        grid=(T // tm,),
        in_specs=[row, row, row, pl.BlockSpec((D_MODEL, D_MODEL), lambda i: (0, 0))],
        out_specs=row,
        out_shape=jax.ShapeDtypeStruct((T, D_MODEL), F32),
        compiler_params=_cparams(("parallel",)),
        name="outproj",
    )(x2, ya, yb, w)


def _memkv_kernel(m_ref, g_ref, wk_ref, wv_ref, k_ref, v_ref):
    m = _rms(m_ref[...], g_ref[...]).astype(BF16)
    k_ref[...] = jnp.dot(m, wk_ref[...], preferred_element_type=F32).astype(BF16)
    v_ref[...] = jnp.dot(m, wv_ref[...], preferred_element_type=F32).astype(BF16)


def _memkv(mem2, g, wk, wv, B, n_mem):
    row = pl.BlockSpec((n_mem, D_MODEL), lambda b: (b, 0))
    wspec = pl.BlockSpec((D_MODEL, D_MODEL), lambda b: (0, 0))
    sds = jax.ShapeDtypeStruct((B * n_mem, D_MODEL), BF16)
    return pl.pallas_call(
        _memkv_kernel,
        grid=(B,),
        in_specs=[row, pl.BlockSpec((1, D_MODEL), lambda b: (0, 0)), wspec, wspec],
        out_specs=[row, row],
        out_shape=[sds, sds],
        compiler_params=_cparams(("parallel",)),
        name="memkv",
    )(mem2, g, wk, wv)


def _xattn_kernel(h_ref, g_ref, wq_ref, k_ref, v_ref, wo_ref, o_ref):
    h = h_ref[...]
    u = _rms(h, g_ref[...]).astype(BF16)
    q = jnp.dot(u, wq_ref[...], preferred_element_type=F32)
    k = k_ref[...]
    v = v_ref[...]
    heads = []
    for hd in range(XA_HEADS):
        sl = slice(hd * XA_HEAD, (hd + 1) * XA_HEAD)
        s = _dot_nt(q[:, sl], k[:, sl]) * (XA_HEAD ** -0.5)
        s = s - jnp.max(s, axis=-1, keepdims=True)
        e = jnp.exp(s)
        p = e / jnp.sum(e, axis=-1, keepdims=True)
        heads.append(_dot(p, v[:, sl]))
    o = jnp.concatenate(heads, axis=-1)
    o_ref[...] = h + _dot(o, wo_ref[...])


def _xattn(h1, g, wq, km, vm, wo, B, S, n_mem):
    tm = _pick(S, (512, 256))
    nt = S // tm
    row = pl.BlockSpec((tm, D_MODEL), lambda b, t: (b * nt + t, 0))
    wspec = pl.BlockSpec((D_MODEL, D_MODEL), lambda b, t: (0, 0))
    kv = pl.BlockSpec((n_mem, D_MODEL), lambda b, t: (b, 0))
    return pl.pallas_call(
        _xattn_kernel,
        grid=(B, nt),
        in_specs=[row, pl.BlockSpec((1, D_MODEL), lambda b, t: (0, 0)), wspec, kv, kv, wspec],
        out_specs=row,
        out_shape=jax.ShapeDtypeStruct((B * S, D_MODEL), F32),
        compiler_params=_cparams(("parallel", "parallel")),
        name="xattn",
    )(h1, g, wq, km, vm, wo)


def _ffn_kernel(h_ref, g_ref, w1_ref, w3_ref, w2_ref, gf_ref, o_ref, u_ref, acc_ref):
    j = pl.program_id(1)

    @pl.when(j == 0)
    def _():
        u_ref[...] = _rms(h_ref[...], g_ref[...]).astype(BF16)
        acc_ref[...] = h_ref[...]

    u = u_ref[...]
    a = jnp.dot(u, w1_ref[...], preferred_element_type=F32)
    b = jnp.dot(u, w3_ref[...], preferred_element_type=F32)
    mid = (a * _sigmoid(a)) * b
    acc_ref[...] += _dot(mid, w2_ref[...])

    @pl.when(j == pl.num_programs(1) - 1)
    def _():
        o_ref[...] = _rms(acc_ref[...], gf_ref[...])


def _ffn(h2, g, w1, w3, w2, gf):
    T = h2.shape[0]
    tm = _pick(T, (512, 256))
    tf = D_FF // 2
    row = pl.BlockSpec((tm, D_MODEL), lambda i, j: (i, 0))
    vec = pl.BlockSpec((1, D_MODEL), lambda i, j: (0, 0))
    return pl.pallas_call(
        _ffn_kernel,
        grid=(T // tm, D_FF // tf),
        in_specs=[row, vec,
                  pl.BlockSpec((D_MODEL, tf), lambda i, j: (0, j)),
                  pl.BlockSpec((D_MODEL, tf), lambda i, j: (0, j)),
                  pl.BlockSpec((tf, D_MODEL), lambda i, j: (j, 0)),
                  vec],
        out_specs=row,
        out_shape=jax.ShapeDtypeStruct((T, D_MODEL), F32),
        scratch_shapes=[pltpu.VMEM((tm, D_MODEL), BF16), pltpu.VMEM((tm, D_MODEL), F32)],
        compiler_params=_cparams(("parallel", "arbitrary")),
        name="ffn",
    )(h2, g, w1, w3, w2, gf)


def _pack_in_weights(w_in):
    rw_cols = 3 * RW_WIDTH + RW_LO
    pad = jnp.zeros((D_MODEL, RW_LO_PAD - RW_LO), w_in.dtype)
    packed = jnp.concatenate([w_in[:, :rw_cols], pad, w_in[:, rw_cols:]], axis=1)
    return packed.astype(BF16)


def _pack_mu_lo(mu):
    lo = mu[3 * RW_WIDTH:]
    return jnp.concatenate([lo, jnp.zeros((RW_LO_PAD - RW_LO,), mu.dtype)])[None, :].astype(F32)


def kernel(x, mem, norm_mix_g, w_in, rw_mu, rw_w0, rw_w2, rw_a0, rw_a2, rw_g2, rw_k_k, rw_k_a, rw_r_k, rw_ln_w, rw_ln_b, hg_lb_logits, hg_norm_g, w_out, norm_xa_g, norm_mem_g, xa_wq, xa_wk, xa_wv, xa_wo, norm_ffn_g, ffn_w1, ffn_w3, ffn_w2, norm_final_g):
    B, S, _ = x.shape
    n_mem = mem.shape[1]
    depth = w_in.shape[0]
    assert depth == 1, "single-layer block"
    assert S % (RW_GROUP * RW_CHUNK) == 0 and S % HG_CHUNK == 0
    l = 0
    T = B * S
    row = lambda a: a.reshape(1, -1).astype(F32)

    x2 = x.reshape(T, D_MODEL)
    cols = _inproj(x2, row(norm_mix_g[l]), _pack_in_weights(w_in[l]))
    cols3 = cols.reshape(B, S, N_COLS)

    mu = rw_mu[l].astype(F32)
    lo_feat = _lofeat(cols3, _pack_mu_lo(mu))
    rw = dict(
        mu_rkv=mu[:3 * RW_WIDTH].reshape(3, RW_WIDTH),
        w0=row(rw_w0[l]), w2=rw_w2[l].astype(BF16), a0=row(rw_a0[l]), a2=rw_a2[l].astype(BF16),
        g2=rw_g2[l].astype(BF16), k_k=row(rw_k_k[l]), k_a=row(rw_k_a[l]), r_k=row(rw_r_k[l]),
        ln_w=row(rw_ln_w[l]), ln_b=row(rw_ln_b[l]))
    ya = _rwkv(cols3, lo_feat, rw).reshape(T, D_MODEL)
    yb = _hgrn(cols3, hg_lb_logits.astype(F32), row(hg_norm_g[l])).reshape(T, D_MODEL)

    h1 = _outproj(x2, ya, yb, w_out[l].astype(BF16))

    km, vm = _memkv(mem.reshape(B * n_mem, D_MODEL), row(norm_mem_g[l]),
                    xa_wk[l].astype(BF16), xa_wv[l].astype(BF16), B, n_mem)
    h2 = _xattn(h1, row(norm_xa_g[l]), xa_wq[l].astype(BF16), km, vm,
                xa_wo[l].astype(BF16), B, S, n_mem)

    out = _ffn(h2, row(norm_ffn_g[l]), ffn_w1[l].astype(BF16), ffn_w3[l].astype(BF16),
               ffn_w2[l].astype(BF16), row(norm_final_g))
    return out.reshape(B, S, D_MODEL)
```

```python
import functools

import numpy as np
import jax
import jax.numpy as jnp
from jax import lax
from jax.experimental import pallas as pl
from jax.experimental.pallas import tpu as pltpu

F32 = jnp.float32
BF16 = jnp.bfloat16

D_MODEL = 1024
NORM_EPS = 1e-6
LANES = 128

RW_HEAD = 64
RW_WIDTH = D_MODEL
RW_DECAY_RANK = 64
RW_ICLR_RANK = 64
RW_GATE_RANK = 160
RW_LN_EPS = 64e-5
RW_LO = RW_DECAY_RANK + RW_ICLR_RANK + RW_GATE_RANK
RW_LO_PAD = 512
RW_LO_FEAT = 384
RW_CHUNK = 64
RW_SUB = 16
RW_GROUP = 4
LOG2E = 1.4426950408889634

HG_KEY = 128
HG_HEADS = D_MODEL // HG_KEY
HG_CHUNK = 64
HG_LEVELS = 6
HG_GROUP = 4

XA_HEADS = 4
XA_HEAD = D_MODEL // XA_HEADS
D_FF = 2816

COL_R = 0
COL_K = COL_R + RW_WIDTH
COL_V = COL_K + RW_WIDTH
COL_LO = COL_V + RW_WIDTH
COL_HQ = COL_LO + RW_LO_PAD
COL_HF = COL_HQ + D_MODEL
COL_HI = COL_HF + D_MODEL
COL_HG = COL_HI + D_MODEL
COL_GA = COL_HG + D_MODEL
COL_GB = COL_GA + D_MODEL
N_COLS = COL_GB + D_MODEL

COLS_DTYPE = BF16

VMEM_LIMIT = 56 * 1024 * 1024


def _cparams(sem):
    return pltpu.CompilerParams(dimension_semantics=sem, vmem_limit_bytes=VMEM_LIMIT)


def _sigmoid(x):
    return 1.0 / (1.0 + jnp.exp(-x))


def _softplus(x):
    return jnp.maximum(x, 0.0) + jnp.log(1.0 + jnp.exp(-jnp.abs(x)))


def _rms(x, g):
    ms = jnp.mean(x * x, axis=-1, keepdims=True)
    return x * lax.rsqrt(ms + NORM_EPS) * g


def _dot(a, b):
    return jnp.dot(a.astype(BF16), b.astype(BF16), preferred_element_type=F32)


def _dot_nt(a, b):
    return lax.dot_general(a.astype(BF16), b.astype(BF16), (((1,), (1,)), ((), ())),
                           preferred_element_type=F32)


def _dot_tn(a, b):
    return lax.dot_general(a.astype(BF16), b.astype(BF16), (((0,), (0,)), ((), ())),
                           preferred_element_type=F32)


def _split3(x):
    hi = x.astype(BF16)
    r1 = x - hi.astype(F32)
    mid = r1.astype(BF16)
    lo = (r1 - mid.astype(F32)).astype(BF16)
    return jnp.concatenate([hi, mid, lo], axis=1)


def _split2(x):
    hi = x.astype(BF16)
    mid = (x - hi.astype(F32)).astype(BF16)
    return jnp.concatenate([hi, mid], axis=1)


def _sums2(pattern_bf16, x):
    y = jnp.dot(pattern_bf16, _split2(x), preferred_element_type=F32)
    return y[:, 0:LANES] + y[:, LANES:2 * LANES]


def _exact_sums(pattern_bf16, x):
    y = jnp.dot(pattern_bf16, _split3(x), preferred_element_type=F32)
    return y[:, 0:LANES] + y[:, LANES:2 * LANES] + y[:, 2 * LANES:3 * LANES]


def _pick(n, prefs):
    for p in prefs:
        if n % p == 0:
            return p
    return n


def _inproj_kernel(x_ref, g_ref, w_ref, o_ref, u_ref):
    @pl.when(pl.program_id(1) == 0)
    def _():
        u_ref[...] = _rms(x_ref[...], g_ref[...]).astype(BF16)

    o_ref[...] = jnp.dot(u_ref[...], w_ref[...], preferred_element_type=F32).astype(o_ref.dtype)


def _inproj(x2, g, w_packed):
    T = x2.shape[0]
    tm = _pick(T, (1024, 512, 256))
    tn = N_COLS // 4
    return pl.pallas_call(
        _inproj_kernel,
        grid=(T // tm, N_COLS // tn),
        in_specs=[
            pl.BlockSpec((tm, D_MODEL), lambda i, j: (i, 0)),
            pl.BlockSpec((1, D_MODEL), lambda i, j: (0, 0)),
            pl.BlockSpec((D_MODEL, tn), lambda i, j: (0, j)),
        ],
        out_specs=pl.BlockSpec((tm, tn), lambda i, j: (i, j)),
        out_shape=jax.ShapeDtypeStruct((T, N_COLS), COLS_DTYPE),
        scratch_shapes=[pltpu.VMEM((tm, D_MODEL), BF16)],
        compiler_params=_cparams(("parallel", "arbitrary")),
        name="inproj",
    )(x2, g, w_packed)


def _shift_mix(raw, carry_row, mu):
    rolled = pltpu.roll(raw, 1, axis=0)
    row = lax.broadcasted_iota(jnp.int32, raw.shape, 0)
    prev = jnp.where(row == 0, carry_row, rolled)
    return raw + mu * (prev - raw)


def _lofeat_kernel(lo_ref, mu_ref, o_ref, carry_ref):
    ts = lo_ref.shape[1]

    @pl.when(pl.program_id(1) == 0)
    def _():
        carry_ref[...] = jnp.zeros_like(carry_ref)

    raw = lo_ref[0].astype(F32)
    lo = _shift_mix(raw, carry_ref[0:1, :], mu_ref[...])
    carry_ref[0:1, :] = raw[ts - 1:ts, :]
    lane = lax.broadcasted_iota(jnp.int32, (ts, RW_LO_FEAT), 1)
    x = lo[:, 0:RW_LO_FEAT]
    feat = jnp.where(lane < RW_DECAY_RANK, jnp.tanh(x),
                     jnp.where(lane < LANES, x, _sigmoid(x)))
    o_ref[0] = feat.astype(o_ref.dtype)


def _lofeat(cols3, mu_lo):
    B, S, _ = cols3.shape
    ts = _pick(S, (1024, 512, 256, 128, 64))
    return pl.pallas_call(
        _lofeat_kernel,
        grid=(B, S // ts),
        in_specs=[
            pl.BlockSpec((1, ts, RW_LO_PAD), lambda b, t: (b, t, COL_LO // RW_LO_PAD)),
            pl.BlockSpec((1, RW_LO_PAD), lambda b, t: (0, 0)),
        ],
        out_specs=pl.BlockSpec((1, ts, RW_LO_FEAT), lambda b, t: (b, t, 0)),
        out_shape=jax.ShapeDtypeStruct((B, S, RW_LO_FEAT), BF16),
        scratch_shapes=[pltpu.VMEM((8, RW_LO_PAD), F32)],
        compiler_params=_cparams(("parallel", "arbitrary")),
        name="lofeat",
    )(cols3, mu_lo)


def _pair_sum(x):
    lane = lax.broadcasted_iota(jnp.int32, x.shape, 1)
    first = lane < RW_HEAD
    s0 = jnp.sum(jnp.where(first, x, 0.0), axis=-1, keepdims=True)
    s1 = jnp.sum(jnp.where(first, 0.0, x), axis=-1, keepdims=True)
    return jnp.where(first, s0, s1)


def _rwkv_kernel(r_ref, k_ref, v_ref, lo_ref, ga_ref, mu_ref,
                 w0_ref, w2_ref, a0_ref, a2_ref, g2_ref,
                 kk_ref, ka_ref, rk_ref, lnw_ref, lnb_ref, tri_ref,
                 o_ref, state_ref, carry_ref):
    nb, ts, _ = r_ref.shape
    c = RW_CHUNK
    n_chunks = ts // c

    @pl.when(pl.program_id(1) == 0)
    def _():
        state_ref[...] = jnp.zeros_like(state_ref)
        carry_ref[...] = jnp.zeros_like(carry_ref)

    t_i = lax.broadcasted_iota(jnp.int32, (c, LANES), 0)
    s_i = lax.broadcasted_iota(jnp.int32, (c, LANES), 1) & (RW_HEAD - 1)
    strict = t_i > s_i
    incl = t_i >= s_i
    eye = (t_i == s_i).astype(F32)
    same_sub = (t_i // RW_SUB) == (s_i // RW_SUB)
    rr = lax.broadcasted_iota(jnp.int32, (LANES, LANES), 0)
    cc = lax.broadcasted_iota(jnp.int32, (LANES, LANES), 1)
    bd_mask = (rr // RW_HEAD) == (cc // RW_HEAD)

    def bd(x):
        return jnp.where(bd_mask, jnp.concatenate([x, x], axis=0), 0.0).astype(BF16)

    def pmul(a, bmat):
        return _dot(a, bd(bmat))

    tri = tri_ref[...]
    pre, pc, d, ys = {}, {}, {}, {}

    gl = RW_GROUP * c
    n_groups = ts // gl
    last_raw = {}

    def prologue(b, g):
        rs = slice(g * gl, (g + 1) * gl)
        rkv = []
        for n, ref in enumerate((r_ref, k_ref, v_ref)):
            raw = ref[b, rs, :].astype(F32)
            carry = carry_ref[b, n:n + 1, :] if g == 0 else last_raw[b, n]
            rkv.append(_shift_mix(raw, carry, mu_ref[n:n + 1, :]))
            last_raw[b, n] = raw[gl - 1:gl, :]
            if g == n_groups - 1:
                carry_ref[b, n:n + 1, :] = last_raw[b, n]
        r, k, v = rkv
        lo = lo_ref[b, rs, :]
        wl = w0_ref[...] + _dot(lo[:, 0:RW_DECAY_RANK], w2_ref[...])
        lw = -LOG2E * jnp.exp(-_softplus(-wl) - 0.5)
        a = _sigmoid(a0_ref[...] + _dot(lo[:, RW_DECAY_RANK:LANES], a2_ref[...]))
        gate = _dot(lo[:, LANES:LANES + RW_GATE_RANK], g2_ref[...])
        kk = k * kk_ref[...]
        kkn = kk * lax.rsqrt(jnp.maximum(_pair_sum(kk * kk), 1e-24))
        kmod = k * (1.0 + (a - 1.0) * ka_ref[...])
        pre[b, g] = dict(r=r, v=v, kmod=kmod, g=gate, lw=lw, kkn=kkn, beta=kkn * a)

    def rows(name, it):
        b, ch = it
        lc = ch % RW_GROUP
        return pre[b, ch // RW_GROUP][name][lc * c:(lc + 1) * c]

    def prep(it):
        lw_c = rows("lw", it)
        cum = _exact_sums(tri, lw_c)
        c_end = cum[c - 1:c, :]
        e_neg = jnp.exp2(-cum)
        e_end = jnp.exp2(c_end - cum)
        beta_c, kmod_c = rows("beta", it), rows("kmod", it)
        pc[it] = dict(
            rt=rows("r", it) * jnp.exp2(cum), at=-rows("kkn", it) * jnp.exp2(cum - lw_c),
            bt=beta_c * e_neg, kt=kmod_c * e_neg, bh=beta_c * e_end, kh=kmod_c * e_end,
            w_end=jnp.exp2(c_end))

    def s_amat(its):
        for it in its:
            d["lhs2", it] = jnp.concatenate([pc[it]["at"], pc[it]["rt"]], axis=0)
            d["amat", it] = _dot_nt(d["lhs2", it], jnp.concatenate(
                [bd(pc[it]["bt"]), bd(pc[it]["kt"])], axis=0))

    def s_split(its):
        for it in its:
            am = d["amat", it]
            n_ab = jnp.where(strict, am[0:c, 0:LANES], 0.0)
            d["a_r2", it] = jnp.concatenate(
                [jnp.where(incl, am[c:2 * c, 0:LANES], 0.0),
                 jnp.where(incl, am[c:2 * c, LANES:2 * LANES], 0.0)], axis=1)
            d["bdv", it] = bd(rows("v", it))
            d["u0", it] = _dot(jnp.where(strict, am[0:c, LANES:2 * LANES], 0.0), d["bdv", it])
            d["dg", it] = jnp.where(same_sub, n_ab, 0.0)
            d["off", it] = n_ab - d["dg", it]

    def s_d2(its):
        for it in its:
            d["d2", it] = pmul(d["dg", it], d["dg", it])
            d["x", it] = eye + d["dg", it]

    def s_x(pw, nxt):
        def stage(its):
            for it in its:
                if nxt:
                    both = pmul(jnp.concatenate([d["x", it], d[pw, it]], axis=0), d[pw, it])
                    d["x", it] = d["x", it] + both[0:c]
                    d[nxt, it] = both[c:2 * c]
                else:
                    d["x", it] = d["x", it] + pmul(d["x", it], d[pw, it])
        return stage

    def s_r(its):
        for it in its:
            d["r", it] = pmul(d["off", it], d["x", it])

    def s_r2(its):
        for it in its:
            both = pmul(jnp.concatenate([d["x", it], d["r", it]], axis=0), d["r", it])
            d["z", it] = d["x", it] + both[0:c]
            d["r2", it] = both[c:2 * c]

    def s_tinv(its):
        for it in its:
            d["t_inv", it] = d["z", it] + pmul(d["z", it], d["r2", it])

    def s_au(its):
        for it in its:
            au = _dot(d["t_inv", it], jnp.concatenate([bd(pc[it]["at"]), bd(d["u0", it])], axis=1))
            d["ahat", it], d["uu", it] = au[:, 0:LANES], au[:, LANES:2 * LANES]

    def s_trans(its):
        for it in its:
            lhs = jnp.concatenate(
                [jnp.concatenate([d["ahat", it], d["uu", it]], axis=1),
                 jnp.concatenate([jnp.zeros((c, LANES), F32), rows("v", it)], axis=1)], axis=0)
            mg = _dot_tn(lhs, jnp.concatenate([pc[it]["bh"], pc[it]["kh"]], axis=0))
            d["mmat", it] = jnp.where(bd_mask, mg[0:LANES], 0.0).astype(BF16)
            d["gmat", it] = jnp.where(bd_mask, mg[LANES:2 * LANES], 0.0)
            d["rhat", it] = pc[it]["rt"] + pmul(d["a_r2", it][:, 0:LANES], d["ahat", it])
            d["y0", it] = _dot(d["a_r2", it], jnp.concatenate([bd(d["uu", it]), d["bdv", it]], axis=0))

    stages = [s_amat, s_split, s_d2, s_x("d2", "d4"), s_x("d4", "d8"), s_x("d8", None),
              s_r, s_r2, s_tinv, s_au, s_trans]

    def chain_step(bs, ch, st):
        st_bf = {b: st[b].astype(BF16) for b in bs}
        for b in bs:
            ys[b, ch] = d["y0", (b, ch)] + _dot_nt(d["rhat", (b, ch)], st_bf[b])
        for b in bs:
            st[b] = (st[b] * pc[b, ch]["w_end"] + _dot(st_bf[b], d["mmat", (b, ch)])
                     + d["gmat", (b, ch)])

    def epilogue(b, g):
        rs = slice(g * gl, (g + 1) * gl)
        y = jnp.concatenate([ys[b, ch] for ch in range(g * RW_GROUP, (g + 1) * RW_GROUP)], axis=0)
        pb = pre[b, g]
        mean = _pair_sum(y) * (1.0 / RW_HEAD)
        dlt = y - mean
        var = _pair_sum(dlt * dlt) * (1.0 / RW_HEAD)
        on = dlt * lax.rsqrt(var + RW_LN_EPS) * lnw_ref[...] + lnb_ref[...]
        bonus = _pair_sum(pb["r"] * pb["kmod"] * rk_ref[...]) * pb["v"]
        o_ref[b, rs, :] = (((on + bonus) * pb["g"])
                           * _sigmoid(ga_ref[b, rs, :].astype(F32))).astype(o_ref.dtype)

    batches = list(range(nb))
    group_items = [[(b, ch) for ch in range(g * RW_GROUP, (g + 1) * RW_GROUP) for b in batches]
                   for g in range(n_groups)]
    st = {}

    def prep_units(g):
        return ([functools.partial(prologue, b, g) for b in batches]
                + [functools.partial(prep, it) for it in group_items[g]])

    def tail_units(g):
        units = [functools.partial(chain_step, batches, ch, st)
                 for ch in range(g * RW_GROUP, (g + 1) * RW_GROUP)]
        return units + [functools.partial(epilogue, b, g) for b in batches]

    def interleave(stage_items, units):
        per = -(-len(units) // len(stages)) if units else 0
        for n, stage in enumerate(stages):
            stage(stage_items)
            for u in units[n * per:(n + 1) * per]:
                u()

    for b in batches:
        st[b] = state_ref[b]
    for u in prep_units(0):
        u()
    for g in range(n_groups):
        units = (prep_units(g + 1) if g + 1 < n_groups else []) + (tail_units(g - 1) if g > 0 else [])
        interleave(group_items[g], units)
    for u in tail_units(n_groups - 1):
        u()
    for b in batches:
        state_ref[b] = st[b]


def _rwkv(cols3, lo_feat, p):
    B, S, _ = cols3.shape
    ts = _pick(S, (2 * RW_GROUP * RW_CHUNK, RW_GROUP * RW_CHUNK))
    nt = S // ts
    n_hp = RW_WIDTH // LANES
    cb = lambda col: col // LANES

    def colspec(col0):
        return pl.BlockSpec((B, ts, LANES), lambda h, t: (0, t, cb(col0) + h))

    vec = pl.BlockSpec((1, LANES), lambda h, t: (0, h))
    mat = lambda nrows: pl.BlockSpec((nrows, LANES), lambda h, t: (0, h))
    full = lambda shape: pl.BlockSpec(shape, lambda h, t: (0, 0))
    tri = jnp.asarray(np.tril(np.ones((RW_CHUNK, RW_CHUNK), np.float32)), dtype=BF16)
    return pl.pallas_call(
        _rwkv_kernel,
        grid=(n_hp, nt),
        in_specs=[
            colspec(COL_R), colspec(COL_K), colspec(COL_V),
            pl.BlockSpec((B, ts, RW_LO_FEAT), lambda h, t: (0, t, 0)),
            colspec(COL_GA),
            mat(3),
            vec, mat(RW_DECAY_RANK), vec, mat(RW_ICLR_RANK), mat(RW_GATE_RANK),
            vec, vec, vec, vec, vec,
            full(tri.shape),
        ],
        out_specs=pl.BlockSpec((B, ts, LANES), lambda h, t: (0, t, h)),
        out_shape=jax.ShapeDtypeStruct((B, S, RW_WIDTH), BF16),
        scratch_shapes=[pltpu.VMEM((B, LANES, LANES), F32), pltpu.VMEM((B, 8, LANES), F32)],
        compiler_params=_cparams(("parallel", "arbitrary")),
        name="rwkv7",
    )(cols3, cols3, cols3, lo_feat, cols3, p["mu_rkv"],
      p["w0"], p["w2"], p["a0"], p["a2"], p["g2"],
      p["k_k"], p["k_a"], p["r_k"], p["ln_w"], p["ln_b"], tri)


def _hg_level_matrix():
    c = HG_CHUNK
    m = np.zeros((HG_LEVELS + 1, c, c), np.float32)
    for l in range(HG_LEVELS):
        bs = c >> l
        half = bs // 2
        for t in range(c):
            mid = (t // bs) * bs + half
            if t % bs >= half:
                m[l, t, mid:t + 1] = 1.0
            else:
                m[l, t, t + 1:mid] = 1.0
    m[HG_LEVELS] = np.tril(np.ones((c, c), np.float32))
    return m.reshape((HG_LEVELS + 1) * c, c)


def _hgrn_kernel(q_ref, f_ref, i_ref, g_ref, gb_ref, lbl_ref, ng_ref, lvl_ref,
                 o_ref, state_ref):
    nb, ts, _ = q_ref.shape
    c = HG_CHUNK
    gl = HG_GROUP * c
    n_groups = ts // gl

    @pl.when(pl.program_id(1) == 0)
    def _():
        state_ref[...] = jnp.zeros_like(state_ref)

    logits = lbl_ref[...]
    mx = jnp.max(logits, axis=0, keepdims=True)
    ex = jnp.exp(logits - mx)
    lb = ex[0:1, :] / jnp.sum(ex, axis=0, keepdims=True)

    ri = lax.broadcasted_iota(jnp.int32, (c, c), 0)
    ci = lax.broadcasted_iota(jnp.int32, (c, c), 1)
    rowi = lax.broadcasted_iota(jnp.int32, (c, LANES), 0)
    lvl = lvl_ref[...]
    ng = ng_ref[...]
    second, valid = [], []
    for l in range(HG_LEVELS):
        bs = c >> l
        second.append((rowi & (bs - 1)) >= (bs // 2))
        valid.append(((ri // bs) == (ci // bs)) & ((ri & (bs - 1)) >= (bs // 2))
                     & ((ci & (bs - 1)) < (bs // 2)))

    pre, d, outs = {}, {}, {}

    def prologue(b, g):
        rs = slice(g * gl, (g + 1) * gl)
        f = lb + (1.0 - lb) * _sigmoid(f_ref[b, rs, :].astype(F32))
        qraw = q_ref[b, rs, :].astype(F32)
        pre[b, g] = dict(lf=LOG2E * jnp.log(f), kx=1.0 - f, qs=qraw * _sigmoid(qraw),
                         iv=i_ref[b, rs, :])

    def rows(name, it):
        b, ch = it
        lc = ch % HG_GROUP
        return pre[b, ch // HG_GROUP][name][lc * c:(lc + 1) * c]

    def s_part(its):
        for it in its:
            d["part", it] = _sums2(lvl, rows("lf", it))
            q_c, k_c = rows("qs", it), rows("kx", it)
            dsum = jnp.sum(q_c * k_c, axis=-1, keepdims=True)
            d["sc", it] = jnp.where(ri == ci, dsum, 0.0)

    def s_level(l):
        def stage(its):
            for it in its:
                qk = (jnp.where(second[l], rows("qs", it), rows("kx", it))
                      * jnp.exp2(d["part", it][l * c:(l + 1) * c])).astype(BF16)
                d["sc", it] = jnp.where(valid[l], _dot_nt(qk, qk), d["sc", it])
        return stage

    def s_out(its):
        for it in its:
            bcum = d["part", it][HG_LEVELS * c:(HG_LEVELS + 1) * c]
            b_end = bcum[c - 1:c, :]
            d["o_intra", it] = _dot(d["sc", it], rows("iv", it))
            d["zc", it] = _dot_tn(rows("iv", it), rows("kx", it) * jnp.exp2(b_end - bcum))
            d["qb", it] = rows("qs", it) * jnp.exp2(bcum)
            d["dec", it] = jnp.exp2(b_end)

    stages = [s_part] + [s_level(l) for l in range(HG_LEVELS)] + [s_out]

    def tail(b, g, st):
        rs = slice(g * gl, (g + 1) * gl)
        os_ = []
        for ch in range(g * HG_GROUP, (g + 1) * HG_GROUP):
            it = (b, ch)
            os_.append(d["o_intra", it] + _dot_nt(d["qb", it], st[b]))
            st[b] = st[b] * d["dec", it] + d["zc", it]
        o = jnp.concatenate(os_, axis=0)
        o = o * lax.rsqrt(jnp.mean(o * o, axis=-1, keepdims=True) + NORM_EPS) * ng
        graw = g_ref[b, rs, :].astype(F32)
        o_ref[b, rs, :] = (o * (graw * _sigmoid(graw))
                           * _sigmoid(gb_ref[b, rs, :].astype(F32))).astype(o_ref.dtype)

    batches = list(range(nb))
    group_items = [[(b, ch) for ch in range(g * HG_GROUP, (g + 1) * HG_GROUP) for b in batches]
                   for g in range(n_groups)]
    st = {b: state_ref[b] for b in batches}

    def interleave(stage_items, units):
        per = -(-len(units) // len(stages)) if units else 0
        for n, stage in enumerate(stages):
            stage(stage_items)
            for u in units[n * per:(n + 1) * per]:
                u()

    for b in batches:
        prologue(b, 0)
    for g in range(n_groups):
        units = ([functools.partial(prologue, b, g + 1) for b in batches] if g + 1 < n_groups else [])
        units += ([functools.partial(tail, b, g - 1, st) for b in batches] if g > 0 else [])
        interleave(group_items[g], units)
    for b in batches:
        tail(b, n_groups - 1, st)
        state_ref[b] = st[b]


def _hgrn(cols3, lb_logits, norm_g):
    B, S, _ = cols3.shape
    ts = _pick(S, (2 * HG_GROUP * HG_CHUNK, HG_GROUP * HG_CHUNK))
    nt = S // ts
    cb = lambda col: col // LANES

    def colspec(col0):
        return pl.BlockSpec((B, ts, LANES), lambda h, t: (0, t, cb(col0) + h))

    n_slots = lb_logits.shape[0]
    lvl = jnp.asarray(_hg_level_matrix(), dtype=BF16)
    return pl.pallas_call(
        _hgrn_kernel,
        grid=(HG_HEADS, nt),
        in_specs=[
            colspec(COL_HQ), colspec(COL_HF), colspec(COL_HI), colspec(COL_HG), colspec(COL_GB),
            pl.BlockSpec((n_slots, LANES), lambda h, t: (0, h)),
            pl.BlockSpec((1, LANES), lambda h, t: (0, 0)),
            pl.BlockSpec(lvl.shape, lambda h, t: (0, 0)),
        ],
        out_specs=pl.BlockSpec((B, ts, LANES), lambda h, t: (0, t, h)),
        out_shape=jax.ShapeDtypeStruct((B, S, D_MODEL), BF16),
        scratch_shapes=[pltpu.VMEM((B, HG_KEY, HG_KEY), F32)],
        compiler_params=_cparams(("parallel", "arbitrary")),
        name="hgrn2",
    )(cols3, cols3, cols3, cols3, cols3, lb_logits, norm_g, lvl)


def _memkv_kernel(m_ref, g_ref, wk_ref, wv_ref, k_ref, v_ref):
    m = _rms(m_ref[...], g_ref[...]).astype(BF16)
    k_ref[...] = jnp.dot(m, wk_ref[...], preferred_element_type=F32).astype(BF16)
    v_ref[...] = jnp.dot(m, wv_ref[...], preferred_element_type=F32).astype(BF16)


def _memkv(mem2, g, wk, wv, B, n_mem):
    row = pl.BlockSpec((n_mem, D_MODEL), lambda b: (b, 0))
    wspec = pl.BlockSpec((D_MODEL, D_MODEL), lambda b: (0, 0))
    sds = jax.ShapeDtypeStruct((B * n_mem, D_MODEL), BF16)
    return pl.pallas_call(
        _memkv_kernel,
        grid=(B,),
        in_specs=[row, pl.BlockSpec((1, D_MODEL), lambda b: (0, 0)), wspec, wspec],
        out_specs=[row, row],
        out_shape=[sds, sds],
        compiler_params=_cparams(("parallel",)),
        name="memkv",
    )(mem2, g, wk, wv)


def _xattn_kernel(x_ref, ya_ref, yb_ref, wout_ref, g_ref, wq_ref, k_ref, v_ref, wo_ref, o_ref):
    h = x_ref[...] + _dot(ya_ref[...].astype(F32) + yb_ref[...].astype(F32), wout_ref[...])
    u = _rms(h, g_ref[...]).astype(BF16)
    q = jnp.dot(u, wq_ref[...], preferred_element_type=F32)
    k = k_ref[...]
    v = v_ref[...]
    heads = []
    for hd in range(XA_HEADS):
        sl = slice(hd * XA_HEAD, (hd + 1) * XA_HEAD)
        s = _dot_nt(q[:, sl], k[:, sl]) * (XA_HEAD ** -0.5)
        s = s - jnp.max(s, axis=-1, keepdims=True)
        e = jnp.exp(s)
        p = e / jnp.sum(e, axis=-1, keepdims=True)
        heads.append(_dot(p, v[:, sl]))
    o = jnp.concatenate(heads, axis=-1)
    o_ref[...] = h + _dot(o, wo_ref[...])


def _xattn(x2, ya, yb, wout, g, wq, km, vm, wo, B, S, n_mem):
    tm = _pick(S, (512, 256))
    nt = S // tm
    row = pl.BlockSpec((tm, D_MODEL), lambda b, t: (b * nt + t, 0))
    wspec = pl.BlockSpec((D_MODEL, D_MODEL), lambda b, t: (0, 0))
    kv = pl.BlockSpec((n_mem, D_MODEL), lambda b, t: (b, 0))
    return pl.pallas_call(
        _xattn_kernel,
        grid=(B, nt),
        in_specs=[row, row, row, wspec, pl.BlockSpec((1, D_MODEL), lambda b, t: (0, 0)), wspec, kv, kv,
                  wspec],
        out_specs=row,
        out_shape=jax.ShapeDtypeStruct((B * S, D_MODEL), F32),
        compiler_params=_cparams(("parallel", "parallel")),
        name="xattn",
    )(x2, ya, yb, wout, g, wq, km, vm, wo)


def _ffn_kernel(h_ref, g_ref, w1_ref, w3_ref, w2_ref, gf_ref, o_ref):
    h = h_ref[...]
    u = _rms(h, g_ref[...]).astype(BF16)
    a = jnp.dot(u, w1_ref[...], preferred_element_type=F32)
    b = jnp.dot(u, w3_ref[...], preferred_element_type=F32)
    mid = (a * _sigmoid(a)) * b
    o_ref[...] = _rms(h + _dot(mid, w2_ref[...]), gf_ref[...])


def _ffn(h2, g, w1, w3, w2, gf):
    T = h2.shape[0]
    tm = _pick(T, (512, 256))
    row = pl.BlockSpec((tm, D_MODEL), lambda i: (i, 0))
    vec = pl.BlockSpec((1, D_MODEL), lambda i: (0, 0))
    resident = lambda shape: pl.BlockSpec(shape, lambda i: (0, 0), pipeline_mode=pl.Buffered(1))
    return pl.pallas_call(
        _ffn_kernel,
        grid=(T // tm,),
        in_specs=[row, vec, resident((D_MODEL, D_FF)), resident((D_MODEL, D_FF)),
                  resident((D_FF, D_MODEL)), vec],
        out_specs=row,
        out_shape=jax.ShapeDtypeStruct((T, D_MODEL), F32),
        compiler_params=_cparams(("parallel",)),
        name="ffn",
    )(h2, g, w1, w3, w2, gf)


def _pack_in_weights(w_in):
    rw_cols = 3 * RW_WIDTH + RW_LO
    pad = jnp.zeros((D_MODEL, RW_LO_PAD - RW_LO), w_in.dtype)
    packed = jnp.concatenate([w_in[:, :rw_cols], pad, w_in[:, rw_cols:]], axis=1)
    return packed.astype(BF16)


def _pack_mu_lo(mu):
    lo = mu[3 * RW_WIDTH:]
    return jnp.concatenate([lo, jnp.zeros((RW_LO_PAD - RW_LO,), mu.dtype)])[None, :].astype(F32)


def kernel(x, mem, norm_mix_g, w_in, rw_mu, rw_w0, rw_w2, rw_a0, rw_a2, rw_g2, rw_k_k, rw_k_a, rw_r_k, rw_ln_w, rw_ln_b, hg_lb_logits, hg_norm_g, w_out, norm_xa_g, norm_mem_g, xa_wq, xa_wk, xa_wv, xa_wo, norm_ffn_g, ffn_w1, ffn_w3, ffn_w2, norm_final_g):
    B, S, _ = x.shape
    n_mem = mem.shape[1]
    depth = w_in.shape[0]
    assert depth == 1, "single-layer block"
    assert S % (RW_GROUP * RW_CHUNK) == 0 and S % (HG_GROUP * HG_CHUNK) == 0
    l = 0
    T = B * S
    row = lambda a: a.reshape(1, -1).astype(F32)

    x2 = x.reshape(T, D_MODEL)
    cols = _inproj(x2, row(norm_mix_g[l]), _pack_in_weights(w_in[l]))
    cols3 = cols.reshape(B, S, N_COLS)

    mu = rw_mu[l].astype(F32)
    lo_feat = _lofeat(cols3, _pack_mu_lo(mu))
    rw = dict(
        mu_rkv=mu[:3 * RW_WIDTH].reshape(3, RW_WIDTH),
        w0=row(rw_w0[l]), w2=rw_w2[l].astype(BF16), a0=row(rw_a0[l]), a2=rw_a2[l].astype(BF16),
        g2=rw_g2[l].astype(BF16), k_k=row(rw_k_k[l]), k_a=row(rw_k_a[l]), r_k=row(rw_r_k[l]),
        ln_w=row(rw_ln_w[l]), ln_b=row(rw_ln_b[l]))
    ya = _rwkv(cols3, lo_feat, rw).reshape(T, D_MODEL)
    yb = _hgrn(cols3, hg_lb_logits.astype(F32), row(hg_norm_g[l])).reshape(T, D_MODEL)

    km, vm = _memkv(mem.reshape(B * n_mem, D_MODEL), row(norm_mem_g[l]),
                    xa_wk[l].astype(BF16), xa_wv[l].astype(BF16), B, n_mem)
    h2 = _xattn(x2, ya, yb, w_out[l].astype(BF16), row(norm_xa_g[l]), xa_wq[l].astype(BF16), km, vm,
                xa_wo[l].astype(BF16), B, S, n_mem)

    out = _ffn(h2, row(norm_ffn_g[l]), ffn_w1[l].astype(BF16), ffn_w3[l].astype(BF16),
               ffn_w2[l].astype(BF16), row(norm_final_g))
    return out.reshape(B, S, D_MODEL)
```

```python
import functools

import numpy as np
import jax
import jax.numpy as jnp
from jax import lax
from jax.experimental import pallas as pl
from jax.experimental.pallas import tpu as pltpu

F32 = jnp.float32
BF16 = jnp.bfloat16

D_MODEL = 1024
NORM_EPS = 1e-6
LANES = 128

RW_HEAD = 64
RW_WIDTH = D_MODEL
RW_DECAY_RANK = 64
RW_ICLR_RANK = 64
RW_GATE_RANK = 160
RW_LN_EPS = 64e-5
RW_LO = RW_DECAY_RANK + RW_ICLR_RANK + RW_GATE_RANK
RW_LO_PAD = 512
RW_LO_FEAT = 384
RW_CHUNK = 64
RW_SUB = 16
RW_GROUP = 4
LOG2E = 1.4426950408889634

HG_KEY = 128
HG_HEADS = D_MODEL // HG_KEY
HG_CHUNK = 64
HG_LEVELS = 6
HG_GROUP = 4

XA_HEADS = 4
XA_HEAD = D_MODEL // XA_HEADS
D_FF = 2816

COL_R = 0
COL_K = COL_R + RW_WIDTH
COL_V = COL_K + RW_WIDTH
COL_LO = COL_V + RW_WIDTH
COL_HQ = COL_LO + RW_LO_PAD
COL_HF = COL_HQ + D_MODEL
COL_HI = COL_HF + D_MODEL
COL_HG = COL_HI + D_MODEL
COL_GA = COL_HG + D_MODEL
COL_GB = COL_GA + D_MODEL
N_COLS = COL_GB + D_MODEL

COLS_DTYPE = BF16
IN_COL_CHUNKS = 4

VMEM_LIMIT = 56 * 1024 * 1024


def _cparams(sem):
    return pltpu.CompilerParams(dimension_semantics=sem, vmem_limit_bytes=VMEM_LIMIT)


def _sigmoid(x):
    return 1.0 / (1.0 + jnp.exp(-x))


def _softplus(x):
    return jnp.maximum(x, 0.0) + jnp.log(1.0 + jnp.exp(-jnp.abs(x)))


def _rms(x, g):
    ms = jnp.mean(x * x, axis=-1, keepdims=True)
    return x * lax.rsqrt(ms + NORM_EPS) * g


def _dot(a, b):
    return jnp.dot(a.astype(BF16), b.astype(BF16), preferred_element_type=F32)


def _dot_nt(a, b):
    return lax.dot_general(a.astype(BF16), b.astype(BF16), (((1,), (1,)), ((), ())),
                           preferred_element_type=F32)


def _dot_tn(a, b):
    return lax.dot_general(a.astype(BF16), b.astype(BF16), (((0,), (0,)), ((), ())),
                           preferred_element_type=F32)


def _split3(x):
    hi = x.astype(BF16)
    r1 = x - hi.astype(F32)
    mid = r1.astype(BF16)
    lo = (r1 - mid.astype(F32)).astype(BF16)
    return jnp.concatenate([hi, mid, lo], axis=1)


def _split2(x):
    hi = x.astype(BF16)
    mid = (x - hi.astype(F32)).astype(BF16)
    return jnp.concatenate([hi, mid], axis=1)


def _sums2(pattern_bf16, x):
    y = jnp.dot(pattern_bf16, _split2(x), preferred_element_type=F32)
    return y[:, 0:LANES] + y[:, LANES:2 * LANES]


def _exact_sums(pattern_bf16, x):
    y = jnp.dot(pattern_bf16, _split3(x), preferred_element_type=F32)
    return y[:, 0:LANES] + y[:, LANES:2 * LANES] + y[:, 2 * LANES:3 * LANES]


def _pick(n, prefs):
    for p in prefs:
        if n % p == 0:
            return p
    return n


def _inproj_kernel(x_ref, g_ref, w_ref, o_ref):
    u = _rms(x_ref[...], g_ref[...]).astype(BF16)
    for j in range(IN_COL_CHUNKS):
        cs = slice(j * (N_COLS // IN_COL_CHUNKS), (j + 1) * (N_COLS // IN_COL_CHUNKS))
        o_ref[:, cs] = lax.dot_general(u, w_ref[cs, :], (((1,), (1,)), ((), ())),
                                       preferred_element_type=F32).astype(o_ref.dtype)


def _inproj(x2, g, w_packed_t):
    T = x2.shape[0]
    tm = _pick(T, (512, 256))
    return pl.pallas_call(
        _inproj_kernel,
        grid=(T // tm,),
        in_specs=[
            pl.BlockSpec((tm, D_MODEL), lambda i: (i, 0)),
            pl.BlockSpec((1, D_MODEL), lambda i: (0, 0)),
            pl.BlockSpec((N_COLS, D_MODEL), lambda i: (0, 0), pipeline_mode=pl.Buffered(1)),
        ],
        out_specs=pl.BlockSpec((tm, N_COLS), lambda i: (i, 0)),
        out_shape=jax.ShapeDtypeStruct((T, N_COLS), COLS_DTYPE),
        compiler_params=_cparams(("parallel",)),
        name="inproj",
    )(x2, g, w_packed_t)


def _shift_mix(raw, carry_row, mu):
    rolled = pltpu.roll(raw, 1, axis=0)
    row = lax.broadcasted_iota(jnp.int32, raw.shape, 0)
    prev = jnp.where(row == 0, carry_row, rolled)
    return raw + mu * (prev - raw)


def _lofeat_kernel(lo_ref, mu_ref, o_ref, carry_ref):
    ts = lo_ref.shape[1]

    @pl.when(pl.program_id(1) == 0)
    def _():
        carry_ref[...] = jnp.zeros_like(carry_ref)

    raw = lo_ref[0].astype(F32)
    lo = _shift_mix(raw, carry_ref[0:1, :], mu_ref[...])
    carry_ref[0:1, :] = raw[ts - 1:ts, :]
    lane = lax.broadcasted_iota(jnp.int32, (ts, RW_LO_FEAT), 1)
    x = lo[:, 0:RW_LO_FEAT]
    feat = jnp.where(lane < RW_DECAY_RANK, jnp.tanh(x),
                     jnp.where(lane < LANES, x, _sigmoid(x)))
    o_ref[0] = feat.astype(o_ref.dtype)


def _lofeat(cols3, mu_lo):
    B, S, _ = cols3.shape
    ts = _pick(S, (1024, 512, 256, 128, 64))
    return pl.pallas_call(
        _lofeat_kernel,
        grid=(B, S // ts),
        in_specs=[
            pl.BlockSpec((1, ts, RW_LO_PAD), lambda b, t: (b, t, COL_LO // RW_LO_PAD)),
            pl.BlockSpec((1, RW_LO_PAD), lambda b, t: (0, 0)),
        ],
        out_specs=pl.BlockSpec((1, ts, RW_LO_FEAT), lambda b, t: (b, t, 0)),
        out_shape=jax.ShapeDtypeStruct((B, S, RW_LO_FEAT), BF16),
        scratch_shapes=[pltpu.VMEM((8, RW_LO_PAD), F32)],
        compiler_params=_cparams(("parallel", "arbitrary")),
        name="lofeat",
    )(cols3, mu_lo)


def _pair_sum(x):
    lane = lax.broadcasted_iota(jnp.int32, x.shape, 1)
    first = lane < RW_HEAD
    s0 = jnp.sum(jnp.where(first, x, 0.0), axis=-1, keepdims=True)
    s1 = jnp.sum(jnp.where(first, 0.0, x), axis=-1, keepdims=True)
    return jnp.where(first, s0, s1)


def _rwkv_kernel(r_ref, k_ref, v_ref, lo_ref, ga_ref, mu_ref,
                 w0_ref, w2_ref, a0_ref, a2_ref, g2_ref,
                 kk_ref, ka_ref, rk_ref, lnw_ref, lnb_ref, tri_ref,
                 o_ref, state_ref, carry_ref):
    nb, ts, _ = r_ref.shape
    c = RW_CHUNK
    n_chunks = ts // c

    @pl.when(pl.program_id(1) == 0)
    def _():
        state_ref[...] = jnp.zeros_like(state_ref)
        carry_ref[...] = jnp.zeros_like(carry_ref)

    t_i = lax.broadcasted_iota(jnp.int32, (c, LANES), 0)
    s_i = lax.broadcasted_iota(jnp.int32, (c, LANES), 1) & (RW_HEAD - 1)
    strict = t_i > s_i
    incl = t_i >= s_i
    eye = (t_i == s_i).astype(F32)
    same_sub = (t_i // RW_SUB) == (s_i // RW_SUB)
    rr = lax.broadcasted_iota(jnp.int32, (LANES, LANES), 0)
    cc = lax.broadcasted_iota(jnp.int32, (LANES, LANES), 1)
    bd_mask = (rr // RW_HEAD) == (cc // RW_HEAD)

    def bd(x):
        return jnp.where(bd_mask, jnp.concatenate([x, x], axis=0), 0.0).astype(BF16)

    def pmul(a, bmat):
        return _dot(a, bd(bmat))

    tri = tri_ref[...]
    pre, pc, d, ys = {}, {}, {}, {}

    gl = RW_GROUP * c
    n_groups = ts // gl
    last_raw = {}

    def prologue(b, g):
        rs = slice(g * gl, (g + 1) * gl)
        rkv = []
        for n, ref in enumerate((r_ref, k_ref, v_ref)):
            raw = ref[b, rs, :].astype(F32)
            carry = carry_ref[b, n:n + 1, :] if g == 0 else last_raw[b, n]
            rkv.append(_shift_mix(raw, carry, mu_ref[n:n + 1, :]))
            last_raw[b, n] = raw[gl - 1:gl, :]
            if g == n_groups - 1:
                carry_ref[b, n:n + 1, :] = last_raw[b, n]
        r, k, v = rkv
        lo = lo_ref[b, rs, :]
        wl = w0_ref[...] + _dot(lo[:, 0:RW_DECAY_RANK], w2_ref[...])
        lw = -LOG2E * jnp.exp(-_softplus(-wl) - 0.5)
        a = _sigmoid(a0_ref[...] + _dot(lo[:, RW_DECAY_RANK:LANES], a2_ref[...]))
        gate = _dot(lo[:, LANES:LANES + RW_GATE_RANK], g2_ref[...])
        kk = k * kk_ref[...]
        kkn = kk * lax.rsqrt(jnp.maximum(_pair_sum(kk * kk), 1e-24))
        kmod = k * (1.0 + (a - 1.0) * ka_ref[...])
        pre[b, g] = dict(r=r, v=v, kmod=kmod, g=gate, lw=lw, kkn=kkn, beta=kkn * a)

    def rows(name, it):
        b, ch = it
        lc = ch % RW_GROUP
        return pre[b, ch // RW_GROUP][name][lc * c:(lc + 1) * c]

    def prep(it):
        lw_c = rows("lw", it)
        cum = _exact_sums(tri, lw_c)
        c_end = cum[c - 1:c, :]
        e_neg = jnp.exp2(-cum)
        e_end = jnp.exp2(c_end - cum)
        beta_c, kmod_c = rows("beta", it), rows("kmod", it)
        pc[it] = dict(
            rt=rows("r", it) * jnp.exp2(cum), at=-rows("kkn", it) * jnp.exp2(cum - lw_c),
            bt=beta_c * e_neg, kt=kmod_c * e_neg, bh=beta_c * e_end, kh=kmod_c * e_end,
            w_end=jnp.exp2(c_end))

    def s_amat(its):
        for it in its:
            d["lhs2", it] = jnp.concatenate([pc[it]["at"], pc[it]["rt"]], axis=0)
            d["amat", it] = _dot_nt(d["lhs2", it], jnp.concatenate(
                [bd(pc[it]["bt"]), bd(pc[it]["kt"])], axis=0))

    def s_split(its):
        for it in its:
            am = d["amat", it]
            n_ab = jnp.where(strict, am[0:c, 0:LANES], 0.0)
            d["a_r2", it] = jnp.concatenate(
                [jnp.where(incl, am[c:2 * c, 0:LANES], 0.0),
                 jnp.where(incl, am[c:2 * c, LANES:2 * LANES], 0.0)], axis=1)
            d["bdv", it] = bd(rows("v", it))
            d["u0", it] = _dot(jnp.where(strict, am[0:c, LANES:2 * LANES], 0.0), d["bdv", it])
            d["dg", it] = jnp.where(same_sub, n_ab, 0.0)
            d["off", it] = n_ab - d["dg", it]

    def s_d2(its):
        for it in its:
            d["d2", it] = pmul(d["dg", it], d["dg", it])
            d["x", it] = eye + d["dg", it]

    def s_x(pw, nxt):
        def stage(its):
            for it in its:
                if nxt:
                    both = pmul(jnp.concatenate([d["x", it], d[pw, it]], axis=0), d[pw, it])
                    d["x", it] = d["x", it] + both[0:c]
                    d[nxt, it] = both[c:2 * c]
                else:
                    d["x", it] = d["x", it] + pmul(d["x", it], d[pw, it])
        return stage

    def s_r(its):
        for it in its:
            d["r", it] = pmul(d["off", it], d["x", it])

    def s_r2(its):
        for it in its:
            both = pmul(jnp.concatenate([d["x", it], d["r", it]], axis=0), d["r", it])
            d["z", it] = d["x", it] + both[0:c]
            d["r2", it] = both[c:2 * c]

    def s_tinv(its):
        for it in its:
            d["t_inv", it] = d["z", it] + pmul(d["z", it], d["r2", it])

    def s_au(its):
        for it in its:
            au = _dot(d["t_inv", it], jnp.concatenate([bd(pc[it]["at"]), bd(d["u0", it])], axis=1))
            d["ahat", it], d["uu", it] = au[:, 0:LANES], au[:, LANES:2 * LANES]

    def s_trans(its):
        for it in its:
            lhs = jnp.concatenate(
                [jnp.concatenate([d["ahat", it], d["uu", it]], axis=1),
                 jnp.concatenate([jnp.zeros((c, LANES), F32), rows("v", it)], axis=1)], axis=0)
            mg = _dot_tn(lhs, jnp.concatenate([pc[it]["bh"], pc[it]["kh"]], axis=0))
            d["mmat", it] = jnp.where(bd_mask, mg[0:LANES], 0.0).astype(BF16)
            d["gmat", it] = jnp.where(bd_mask, mg[LANES:2 * LANES], 0.0)
            d["rhat", it] = pc[it]["rt"] + pmul(d["a_r2", it][:, 0:LANES], d["ahat", it])
            d["y0", it] = _dot(d["a_r2", it], jnp.concatenate([bd(d["uu", it]), d["bdv", it]], axis=0))

    stages = [s_amat, s_split, s_d2, s_x("d2", "d4"), s_x("d4", "d8"), s_x("d8", None),
              s_r, s_r2, s_tinv, s_au, s_trans]

    def chain_step(bs, ch, st):
        st_bf = {b: st[b].astype(BF16) for b in bs}
        for b in bs:
            ys[b, ch] = d["y0", (b, ch)] + _dot_nt(d["rhat", (b, ch)], st_bf[b])
        for b in bs:
            st[b] = (st[b] * pc[b, ch]["w_end"] + _dot(st_bf[b], d["mmat", (b, ch)])
                     + d["gmat", (b, ch)])

    def epilogue(b, g):
        rs = slice(g * gl, (g + 1) * gl)
        y = jnp.concatenate([ys[b, ch] for ch in range(g * RW_GROUP, (g + 1) * RW_GROUP)], axis=0)
        pb = pre[b, g]
        mean = _pair_sum(y) * (1.0 / RW_HEAD)
        dlt = y - mean
        var = _pair_sum(dlt * dlt) * (1.0 / RW_HEAD)
        on = dlt * lax.rsqrt(var + RW_LN_EPS) * lnw_ref[...] + lnb_ref[...]
        bonus = _pair_sum(pb["r"] * pb["kmod"] * rk_ref[...]) * pb["v"]
        o_ref[b, rs, :] = (((on + bonus) * pb["g"])
                           * _sigmoid(ga_ref[b, rs, :].astype(F32))).astype(o_ref.dtype)

    batches = list(range(nb))
    group_items = [[(b, ch) for ch in range(g * RW_GROUP, (g + 1) * RW_GROUP) for b in batches]
                   for g in range(n_groups)]
    st = {}

    def prep_units(g):
        return ([functools.partial(prologue, b, g) for b in batches]
                + [functools.partial(prep, it) for it in group_items[g]])

    def tail_units(g):
        units = [functools.partial(chain_step, batches, ch, st)
                 for ch in range(g * RW_GROUP, (g + 1) * RW_GROUP)]
        return units + [functools.partial(epilogue, b, g) for b in batches]

    def interleave(stage_items, units):
        per = -(-len(units) // len(stages)) if units else 0
        for n, stage in enumerate(stages):
            stage(stage_items)
            for u in units[n * per:(n + 1) * per]:
                u()

    for b in batches:
        st[b] = state_ref[b]
    for u in prep_units(0):
        u()
    for g in range(n_groups):
        units = (prep_units(g + 1) if g + 1 < n_groups else []) + (tail_units(g - 1) if g > 0 else [])
        interleave(group_items[g], units)
    for u in tail_units(n_groups - 1):
        u()
    for b in batches:
        state_ref[b] = st[b]


def _rwkv(cols3, lo_feat, p):
    B, S, _ = cols3.shape
    ts = _pick(S, (2 * RW_GROUP * RW_CHUNK, RW_GROUP * RW_CHUNK))
    nt = S // ts
    n_hp = RW_WIDTH // LANES
    cb = lambda col: col // LANES

    def colspec(col0):
        return pl.BlockSpec((B, ts, LANES), lambda h, t: (0, t, cb(col0) + h))

    vec = pl.BlockSpec((1, LANES), lambda h, t: (0, h))
    mat = lambda nrows: pl.BlockSpec((nrows, LANES), lambda h, t: (0, h))
    full = lambda shape: pl.BlockSpec(shape, lambda h, t: (0, 0))
    tri = jnp.asarray(np.tril(np.ones((RW_CHUNK, RW_CHUNK), np.float32)), dtype=BF16)
    return pl.pallas_call(
        _rwkv_kernel,
        grid=(n_hp, nt),
        in_specs=[
            colspec(COL_R), colspec(COL_K), colspec(COL_V),
            pl.BlockSpec((B, ts, RW_LO_FEAT), lambda h, t: (0, t, 0)),
            colspec(COL_GA),
            mat(3),
            vec, mat(RW_DECAY_RANK), vec, mat(RW_ICLR_RANK), mat(RW_GATE_RANK),
            vec, vec, vec, vec, vec,
            full(tri.shape),
        ],
        out_specs=pl.BlockSpec((B, ts, LANES), lambda h, t: (0, t, h)),
        out_shape=jax.ShapeDtypeStruct((B, S, RW_WIDTH), BF16),
        scratch_shapes=[pltpu.VMEM((B, LANES, LANES), F32), pltpu.VMEM((B, 8, LANES), F32)],
        compiler_params=_cparams(("parallel", "arbitrary")),
        name="rwkv7",
    )(cols3, cols3, cols3, lo_feat, cols3, p["mu_rkv"],
      p["w0"], p["w2"], p["a0"], p["a2"], p["g2"],
      p["k_k"], p["k_a"], p["r_k"], p["ln_w"], p["ln_b"], tri)


def _hg_level_matrix():
    c = HG_CHUNK
    m = np.zeros((HG_LEVELS + 1, c, c), np.float32)
    for l in range(HG_LEVELS):
        bs = c >> l
        half = bs // 2
        for t in range(c):
            mid = (t // bs) * bs + half
            if t % bs >= half:
                m[l, t, mid:t + 1] = 1.0
            else:
                m[l, t, t + 1:mid] = 1.0
    m[HG_LEVELS] = np.tril(np.ones((c, c), np.float32))
    return m.reshape((HG_LEVELS + 1) * c, c)


def _hgrn_kernel(q_ref, f_ref, i_ref, g_ref, gb_ref, lbl_ref, ng_ref, lvl_ref,
                 o_ref, state_ref):
    nb, ts, _ = q_ref.shape
    c = HG_CHUNK
    gl = HG_GROUP * c
    n_groups = ts // gl

    @pl.when(pl.program_id(1) == 0)
    def _():
        state_ref[...] = jnp.zeros_like(state_ref)

    logits = lbl_ref[...]
    mx = jnp.max(logits, axis=0, keepdims=True)
    ex = jnp.exp(logits - mx)
    lb = ex[0:1, :] / jnp.sum(ex, axis=0, keepdims=True)

    ri = lax.broadcasted_iota(jnp.int32, (c, c), 0)
    ci = lax.broadcasted_iota(jnp.int32, (c, c), 1)
    rowi = lax.broadcasted_iota(jnp.int32, (c, LANES), 0)
    lvl = lvl_ref[...]
    ng = ng_ref[...]
    second, valid = [], []
    for l in range(HG_LEVELS):
        bs = c >> l
        second.append((rowi & (bs - 1)) >= (bs // 2))
        valid.append(((ri // bs) == (ci // bs)) & ((ri & (bs - 1)) >= (bs // 2))
                     & ((ci & (bs - 1)) < (bs // 2)))

    pre, d, outs = {}, {}, {}

    def prologue(b, g):
        rs = slice(g * gl, (g + 1) * gl)
        f = lb + (1.0 - lb) * _sigmoid(f_ref[b, rs, :].astype(F32))
        qraw = q_ref[b, rs, :].astype(F32)
        pre[b, g] = dict(lf=LOG2E * jnp.log(f), kx=1.0 - f, qs=qraw * _sigmoid(qraw),
                         iv=i_ref[b, rs, :])

    def rows(name, it):
        b, ch = it
        lc = ch % HG_GROUP
        return pre[b, ch // HG_GROUP][name][lc * c:(lc + 1) * c]

    def s_part(its):
        for it in its:
            d["part", it] = _sums2(lvl, rows("lf", it))
            q_c, k_c = rows("qs", it), rows("kx", it)
            dsum = jnp.sum(q_c * k_c, axis=-1, keepdims=True)
            d["sc", it] = jnp.where(ri == ci, dsum, 0.0)

    def s_level(l):
        def stage(its):
            for it in its:
                qk = (jnp.where(second[l], rows("qs", it), rows("kx", it))
                      * jnp.exp2(d["part", it][l * c:(l + 1) * c])).astype(BF16)
                d["sc", it] = jnp.where(valid[l], _dot_nt(qk, qk), d["sc", it])
        return stage

    def s_out(its):
        for it in its:
            bcum = d["part", it][HG_LEVELS * c:(HG_LEVELS + 1) * c]
            b_end = bcum[c - 1:c, :]
            d["o_intra", it] = _dot(d["sc", it], rows("iv", it))
            d["zc", it] = _dot_tn(rows("iv", it), rows("kx", it) * jnp.exp2(b_end - bcum))
            d["qb", it] = rows("qs", it) * jnp.exp2(bcum)
            d["dec", it] = jnp.exp2(b_end)

    stages = [s_part] + [s_level(l) for l in range(HG_LEVELS)] + [s_out]

    def tail(b, g, st):
        rs = slice(g * gl, (g + 1) * gl)
        os_ = []
        for ch in range(g * HG_GROUP, (g + 1) * HG_GROUP):
            it = (b, ch)
            os_.append(d["o_intra", it] + _dot_nt(d["qb", it], st[b]))
            st[b] = st[b] * d["dec", it] + d["zc", it]
        o = jnp.concatenate(os_, axis=0)
        o = o * lax.rsqrt(jnp.mean(o * o, axis=-1, keepdims=True) + NORM_EPS) * ng
        graw = g_ref[b, rs, :].astype(F32)
        o_ref[b, rs, :] = (o * (graw * _sigmoid(graw))
                           * _sigmoid(gb_ref[b, rs, :].astype(F32))).astype(o_ref.dtype)

    batches = list(range(nb))
    group_items = [[(b, ch) for ch in range(g * HG_GROUP, (g + 1) * HG_GROUP) for b in batches]
                   for g in range(n_groups)]
    st = {b: state_ref[b] for b in batches}

    def interleave(stage_items, units):
        per = -(-len(units) // len(stages)) if units else 0
        for n, stage in enumerate(stages):
            stage(stage_items)
            for u in units[n * per:(n + 1) * per]:
                u()

    for b in batches:
        prologue(b, 0)
    for g in range(n_groups):
        units = ([functools.partial(prologue, b, g + 1) for b in batches] if g + 1 < n_groups else [])
        units += ([functools.partial(tail, b, g - 1, st) for b in batches] if g > 0 else [])
        interleave(group_items[g], units)
    for b in batches:
        tail(b, n_groups - 1, st)
        state_ref[b] = st[b]


def _hgrn(cols3, lb_logits, norm_g):
    B, S, _ = cols3.shape
    ts = _pick(S, (2 * HG_GROUP * HG_CHUNK, HG_GROUP * HG_CHUNK))
    nt = S // ts
    cb = lambda col: col // LANES

    def colspec(col0):
        return pl.BlockSpec((B, ts, LANES), lambda h, t: (0, t, cb(col0) + h))

    n_slots = lb_logits.shape[0]
    lvl = jnp.asarray(_hg_level_matrix(), dtype=BF16)
    return pl.pallas_call(
        _hgrn_kernel,
        grid=(HG_HEADS, nt),
        in_specs=[
            colspec(COL_HQ), colspec(COL_HF), colspec(COL_HI), colspec(COL_HG), colspec(COL_GB),
            pl.BlockSpec((n_slots, LANES), lambda h, t: (0, h)),
            pl.BlockSpec((1, LANES), lambda h, t: (0, 0)),
            pl.BlockSpec(lvl.shape, lambda h, t: (0, 0)),
        ],
        out_specs=pl.BlockSpec((B, ts, LANES), lambda h, t: (0, t, h)),
        out_shape=jax.ShapeDtypeStruct((B, S, D_MODEL), BF16),
        scratch_shapes=[pltpu.VMEM((B, HG_KEY, HG_KEY), F32)],
        compiler_params=_cparams(("parallel", "arbitrary")),
        name="hgrn2",
    )(cols3, cols3, cols3, cols3, cols3, lb_logits, norm_g, lvl)


def _memkv_kernel(m_ref, g_ref, wk_ref, wv_ref, k_ref, v_ref):
    m = _rms(m_ref[...], g_ref[...]).astype(BF16)
    k_ref[...] = jnp.dot(m, wk_ref[...], preferred_element_type=F32).astype(BF16)
    v_ref[...] = jnp.dot(m, wv_ref[...], preferred_element_type=F32).astype(BF16)


def _memkv(mem2, g, wk, wv, B, n_mem):
    row = pl.BlockSpec((n_mem, D_MODEL), lambda b: (b, 0))
    wspec = pl.BlockSpec((D_MODEL, D_MODEL), lambda b: (0, 0))
    sds = jax.ShapeDtypeStruct((B * n_mem, D_MODEL), BF16)
    return pl.pallas_call(
        _memkv_kernel,
        grid=(B,),
        in_specs=[row, pl.BlockSpec((1, D_MODEL), lambda b: (0, 0)), wspec, wspec],
        out_specs=[row, row],
        out_shape=[sds, sds],
        compiler_params=_cparams(("parallel",)),
        name="memkv",
    )(mem2, g, wk, wv)


def _xattn_kernel(x_ref, ya_ref, yb_ref, wout_ref, g_ref, wq_ref, k_ref, v_ref, wo_ref, o_ref):
    h = x_ref[...] + _dot(ya_ref[...].astype(F32) + yb_ref[...].astype(F32), wout_ref[...])
    u = _rms(h, g_ref[...]).astype(BF16)
    q = jnp.dot(u, wq_ref[...], preferred_element_type=F32)
    k = k_ref[...]
    v = v_ref[...]
    heads = []
    for hd in range(XA_HEADS):
        sl = slice(hd * XA_HEAD, (hd + 1) * XA_HEAD)
        s = _dot_nt(q[:, sl], k[:, sl]) * (XA_HEAD ** -0.5)
        s = s - jnp.max(s, axis=-1, keepdims=True)
        e = jnp.exp(s)
        p = e / jnp.sum(e, axis=-1, keepdims=True)
        heads.append(_dot(p, v[:, sl]))
    o = jnp.concatenate(heads, axis=-1)
    o_ref[...] = h + _dot(o, wo_ref[...])


def _xattn(x2, ya, yb, wout, g, wq, km, vm, wo, B, S, n_mem):
    tm = _pick(S, (512, 256))
    nt = S // tm
    row = pl.BlockSpec((tm, D_MODEL), lambda b, t: (b * nt + t, 0))
    wspec = pl.BlockSpec((D_MODEL, D_MODEL), lambda b, t: (0, 0))
    kv = pl.BlockSpec((n_mem, D_MODEL), lambda b, t: (b, 0))
    return pl.pallas_call(
        _xattn_kernel,
        grid=(B, nt),
        in_specs=[row, row, row, wspec, pl.BlockSpec((1, D_MODEL), lambda b, t: (0, 0)), wspec, kv, kv,
                  wspec],
        out_specs=row,
        out_shape=jax.ShapeDtypeStruct((B * S, D_MODEL), F32),
        compiler_params=_cparams(("parallel", "parallel")),
        name="xattn",
    )(x2, ya, yb, wout, g, wq, km, vm, wo)


def _ffn_kernel(h_ref, g_ref, w1_ref, w3_ref, w2_ref, gf_ref, o_ref):
    h = h_ref[...]
    u = _rms(h, g_ref[...]).astype(BF16)
    a = jnp.dot(u, w1_ref[...], preferred_element_type=F32)
    b = jnp.dot(u, w3_ref[...], preferred_element_type=F32)
    mid = (a * _sigmoid(a)) * b
    o_ref[...] = _rms(h + _dot(mid, w2_ref[...]), gf_ref[...])


def _ffn(h2, g, w1, w3, w2, gf):
    T = h2.shape[0]
    tm = _pick(T, (512, 256))
    row = pl.BlockSpec((tm, D_MODEL), lambda i: (i, 0))
    vec = pl.BlockSpec((1, D_MODEL), lambda i: (0, 0))
    resident = lambda shape: pl.BlockSpec(shape, lambda i: (0, 0), pipeline_mode=pl.Buffered(1))
    return pl.pallas_call(
        _ffn_kernel,
        grid=(T // tm,),
        in_specs=[row, vec, resident((D_MODEL, D_FF)), resident((D_MODEL, D_FF)),
                  resident((D_FF, D_MODEL)), vec],
        out_specs=row,
        out_shape=jax.ShapeDtypeStruct((T, D_MODEL), F32),
        compiler_params=_cparams(("parallel",)),
        name="ffn",
    )(h2, g, w1, w3, w2, gf)


def _pack_in_weights(w_in):
    rw_cols = 3 * RW_WIDTH + RW_LO
    w_t = jnp.swapaxes(w_in, 0, 1).astype(BF16)
    pad = jnp.zeros((RW_LO_PAD - RW_LO, D_MODEL), BF16)
    return jnp.concatenate([w_t[:rw_cols], pad, w_t[rw_cols:]], axis=0)


def _pack_mu_lo(mu):
    lo = mu[3 * RW_WIDTH:]
    return jnp.concatenate([lo, jnp.zeros((RW_LO_PAD - RW_LO,), mu.dtype)])[None, :].astype(F32)


def kernel(x, mem, norm_mix_g, w_in, rw_mu, rw_w0, rw_w2, rw_a0, rw_a2, rw_g2, rw_k_k, rw_k_a, rw_r_k, rw_ln_w, rw_ln_b, hg_lb_logits, hg_norm_g, w_out, norm_xa_g, norm_mem_g, xa_wq, xa_wk, xa_wv, xa_wo, norm_ffn_g, ffn_w1, ffn_w3, ffn_w2, norm_final_g):
    B, S, _ = x.shape
    n_mem = mem.shape[1]
    depth = w_in.shape[0]
    assert depth == 1, "single-layer block"
    assert S % (RW_GROUP * RW_CHUNK) == 0 and S % (HG_GROUP * HG_CHUNK) == 0
    l = 0
    T = B * S
    row = lambda a: a.reshape(1, -1).astype(F32)

    x2 = x.reshape(T, D_MODEL)
    cols = _inproj(x2, row(norm_mix_g[l]), _pack_in_weights(w_in[l]))
    cols3 = cols.reshape(B, S, N_COLS)

    mu = rw_mu[l].astype(F32)
    lo_feat = _lofeat(cols3, _pack_mu_lo(mu))
    rw = dict(
        mu_rkv=mu[:3 * RW_WIDTH].reshape(3, RW_WIDTH),
        w0=row(rw_w0[l]), w2=rw_w2[l].astype(BF16), a0=row(rw_a0[l]), a2=rw_a2[l].astype(BF16),
        g2=rw_g2[l].astype(BF16), k_k=row(rw_k_k[l]), k_a=row(rw_k_a[l]), r_k=row(rw_r_k[l]),
        ln_w=row(rw_ln_w[l]), ln_b=row(rw_ln_b[l]))
    ya = _rwkv(cols3, lo_feat, rw).reshape(T, D_MODEL)
    yb = _hgrn(cols3, hg_lb_logits.astype(F32), row(hg_norm_g[l])).reshape(T, D_MODEL)

    km, vm = _memkv(mem.reshape(B * n_mem, D_MODEL), row(norm_mem_g[l]),
                    xa_wk[l].astype(BF16), xa_wv[l].astype(BF16), B, n_mem)
    h2 = _xattn(x2, ya, yb, w_out[l].astype(BF16), row(norm_xa_g[l]), xa_wq[l].astype(BF16), km, vm,
                xa_wo[l].astype(BF16), B, S, n_mem)

    out = _ffn(h2, row(norm_ffn_g[l]), ffn_w1[l].astype(BF16), ffn_w3[l].astype(BF16),
               ffn_w2[l].astype(BF16), row(norm_final_g))
    return out.reshape(B, S, D_MODEL)
```

```python
import functools

import numpy as np
import jax
import jax.numpy as jnp
from jax import lax
from jax.experimental import pallas as pl
from jax.experimental.pallas import tpu as pltpu

F32 = jnp.float32
BF16 = jnp.bfloat16

D_MODEL = 1024
NORM_EPS = 1e-6
LANES = 128

RW_HEAD = 64
RW_WIDTH = D_MODEL
RW_DECAY_RANK = 64
RW_ICLR_RANK = 64
RW_GATE_RANK = 160
RW_LN_EPS = 64e-5
RW_LO = RW_DECAY_RANK + RW_ICLR_RANK + RW_GATE_RANK
RW_LO_PAD = 512
RW_LO_FEAT = 384
RW_CHUNK = 64
RW_SUB = 16
RW_GROUP = 4
LOG2E = 1.4426950408889634
EXP_M_HALF = 0.6065306597126334

HG_KEY = 128
HG_HEADS = D_MODEL // HG_KEY
HG_CHUNK = 64
HG_LEVELS = 5
HG_GROUP = 4

XA_HEADS = 4
XA_HEAD = D_MODEL // XA_HEADS
D_FF = 2816

COL_R = 0
COL_K = COL_R + RW_WIDTH
COL_V = COL_K + RW_WIDTH
COL_LO = COL_V + RW_WIDTH
COL_HQ = COL_LO + RW_LO_PAD
COL_HF = COL_HQ + D_MODEL
COL_HI = COL_HF + D_MODEL
COL_HG = COL_HI + D_MODEL
COL_GA = COL_HG + D_MODEL
COL_GB = COL_GA + D_MODEL
N_COLS = COL_GB + D_MODEL

COLS_DTYPE = BF16
IN_COL_CHUNKS = 4

VMEM_LIMIT = 56 * 1024 * 1024


def _cparams(sem):
    return pltpu.CompilerParams(dimension_semantics=sem, vmem_limit_bytes=VMEM_LIMIT)


def _sigmoid(x):
    return 1.0 / (1.0 + jnp.exp(-x))


def _rms(x, g):
    ms = jnp.mean(x * x, axis=-1, keepdims=True)
    return x * lax.rsqrt(ms + NORM_EPS) * g


def _dot(a, b):
    return jnp.dot(a.astype(BF16), b.astype(BF16), preferred_element_type=F32)


def _dot_nt(a, b):
    return lax.dot_general(a.astype(BF16), b.astype(BF16), (((1,), (1,)), ((), ())),
                           preferred_element_type=F32)


def _dot_tn(a, b):
    return lax.dot_general(a.astype(BF16), b.astype(BF16), (((0,), (0,)), ((), ())),
                           preferred_element_type=F32)


def _split3(x):
    hi = x.astype(BF16)
    r1 = x - hi.astype(F32)
    mid = r1.astype(BF16)
    lo = (r1 - mid.astype(F32)).astype(BF16)
    return jnp.concatenate([hi, mid, lo], axis=1)


def _split2(x):
    hi = x.astype(BF16)
    mid = (x - hi.astype(F32)).astype(BF16)
    return jnp.concatenate([hi, mid], axis=1)


def _sums2(pattern_bf16, x):
    y = jnp.dot(pattern_bf16, _split2(x), preferred_element_type=F32)
    return y[:, 0:LANES] + y[:, LANES:2 * LANES]


def _exact_sums(pattern_bf16, x):
    y = jnp.dot(pattern_bf16, _split3(x), preferred_element_type=F32)
    return y[:, 0:LANES] + y[:, LANES:2 * LANES] + y[:, 2 * LANES:3 * LANES]


def _pick(n, prefs):
    for p in prefs:
        if n % p == 0:
            return p
    return n


def _inproj_kernel(x_ref, g_ref, w_ref, o_ref):
    u = _rms(x_ref[...], g_ref[...]).astype(BF16)
    for j in range(IN_COL_CHUNKS):
        cs = slice(j * (N_COLS // IN_COL_CHUNKS), (j + 1) * (N_COLS // IN_COL_CHUNKS))
        o_ref[:, cs] = lax.dot_general(u, w_ref[cs, :], (((1,), (1,)), ((), ())),
                                       preferred_element_type=F32).astype(o_ref.dtype)


def _inproj(x2, g, w_packed_t):
    T = x2.shape[0]
    tm = _pick(T, (512, 256))
    return pl.pallas_call(
        _inproj_kernel,
        grid=(T // tm,),
        in_specs=[
            pl.BlockSpec((tm, D_MODEL), lambda i: (i, 0)),
            pl.BlockSpec((1, D_MODEL), lambda i: (0, 0)),
            pl.BlockSpec((N_COLS, D_MODEL), lambda i: (0, 0), pipeline_mode=pl.Buffered(1)),
        ],
        out_specs=pl.BlockSpec((tm, N_COLS), lambda i: (i, 0)),
        out_shape=jax.ShapeDtypeStruct((T, N_COLS), COLS_DTYPE),
        compiler_params=_cparams(("parallel",)),
        name="inproj",
    )(x2, g, w_packed_t)


def _shift_mix(raw, carry_row, mu):
    rolled = pltpu.roll(raw, 1, axis=0)
    row = lax.broadcasted_iota(jnp.int32, raw.shape, 0)
    prev = jnp.where(row == 0, carry_row, rolled)
    return raw + mu * (prev - raw)


def _lofeat_kernel(lo_ref, mu_ref, o_ref, carry_ref):
    ts = lo_ref.shape[1]

    @pl.when(pl.program_id(1) == 0)
    def _():
        carry_ref[...] = jnp.zeros_like(carry_ref)

    raw = lo_ref[0].astype(F32)
    lo = _shift_mix(raw, carry_ref[0:1, :], mu_ref[...])
    carry_ref[0:1, :] = raw[ts - 1:ts, :]
    lane = lax.broadcasted_iota(jnp.int32, (ts, RW_LO_FEAT), 1)
    x = lo[:, 0:RW_LO_FEAT]
    feat = jnp.where(lane < RW_DECAY_RANK, jnp.tanh(x),
                     jnp.where(lane < LANES, x, _sigmoid(x)))
    o_ref[0] = feat.astype(o_ref.dtype)


def _lofeat(cols3, mu_lo):
    B, S, _ = cols3.shape
    ts = _pick(S, (1024, 512, 256, 128, 64))
    return pl.pallas_call(
        _lofeat_kernel,
        grid=(B, S // ts),
        in_specs=[
            pl.BlockSpec((1, ts, RW_LO_PAD), lambda b, t: (b, t, COL_LO // RW_LO_PAD)),
            pl.BlockSpec((1, RW_LO_PAD), lambda b, t: (0, 0)),
        ],
        out_specs=pl.BlockSpec((1, ts, RW_LO_FEAT), lambda b, t: (b, t, 0)),
        out_shape=jax.ShapeDtypeStruct((B, S, RW_LO_FEAT), BF16),
        scratch_shapes=[pltpu.VMEM((8, RW_LO_PAD), F32)],
        compiler_params=_cparams(("parallel", "arbitrary")),
        name="lofeat",
    )(cols3, mu_lo)


def _pair_sum(x):
    lane = lax.broadcasted_iota(jnp.int32, x.shape, 1)
    first = lane < RW_HEAD
    s0 = jnp.sum(jnp.where(first, x, 0.0), axis=-1, keepdims=True)
    s1 = jnp.sum(jnp.where(first, 0.0, x), axis=-1, keepdims=True)
    return jnp.where(first, s0, s1)


def _rwkv_kernel(r_ref, k_ref, v_ref, lo_ref, ga_ref, mu_ref,
                 w0_ref, w2_ref, a0_ref, a2_ref, g2_ref,
                 kk_ref, ka_ref, rk_ref, lnw_ref, lnb_ref, tri_ref,
                 o_ref, state_ref, carry_ref):
    nb, ts, _ = r_ref.shape
    c = RW_CHUNK
    n_chunks = ts // c

    @pl.when(pl.program_id(1) == 0)
    def _():
        state_ref[...] = jnp.zeros_like(state_ref)
        carry_ref[...] = jnp.zeros_like(carry_ref)

    t_i = lax.broadcasted_iota(jnp.int32, (c, LANES), 0)
    s_i = lax.broadcasted_iota(jnp.int32, (c, LANES), 1) & (RW_HEAD - 1)
    strict = t_i > s_i
    incl = t_i >= s_i
    eye = (t_i == s_i).astype(F32)
    same_sub = (t_i // RW_SUB) == (s_i // RW_SUB)
    rr = lax.broadcasted_iota(jnp.int32, (LANES, LANES), 0)
    cc = lax.broadcasted_iota(jnp.int32, (LANES, LANES), 1)
    bd_mask = (rr // RW_HEAD) == (cc // RW_HEAD)

    def bd(x):
        return jnp.where(bd_mask, jnp.concatenate([x, x], axis=0), 0.0).astype(BF16)

    def pmul(a, bmat):
        return _dot(a, bd(bmat))

    tri = tri_ref[...]
    pre, pc, d, ys = {}, {}, {}, {}

    gl = RW_GROUP * c
    n_groups = ts // gl
    last_raw = {}

    def prologue(b, g):
        rs = slice(g * gl, (g + 1) * gl)
        rkv = []
        for n, ref in enumerate((r_ref, k_ref, v_ref)):
            raw = ref[b, rs, :].astype(F32)
            carry = carry_ref[b, n:n + 1, :] if g == 0 else last_raw[b, n]
            rkv.append(_shift_mix(raw, carry, mu_ref[n:n + 1, :]))
            last_raw[b, n] = raw[gl - 1:gl, :]
            if g == n_groups - 1:
                carry_ref[b, n:n + 1, :] = last_raw[b, n]
        r, k, v = rkv
        lo = lo_ref[b, rs, :]
        wl = w0_ref[...] + _dot(lo[:, 0:RW_DECAY_RANK], w2_ref[...])
        lw = (-LOG2E * EXP_M_HALF) * _sigmoid(wl)
        a = _sigmoid(a0_ref[...] + _dot(lo[:, RW_DECAY_RANK:LANES], a2_ref[...]))
        gate = _dot(lo[:, LANES:LANES + RW_GATE_RANK], g2_ref[...])
        kk = k * kk_ref[...]
        kkn = kk * lax.rsqrt(jnp.maximum(_pair_sum(kk * kk), 1e-24))
        kmod = k * (1.0 + (a - 1.0) * ka_ref[...])
        pre[b, g] = dict(r=r, v=v, kmod=kmod, g=gate, lw=lw, kkn=kkn, beta=kkn * a)

    def rows(name, it):
        b, ch = it
        lc = ch % RW_GROUP
        return pre[b, ch // RW_GROUP][name][lc * c:(lc + 1) * c]

    def prep(it):
        lw_c = rows("lw", it)
        cum = _exact_sums(tri, lw_c)
        c_end = cum[c - 1:c, :]
        e_neg = jnp.exp2(-cum)
        e_end = jnp.exp2(c_end - cum)
        beta_c, kmod_c = rows("beta", it), rows("kmod", it)
        pc[it] = dict(
            rt=rows("r", it) * jnp.exp2(cum),
            at=-rows("kkn", it) * jnp.exp2(cum - lw_c),
            bt=beta_c * e_neg, kt=kmod_c * e_neg,
            bh=(beta_c * e_end).astype(BF16), kh=(kmod_c * e_end).astype(BF16),
            w_end=jnp.exp2(c_end))

    def s_amat(its):
        for it in its:
            d["lhs2", it] = jnp.concatenate([pc[it]["at"], pc[it]["rt"]], axis=0).astype(BF16)
            d["amat", it] = _dot_nt(d["lhs2", it], jnp.concatenate(
                [bd(pc[it]["bt"]), bd(pc[it]["kt"])], axis=0))

    def s_split(its):
        for it in its:
            am = d["amat", it]
            n_ab = jnp.where(strict, am[0:c, 0:LANES], 0.0)
            d["a_r2", it] = jnp.concatenate(
                [jnp.where(incl, am[c:2 * c, 0:LANES], 0.0),
                 jnp.where(incl, am[c:2 * c, LANES:2 * LANES], 0.0)], axis=1).astype(BF16)
            d["bdv", it] = bd(rows("v", it))
            d["u0", it] = _dot(jnp.where(strict, am[0:c, LANES:2 * LANES], 0.0), d["bdv", it])
            d["dg", it] = jnp.where(same_sub, n_ab, 0.0)
            d["off", it] = (n_ab - d["dg", it]).astype(BF16)

    def s_d2(its):
        for it in its:
            d["d2", it] = pmul(d["dg", it], d["dg", it])
            d["x", it] = eye + d["dg", it]

    def s_x(pw, nxt):
        def stage(its):
            for it in its:
                if nxt:
                    both = pmul(jnp.concatenate([d["x", it], d[pw, it]], axis=0), d[pw, it])
                    d["x", it] = d["x", it] + both[0:c]
                    d[nxt, it] = both[c:2 * c]
                else:
                    d["x", it] = d["x", it] + pmul(d["x", it], d[pw, it])
        return stage

    def s_r(its):
        for it in its:
            d["r", it] = pmul(d["off", it], d["x", it])

    def s_r2(its):
        for it in its:
            both = pmul(jnp.concatenate([d["x", it], d["r", it]], axis=0), d["r", it])
            d["z", it] = d["x", it] + both[0:c]
            d["r2", it] = both[c:2 * c]

    def s_tinv(its):
        for it in its:
            d["t_inv", it] = (d["z", it] + pmul(d["z", it], d["r2", it])).astype(BF16)

    def s_au(its):
        for it in its:
            au = _dot(d["t_inv", it], jnp.concatenate([bd(pc[it]["at"]), bd(d["u0", it])], axis=1))
            d["ahat", it], d["uu", it] = au[:, 0:LANES], au[:, LANES:2 * LANES]

    def s_trans(its):
        for it in its:
            lhs = jnp.concatenate(
                [jnp.concatenate([d["ahat", it], d["uu", it]], axis=1),
                 jnp.concatenate([jnp.zeros((c, LANES), BF16), rows("v", it).astype(BF16)], axis=1)],
                axis=0)
            mg = _dot_tn(lhs, jnp.concatenate([pc[it]["bh"], pc[it]["kh"]], axis=0))
            d["mmat", it] = jnp.where(bd_mask, mg[0:LANES], 0.0).astype(BF16)
            d["gmat", it] = jnp.where(bd_mask, mg[LANES:2 * LANES], 0.0)
            d["rhat", it] = (pc[it]["rt"] + pmul(d["a_r2", it][:, 0:LANES], d["ahat", it])).astype(BF16)
            d["y0", it] = _dot(d["a_r2", it], jnp.concatenate([bd(d["uu", it]), d["bdv", it]], axis=0))

    stages = [s_amat, s_split, s_d2, s_x("d2", "d4"), s_x("d4", "d8"), s_x("d8", None),
              s_r, s_r2, s_tinv, s_au, s_trans]

    def chain_step(bs, ch, st):
        st_bf = {b: st[b].astype(BF16) for b in bs}
        for b in bs:
            ys[b, ch] = d["y0", (b, ch)] + _dot_nt(d["rhat", (b, ch)], st_bf[b])
        for b in bs:
            st[b] = (st[b] * pc[b, ch]["w_end"] + _dot(st_bf[b], d["mmat", (b, ch)])
                     + d["gmat", (b, ch)])

    def epilogue(b, g):
        rs = slice(g * gl, (g + 1) * gl)
        y = jnp.concatenate([ys[b, ch] for ch in range(g * RW_GROUP, (g + 1) * RW_GROUP)], axis=0)
        pb = pre[b, g]
        mean = _pair_sum(y) * (1.0 / RW_HEAD)
        dlt = y - mean
        var = _pair_sum(dlt * dlt) * (1.0 / RW_HEAD)
        on = dlt * lax.rsqrt(var + RW_LN_EPS) * lnw_ref[...] + lnb_ref[...]
        bonus = _pair_sum(pb["r"] * pb["kmod"] * rk_ref[...]) * pb["v"]
        o_ref[b, rs, :] = (((on + bonus) * pb["g"])
                           * _sigmoid(ga_ref[b, rs, :].astype(F32))).astype(o_ref.dtype)

    batches = list(range(nb))
    group_items = [[(b, ch) for ch in range(g * RW_GROUP, (g + 1) * RW_GROUP) for b in batches]
                   for g in range(n_groups)]
    st = {}

    def prep_units(g):
        return ([functools.partial(prologue, b, g) for b in batches]
                + [functools.partial(prep, it) for it in group_items[g]])

    def tail_units(g):
        units = [functools.partial(chain_step, batches, ch, st)
                 for ch in range(g * RW_GROUP, (g + 1) * RW_GROUP)]
        return units + [functools.partial(epilogue, b, g) for b in batches]

    def interleave(stage_items, units):
        per = -(-len(units) // len(stages)) if units else 0
        for n, stage in enumerate(stages):
            stage(stage_items)
            for u in units[n * per:(n + 1) * per]:
                u()

    for b in batches:
        st[b] = state_ref[b]
    for u in prep_units(0):
        u()
    for g in range(n_groups):
        units = (prep_units(g + 1) if g + 1 < n_groups else []) + (tail_units(g - 1) if g > 0 else [])
        interleave(group_items[g], units)
    for u in tail_units(n_groups - 1):
        u()
    for b in batches:
        state_ref[b] = st[b]


def _rwkv(cols3, lo_feat, p):
    B, S, _ = cols3.shape
    ts = _pick(S, (2 * RW_GROUP * RW_CHUNK, RW_GROUP * RW_CHUNK))
    nt = S // ts
    n_hp = RW_WIDTH // LANES
    cb = lambda col: col // LANES

    def colspec(col0):
        return pl.BlockSpec((B, ts, LANES), lambda h, t: (0, t, cb(col0) + h))

    vec = pl.BlockSpec((1, LANES), lambda h, t: (0, h))
    mat = lambda nrows: pl.BlockSpec((nrows, LANES), lambda h, t: (0, h))
    full = lambda shape: pl.BlockSpec(shape, lambda h, t: (0, 0))
    tri = jnp.asarray(np.tril(np.ones((RW_CHUNK, RW_CHUNK), np.float32)), dtype=BF16)
    return pl.pallas_call(
        _rwkv_kernel,
        grid=(n_hp, nt),
        in_specs=[
            colspec(COL_R), colspec(COL_K), colspec(COL_V),
            pl.BlockSpec((B, ts, RW_LO_FEAT), lambda h, t: (0, t, 0)),
            colspec(COL_GA),
            mat(3),
            vec, mat(RW_DECAY_RANK), vec, mat(RW_ICLR_RANK), mat(RW_GATE_RANK),
            vec, vec, vec, vec, vec,
            full(tri.shape),
        ],
        out_specs=pl.BlockSpec((B, ts, LANES), lambda h, t: (0, t, h)),
        out_shape=jax.ShapeDtypeStruct((B, S, RW_WIDTH), BF16),
        scratch_shapes=[pltpu.VMEM((B, LANES, LANES), F32), pltpu.VMEM((B, 8, LANES), F32)],
        compiler_params=_cparams(("parallel", "arbitrary")),
        name="rwkv7",
    )(cols3, cols3, cols3, lo_feat, cols3, p["mu_rkv"],
      p["w0"], p["w2"], p["a0"], p["a2"], p["g2"],
      p["k_k"], p["k_a"], p["r_k"], p["ln_w"], p["ln_b"], tri)


def _hg_level_matrix():
    c = HG_CHUNK
    m = np.zeros((HG_LEVELS + 1, c, c), np.float32)
    for l in range(HG_LEVELS):
        bs = c >> l
        half = bs // 2
        for t in range(c):
            mid = (t // bs) * bs + half
            if t % bs >= half:
                m[l, t, mid:t + 1] = 1.0
            else:
                m[l, t, t + 1:mid] = 1.0
    m[HG_LEVELS] = np.tril(np.ones((c, c), np.float32))
    return m.reshape((HG_LEVELS + 1) * c, c)


def _hgrn_kernel(q_ref, f_ref, i_ref, g_ref, gb_ref, lbl_ref, ng_ref, lvl_ref,
                 o_ref, state_ref):
    nb, ts, _ = q_ref.shape
    c = HG_CHUNK
    gl = HG_GROUP * c
    n_groups = ts // gl

    @pl.when(pl.program_id(1) == 0)
    def _():
        state_ref[...] = jnp.zeros_like(state_ref)

    logits = lbl_ref[...]
    mx = jnp.max(logits, axis=0, keepdims=True)
    ex = jnp.exp(logits - mx)
    lb = ex[0:1, :] / jnp.sum(ex, axis=0, keepdims=True)

    ri = lax.broadcasted_iota(jnp.int32, (c, c), 0)
    ci = lax.broadcasted_iota(jnp.int32, (c, c), 1)
    rowi = lax.broadcasted_iota(jnp.int32, (c, LANES), 0)
    lvl = lvl_ref[...]
    ng = ng_ref[...]
    second, valid = [], []
    for l in range(HG_LEVELS):
        bs = c >> l
        second.append((rowi & (bs - 1)) >= (bs // 2))
        valid.append(((ri // bs) == (ci // bs)) & ((ri & (bs - 1)) >= (bs // 2))
                     & ((ci & (bs - 1)) < (bs // 2)))

    pair = (ri == ci + 1) & ((ri & 1) == 1)
    pre, d, outs = {}, {}, {}

    def prologue(b, g):
        rs = slice(g * gl, (g + 1) * gl)
        f = lb + (1.0 - lb) * _sigmoid(f_ref[b, rs, :].astype(F32))
        qraw = q_ref[b, rs, :].astype(F32)
        pre[b, g] = dict(f=f, lf=LOG2E * jnp.log(f), kx=1.0 - f, qs=qraw * _sigmoid(qraw),
                         iv=i_ref[b, rs, :])

    def rows(name, it):
        b, ch = it
        lc = ch % HG_GROUP
        return pre[b, ch // HG_GROUP][name][lc * c:(lc + 1) * c]

    def s_part(its):
        for it in its:
            d["part", it] = _sums2(lvl, rows("lf", it))
            q_c, k_c = rows("qs", it), rows("kx", it)
            dsum = jnp.sum(q_c * k_c, axis=-1, keepdims=True)
            psum = jnp.sum(q_c * rows("f", it) * pltpu.roll(k_c, 1, axis=0), axis=-1, keepdims=True)
            d["sc", it] = jnp.where(ri == ci, dsum, jnp.where(pair, psum, 0.0))

    def s_level(l):
        def stage(its):
            for it in its:
                qk = (jnp.where(second[l], rows("qs", it), rows("kx", it))
                      * jnp.exp2(d["part", it][l * c:(l + 1) * c])).astype(BF16)
                d["sc", it] = jnp.where(valid[l], _dot_nt(qk, qk), d["sc", it])
        return stage

    def s_out(its):
        for it in its:
            bcum = d["part", it][HG_LEVELS * c:(HG_LEVELS + 1) * c]
            b_end = bcum[c - 1:c, :]
            d["o_intra", it] = _dot(d["sc", it], rows("iv", it))
            d["zc", it] = _dot_tn(rows("iv", it), rows("kx", it) * jnp.exp2(b_end - bcum))
            d["qb", it] = rows("qs", it) * jnp.exp2(bcum)
            d["dec", it] = jnp.exp2(b_end)

    stages = [s_part] + [s_level(l) for l in range(HG_LEVELS)] + [s_out]

    def tail(b, g, st):
        rs = slice(g * gl, (g + 1) * gl)
        os_ = []
        for ch in range(g * HG_GROUP, (g + 1) * HG_GROUP):
            it = (b, ch)
            os_.append(d["o_intra", it] + _dot_nt(d["qb", it], st[b]))
            st[b] = st[b] * d["dec", it] + d["zc", it]
        o = jnp.concatenate(os_, axis=0)
        o = o * lax.rsqrt(jnp.mean(o * o, axis=-1, keepdims=True) + NORM_EPS) * ng
        graw = g_ref[b, rs, :].astype(F32)
        o_ref[b, rs, :] = (o * (graw * _sigmoid(graw))
                           * _sigmoid(gb_ref[b, rs, :].astype(F32))).astype(o_ref.dtype)

    batches = list(range(nb))
    group_items = [[(b, ch) for ch in range(g * HG_GROUP, (g + 1) * HG_GROUP) for b in batches]
                   for g in range(n_groups)]
    st = {b: state_ref[b] for b in batches}

    def interleave(stage_items, units):
        per = -(-len(units) // len(stages)) if units else 0
        for n, stage in enumerate(stages):
            stage(stage_items)
            for u in units[n * per:(n + 1) * per]:
                u()

    for b in batches:
        prologue(b, 0)
    for g in range(n_groups):
        units = ([functools.partial(prologue, b, g + 1) for b in batches] if g + 1 < n_groups else [])
        units += ([functools.partial(tail, b, g - 1, st) for b in batches] if g > 0 else [])
        interleave(group_items[g], units)
    for b in batches:
        tail(b, n_groups - 1, st)
        state_ref[b] = st[b]


def _hgrn(cols3, lb_logits, norm_g):
    B, S, _ = cols3.shape
    ts = _pick(S, (2 * HG_GROUP * HG_CHUNK, HG_GROUP * HG_CHUNK))
    nt = S // ts
    cb = lambda col: col // LANES

    def colspec(col0):
        return pl.BlockSpec((B, ts, LANES), lambda h, t: (0, t, cb(col0) + h))

    n_slots = lb_logits.shape[0]
    lvl = jnp.asarray(_hg_level_matrix(), dtype=BF16)
    return pl.pallas_call(
        _hgrn_kernel,
        grid=(HG_HEADS, nt),
        in_specs=[
            colspec(COL_HQ), colspec(COL_HF), colspec(COL_HI), colspec(COL_HG), colspec(COL_GB),
            pl.BlockSpec((n_slots, LANES), lambda h, t: (0, h)),
            pl.BlockSpec((1, LANES), lambda h, t: (0, 0)),
            pl.BlockSpec(lvl.shape, lambda h, t: (0, 0)),
        ],
        out_specs=pl.BlockSpec((B, ts, LANES), lambda h, t: (0, t, h)),
        out_shape=jax.ShapeDtypeStruct((B, S, D_MODEL), BF16),
        scratch_shapes=[pltpu.VMEM((B, HG_KEY, HG_KEY), F32)],
        compiler_params=_cparams(("parallel", "arbitrary")),
        name="hgrn2",
    )(cols3, cols3, cols3, cols3, cols3, lb_logits, norm_g, lvl)


def _memkv_kernel(m_ref, g_ref, wk_ref, wv_ref, k_ref, v_ref):
    m = _rms(m_ref[...], g_ref[...]).astype(BF16)
    k_ref[...] = jnp.dot(m, wk_ref[...], preferred_element_type=F32).astype(BF16)
    v_ref[...] = jnp.dot(m, wv_ref[...], preferred_element_type=F32).astype(BF16)


def _memkv(mem2, g, wk, wv, B, n_mem):
    row = pl.BlockSpec((n_mem, D_MODEL), lambda b: (b, 0))
    wspec = pl.BlockSpec((D_MODEL, D_MODEL), lambda b: (0, 0))
    sds = jax.ShapeDtypeStruct((B * n_mem, D_MODEL), BF16)
    return pl.pallas_call(
        _memkv_kernel,
        grid=(B,),
        in_specs=[row, pl.BlockSpec((1, D_MODEL), lambda b: (0, 0)), wspec, wspec],
        out_specs=[row, row],
        out_shape=[sds, sds],
        compiler_params=_cparams(("parallel",)),
        name="memkv",
    )(mem2, g, wk, wv)


def _tail_kernel(x_ref, ya_ref, yb_ref, wout_ref, gx_ref, wq_ref, k_ref, v_ref, wo_ref,
                 gf_ref, w1_ref, w3_ref, w2_ref, gfin_ref, o_ref):
    h = x_ref[...] + _dot(ya_ref[...].astype(F32) + yb_ref[...].astype(F32), wout_ref[...])
    u = _rms(h, gx_ref[...]).astype(BF16)
    q = jnp.dot(u, wq_ref[...], preferred_element_type=F32)
    k = k_ref[...]
    v = v_ref[...]
    heads = []
    for hd in range(XA_HEADS):
        sl = slice(hd * XA_HEAD, (hd + 1) * XA_HEAD)
        s = _dot_nt(q[:, sl], k[:, sl]) * (XA_HEAD ** -0.5)
        s = s - jnp.max(s, axis=-1, keepdims=True)
        e = jnp.exp(s)
        p = e / jnp.sum(e, axis=-1, keepdims=True)
        heads.append(_dot(p, v[:, sl]))
    h = h + _dot(jnp.concatenate(heads, axis=-1), wo_ref[...])
    u = _rms(h, gf_ref[...]).astype(BF16)
    a = jnp.dot(u, w1_ref[...], preferred_element_type=F32)
    b = jnp.dot(u, w3_ref[...], preferred_element_type=F32)
    mid = (a * _sigmoid(a)) * b
    o_ref[...] = _rms(h + _dot(mid, w2_ref[...]), gfin_ref[...])


def _tail(x2, ya, yb, wout, gx, wq, km, vm, wo, gf, w1, w3, w2, gfin, B, S, n_mem):
    tm = _pick(S, (512, 256))
    nt = S // tm
    row = pl.BlockSpec((tm, D_MODEL), lambda b, t: (b * nt + t, 0))
    vec = pl.BlockSpec((1, D_MODEL), lambda b, t: (0, 0))
    kv = pl.BlockSpec((n_mem, D_MODEL), lambda b, t: (b, 0))
    resident = lambda shape: pl.BlockSpec(shape, lambda b, t: (0, 0), pipeline_mode=pl.Buffered(1))
    sq = resident((D_MODEL, D_MODEL))
    return pl.pallas_call(
        _tail_kernel,
        grid=(B, nt),
        in_specs=[row, row, row, sq, vec, sq, kv, kv, sq,
                  vec, resident((D_MODEL, D_FF)), resident((D_MODEL, D_FF)),
                  resident((D_FF, D_MODEL)), vec],
        out_specs=row,
        out_shape=jax.ShapeDtypeStruct((B * S, D_MODEL), F32),
        compiler_params=_cparams(("parallel", "parallel")),
        name="tail",
    )(x2, ya, yb, wout, gx, wq, km, vm, wo, gf, w1, w3, w2, gfin)


def _pack_in_weights(w_in):
    rw_cols = 3 * RW_WIDTH + RW_LO
    w_t = jnp.swapaxes(w_in, 0, 1).astype(BF16)
    pad = jnp.zeros((RW_LO_PAD - RW_LO, D_MODEL), BF16)
    return jnp.concatenate([w_t[:rw_cols], pad, w_t[rw_cols:]], axis=0)


def _pack_mu_lo(mu):
    lo = mu[3 * RW_WIDTH:]
    return jnp.concatenate([lo, jnp.zeros((RW_LO_PAD - RW_LO,), mu.dtype)])[None, :].astype(F32)


def kernel(x, mem, norm_mix_g, w_in, rw_mu, rw_w0, rw_w2, rw_a0, rw_a2, rw_g2, rw_k_k, rw_k_a, rw_r_k, rw_ln_w, rw_ln_b, hg_lb_logits, hg_norm_g, w_out, norm_xa_g, norm_mem_g, xa_wq, xa_wk, xa_wv, xa_wo, norm_ffn_g, ffn_w1, ffn_w3, ffn_w2, norm_final_g):
    B, S, _ = x.shape
    n_mem = mem.shape[1]
    depth = w_in.shape[0]
    assert depth == 1, "single-layer block"
    assert S % (RW_GROUP * RW_CHUNK) == 0 and S % (HG_GROUP * HG_CHUNK) == 0
    l = 0
    T = B * S
    row = lambda a: a.reshape(1, -1).astype(F32)

    x2 = x.reshape(T, D_MODEL)
    cols = _inproj(x2, row(norm_mix_g[l]), _pack_in_weights(w_in[l]))
    cols3 = cols.reshape(B, S, N_COLS)

    mu = rw_mu[l].astype(F32)
    lo_feat = _lofeat(cols3, _pack_mu_lo(mu))
    rw = dict(
        mu_rkv=mu[:3 * RW_WIDTH].reshape(3, RW_WIDTH),
        w0=row(rw_w0[l]), w2=rw_w2[l].astype(BF16), a0=row(rw_a0[l]), a2=rw_a2[l].astype(BF16),
        g2=rw_g2[l].astype(BF16), k_k=row(rw_k_k[l]), k_a=row(rw_k_a[l]), r_k=row(rw_r_k[l]),
        ln_w=row(rw_ln_w[l]), ln_b=row(rw_ln_b[l]))
    ya = _rwkv(cols3, lo_feat, rw).reshape(T, D_MODEL)
    yb = _hgrn(cols3, hg_lb_logits.astype(F32), row(hg_norm_g[l])).reshape(T, D_MODEL)

    km, vm = _memkv(mem.reshape(B * n_mem, D_MODEL), row(norm_mem_g[l]),
                    xa_wk[l].astype(BF16), xa_wv[l].astype(BF16), B, n_mem)
    out = _tail(x2, ya, yb, w_out[l].astype(BF16), row(norm_xa_g[l]), xa_wq[l].astype(BF16), km, vm,
                xa_wo[l].astype(BF16), row(norm_ffn_g[l]), ffn_w1[l].astype(BF16),
                ffn_w3[l].astype(BF16), ffn_w2[l].astype(BF16), row(norm_final_g), B, S, n_mem)
    return out.reshape(B, S, D_MODEL)
```

```python
import functools

import numpy as np
import jax
import jax.numpy as jnp
from jax import lax
from jax.experimental import pallas as pl
from jax.experimental.pallas import tpu as pltpu

F32 = jnp.float32
BF16 = jnp.bfloat16

D_MODEL = 1024
NORM_EPS = 1e-6
LANES = 128

RW_HEAD = 64
RW_WIDTH = D_MODEL
RW_DECAY_RANK = 64
RW_ICLR_RANK = 64
RW_GATE_RANK = 160
RW_LN_EPS = 64e-5
RW_LO = RW_DECAY_RANK + RW_ICLR_RANK + RW_GATE_RANK
RW_LO_PAD = 512
RW_LO_FEAT = 384
RW_CHUNK = 64
RW_SUB = 16
RW_GROUP = 4
LOG2E = 1.4426950408889634
EXP_M_HALF = 0.6065306597126334

HG_KEY = 128
HG_HEADS = D_MODEL // HG_KEY
HG_CHUNK = 64
HG_LEVELS = 5
HG_GROUP = 4

XA_HEADS = 4
XA_HEAD = D_MODEL // XA_HEADS
D_FF = 2816

COL_R = 0
COL_K = COL_R + RW_WIDTH
COL_V = COL_K + RW_WIDTH
COL_LO = COL_V + RW_WIDTH
COL_HQ = COL_LO + RW_LO_PAD
COL_HF = COL_HQ + D_MODEL
COL_HI = COL_HF + D_MODEL
COL_HG = COL_HI + D_MODEL
COL_GA = COL_HG + D_MODEL
COL_GB = COL_GA + D_MODEL
N_COLS = COL_GB + D_MODEL

COLS_DTYPE = BF16
IN_COL_CHUNKS = 4

VMEM_LIMIT = 56 * 1024 * 1024


def _cparams(sem):
    return pltpu.CompilerParams(dimension_semantics=sem, vmem_limit_bytes=VMEM_LIMIT)


def _sigmoid(x):
    return 1.0 / (1.0 + jnp.exp(-x))


def _rms(x, g):
    ms = jnp.mean(x * x, axis=-1, keepdims=True)
    return x * lax.rsqrt(ms + NORM_EPS) * g


def _dot(a, b):
    return jnp.dot(a.astype(BF16), b.astype(BF16), preferred_element_type=F32)


def _dot_nt(a, b):
    return lax.dot_general(a.astype(BF16), b.astype(BF16), (((1,), (1,)), ((), ())),
                           preferred_element_type=F32)


def _dot_tn(a, b):
    return lax.dot_general(a.astype(BF16), b.astype(BF16), (((0,), (0,)), ((), ())),
                           preferred_element_type=F32)


def _split3(x):
    hi = x.astype(BF16)
    r1 = x - hi.astype(F32)
    mid = r1.astype(BF16)
    lo = (r1 - mid.astype(F32)).astype(BF16)
    return jnp.concatenate([hi, mid, lo], axis=1)


def _split2(x):
    hi = x.astype(BF16)
    mid = (x - hi.astype(F32)).astype(BF16)
    return jnp.concatenate([hi, mid], axis=1)


def _sums2(pattern_bf16, x):
    y = jnp.dot(pattern_bf16, _split2(x), preferred_element_type=F32)
    return y[:, 0:LANES] + y[:, LANES:2 * LANES]


def _exact_sums(pattern_bf16, x):
    y = jnp.dot(pattern_bf16, _split3(x), preferred_element_type=F32)
    return y[:, 0:LANES] + y[:, LANES:2 * LANES] + y[:, 2 * LANES:3 * LANES]


def _pick(n, prefs):
    for p in prefs:
        if n % p == 0:
            return p
    return n


def _inproj_kernel(x_ref, g_ref, w_ref, o_ref):
    u = _rms(x_ref[...], g_ref[...]).astype(BF16)
    for j in range(IN_COL_CHUNKS):
        cs = slice(j * (N_COLS // IN_COL_CHUNKS), (j + 1) * (N_COLS // IN_COL_CHUNKS))
        o_ref[:, cs] = lax.dot_general(u, w_ref[cs, :], (((1,), (1,)), ((), ())),
                                       preferred_element_type=F32).astype(o_ref.dtype)


def _inproj(x2, g, w_packed_t):
    T = x2.shape[0]
    tm = _pick(T, (512, 256))
    return pl.pallas_call(
        _inproj_kernel,
        grid=(T // tm,),
        in_specs=[
            pl.BlockSpec((tm, D_MODEL), lambda i: (i, 0)),
            pl.BlockSpec((1, D_MODEL), lambda i: (0, 0)),
            pl.BlockSpec((N_COLS, D_MODEL), lambda i: (0, 0), pipeline_mode=pl.Buffered(1)),
        ],
        out_specs=pl.BlockSpec((tm, N_COLS), lambda i: (i, 0)),
        out_shape=jax.ShapeDtypeStruct((T, N_COLS), COLS_DTYPE),
        compiler_params=_cparams(("parallel",)),
        name="inproj",
    )(x2, g, w_packed_t)


def _shift_mix(raw, carry_row, mu):
    rolled = pltpu.roll(raw, 1, axis=0)
    row = lax.broadcasted_iota(jnp.int32, raw.shape, 0)
    prev = jnp.where(row == 0, carry_row, rolled)
    return raw + mu * (prev - raw)


def _lofeat_kernel(lo_ref, mu_ref, o_ref, carry_ref):
    ts = lo_ref.shape[1]

    @pl.when(pl.program_id(1) == 0)
    def _():
        carry_ref[...] = jnp.zeros_like(carry_ref)

    raw = lo_ref[0].astype(F32)
    lo = _shift_mix(raw, carry_ref[0:1, :], mu_ref[...])
    carry_ref[0:1, :] = raw[ts - 1:ts, :]
    lane = lax.broadcasted_iota(jnp.int32, (ts, RW_LO_FEAT), 1)
    x = lo[:, 0:RW_LO_FEAT]
    feat = jnp.where(lane < RW_DECAY_RANK, jnp.tanh(x),
                     jnp.where(lane < LANES, x, _sigmoid(x)))
    o_ref[0] = feat.astype(o_ref.dtype)


def _lofeat(cols3, mu_lo):
    B, S, _ = cols3.shape
    ts = _pick(S, (1024, 512, 256, 128, 64))
    return pl.pallas_call(
        _lofeat_kernel,
        grid=(B, S // ts),
        in_specs=[
            pl.BlockSpec((1, ts, RW_LO_PAD), lambda b, t: (b, t, COL_LO // RW_LO_PAD)),
            pl.BlockSpec((1, RW_LO_PAD), lambda b, t: (0, 0)),
        ],
        out_specs=pl.BlockSpec((1, ts, RW_LO_FEAT), lambda b, t: (b, t, 0)),
        out_shape=jax.ShapeDtypeStruct((B, S, RW_LO_FEAT), BF16),
        scratch_shapes=[pltpu.VMEM((8, RW_LO_PAD), F32)],
        compiler_params=_cparams(("parallel", "arbitrary")),
        name="lofeat",
    )(cols3, mu_lo)


def _pair_sum(x):
    lane = lax.broadcasted_iota(jnp.int32, x.shape, 1)
    first = lane < RW_HEAD
    s0 = jnp.sum(jnp.where(first, x, 0.0), axis=-1, keepdims=True)
    s1 = jnp.sum(jnp.where(first, 0.0, x), axis=-1, keepdims=True)
    return jnp.where(first, s0, s1)


def _rwkv_kernel(r_ref, k_ref, v_ref, lo_ref, ga_ref, mu_ref,
                 w0_ref, w2_ref, a0_ref, a2_ref, g2_ref,
                 kk_ref, ka_ref, rk_ref, lnw_ref, lnb_ref, tri_ref,
                 o_ref, state_ref, carry_ref):
    nb, ts, _ = r_ref.shape
    c = RW_CHUNK
    n_chunks = ts // c

    @pl.when(pl.program_id(1) == 0)
    def _():
        state_ref[...] = jnp.zeros_like(state_ref)
        carry_ref[...] = jnp.zeros_like(carry_ref)

    t_i = lax.broadcasted_iota(jnp.int32, (c, LANES), 0)
    s_i = lax.broadcasted_iota(jnp.int32, (c, LANES), 1) & (RW_HEAD - 1)
    strict = t_i > s_i
    incl = t_i >= s_i
    eye = (t_i == s_i).astype(F32)
    same_sub = (t_i // RW_SUB) == (s_i // RW_SUB)
    rr = lax.broadcasted_iota(jnp.int32, (LANES, LANES), 0)
    cc = lax.broadcasted_iota(jnp.int32, (LANES, LANES), 1)
    bd_mask = (rr // RW_HEAD) == (cc // RW_HEAD)

    def bd(x):
        return jnp.where(bd_mask, jnp.concatenate([x, x], axis=0), 0.0).astype(BF16)

    def pmul(a, bmat):
        return _dot(a, bd(bmat))

    tri = tri_ref[...]
    pre, pc, d, ys = {}, {}, {}, {}

    gl = RW_GROUP * c
    n_groups = ts // gl
    last_raw = {}

    def prologue(b, g):
        rs = slice(g * gl, (g + 1) * gl)
        rkv = []
        for n, ref in enumerate((r_ref, k_ref, v_ref)):
            raw = ref[b, rs, :].astype(F32)
            carry = carry_ref[b, n:n + 1, :] if g == 0 else last_raw[b, n]
            rkv.append(_shift_mix(raw, carry, mu_ref[n:n + 1, :]))
            last_raw[b, n] = raw[gl - 1:gl, :]
            if g == n_groups - 1:
                carry_ref[b, n:n + 1, :] = last_raw[b, n]
        r, k, v = rkv
        lo = lo_ref[b, rs, :]
        wl = w0_ref[...] + _dot(lo[:, 0:RW_DECAY_RANK], w2_ref[...])
        lw = (-LOG2E * EXP_M_HALF) * _sigmoid(wl)
        a = _sigmoid(a0_ref[...] + _dot(lo[:, RW_DECAY_RANK:LANES], a2_ref[...]))
        gate = _dot(lo[:, LANES:LANES + RW_GATE_RANK], g2_ref[...])
        kk = k * kk_ref[...]
        kkn = kk * lax.rsqrt(jnp.maximum(_pair_sum(kk * kk), 1e-24))
        kmod = k * (1.0 + (a - 1.0) * ka_ref[...])
        pre[b, g] = dict(r=r, v=v, kmod=kmod, g=gate, lw=lw, kkn=kkn, beta=kkn * a)

    def rows(name, it):
        b, ch = it
        lc = ch % RW_GROUP
        return pre[b, ch // RW_GROUP][name][lc * c:(lc + 1) * c]

    def prep(it):
        lw_c = rows("lw", it)
        cum = _exact_sums(tri, lw_c)
        c_end = cum[c - 1:c, :]
        e_neg = jnp.exp2(-cum)
        e_end = jnp.exp2(c_end - cum)
        beta_c, kmod_c = rows("beta", it), rows("kmod", it)
        pc[it] = dict(
            rt=rows("r", it) * jnp.exp2(cum),
            at=-rows("kkn", it) * jnp.exp2(cum - lw_c),
            bt=beta_c * e_neg, kt=kmod_c * e_neg,
            bh=(beta_c * e_end).astype(BF16), kh=(kmod_c * e_end).astype(BF16),
            w_end=jnp.exp2(c_end))

    def s_amat(its):
        for it in its:
            d["lhs2", it] = jnp.concatenate([pc[it]["at"], pc[it]["rt"]], axis=0).astype(BF16)
            d["amat", it] = _dot_nt(d["lhs2", it], jnp.concatenate(
                [bd(pc[it]["bt"]), bd(pc[it]["kt"])], axis=0))

    def s_split(its):
        for it in its:
            am = d["amat", it]
            n_ab = jnp.where(strict, am[0:c, 0:LANES], 0.0)
            d["a_rb", it] = jnp.where(incl, am[c:2 * c, 0:LANES], 0.0).astype(BF16)
            avk = _dot(jnp.concatenate([jnp.where(strict, am[0:c, LANES:2 * LANES], 0.0),
                                        jnp.where(incl, am[c:2 * c, LANES:2 * LANES], 0.0)], axis=0),
                       bd(rows("v", it)))
            d["u0", it], d["yv", it] = avk[0:c], avk[c:2 * c]
            d["dg", it] = jnp.where(same_sub, n_ab, 0.0)
            d["off", it] = (n_ab - d["dg", it]).astype(BF16)

    def s_d2(its):
        for it in its:
            d["d2", it] = pmul(d["dg", it], d["dg", it])
            d["x", it] = eye + d["dg", it]

    def s_x(pw, nxt):
        def stage(its):
            for it in its:
                if nxt:
                    both = pmul(jnp.concatenate([d["x", it], d[pw, it]], axis=0), d[pw, it])
                    d["x", it] = d["x", it] + both[0:c]
                    d[nxt, it] = both[c:2 * c]
                else:
                    d["x", it] = d["x", it] + pmul(d["x", it], d[pw, it])
        return stage

    def s_r(its):
        for it in its:
            d["r", it] = pmul(d["off", it], d["x", it])

    def s_r2(its):
        for it in its:
            both = pmul(jnp.concatenate([d["x", it], d["r", it]], axis=0), d["r", it])
            d["z", it] = d["x", it] + both[0:c]
            d["r2", it] = both[c:2 * c]

    def s_tinv(its):
        for it in its:
            d["t_inv", it] = (d["z", it] + pmul(d["z", it], d["r2", it])).astype(BF16)

    def s_au(its):
        for it in its:
            au = _dot(d["t_inv", it], jnp.concatenate([bd(pc[it]["at"]), bd(d["u0", it])], axis=1))
            d["ahat", it], d["uu", it] = au[:, 0:LANES], au[:, LANES:2 * LANES]

    def s_trans(its):
        for it in its:
            lhs = jnp.concatenate(
                [jnp.concatenate([d["ahat", it], d["uu", it]], axis=1),
                 jnp.concatenate([jnp.zeros((c, LANES), BF16), rows("v", it).astype(BF16)], axis=1)],
                axis=0)
            mg = _dot_tn(lhs, jnp.concatenate([pc[it]["bh"], pc[it]["kh"]], axis=0))
            d["mmat", it] = jnp.where(bd_mask, mg[0:LANES], 0.0).astype(BF16)
            d["gmat", it] = jnp.where(bd_mask, mg[LANES:2 * LANES], 0.0)
            ru = _dot(d["a_rb", it], jnp.concatenate([bd(d["ahat", it]), bd(d["uu", it])], axis=1))
            d["rhat", it] = (pc[it]["rt"] + ru[:, 0:LANES]).astype(BF16)
            d["y0", it] = ru[:, LANES:2 * LANES] + d["yv", it]

    stages = [s_amat, s_split, s_d2, s_x("d2", "d4"), s_x("d4", "d8"), s_x("d8", None),
              s_r, s_r2, s_tinv, s_au, s_trans]

    def chain_step(bs, ch, st):
        st_bf = {b: st[b].astype(BF16) for b in bs}
        for b in bs:
            ys[b, ch] = d["y0", (b, ch)] + _dot_nt(d["rhat", (b, ch)], st_bf[b])
        for b in bs:
            st[b] = (st[b] * pc[b, ch]["w_end"] + _dot(st_bf[b], d["mmat", (b, ch)])
                     + d["gmat", (b, ch)])

    def epilogue(b, g):
        rs = slice(g * gl, (g + 1) * gl)
        y = jnp.concatenate([ys[b, ch] for ch in range(g * RW_GROUP, (g + 1) * RW_GROUP)], axis=0)
        pb = pre[b, g]
        mean = _pair_sum(y) * (1.0 / RW_HEAD)
        dlt = y - mean
        var = _pair_sum(dlt * dlt) * (1.0 / RW_HEAD)
        on = dlt * lax.rsqrt(var + RW_LN_EPS) * lnw_ref[...] + lnb_ref[...]
        bonus = _pair_sum(pb["r"] * pb["kmod"] * rk_ref[...]) * pb["v"]
        o_ref[b, rs, :] = (((on + bonus) * pb["g"])
                           * _sigmoid(ga_ref[b, rs, :].astype(F32))).astype(o_ref.dtype)

    batches = list(range(nb))
    group_items = [[(b, ch) for ch in range(g * RW_GROUP, (g + 1) * RW_GROUP) for b in batches]
                   for g in range(n_groups)]
    st = {}

    def prep_units(g):
        return ([functools.partial(prologue, b, g) for b in batches]
                + [functools.partial(prep, it) for it in group_items[g]])

    def tail_units(g):
        units = [functools.partial(chain_step, batches, ch, st)
                 for ch in range(g * RW_GROUP, (g + 1) * RW_GROUP)]
        return units + [functools.partial(epilogue, b, g) for b in batches]

    def interleave(stage_items, units):
        per = -(-len(units) // len(stages)) if units else 0
        for n, stage in enumerate(stages):
            stage(stage_items)
            for u in units[n * per:(n + 1) * per]:
                u()

    for b in batches:
        st[b] = state_ref[b]
    for u in prep_units(0):
        u()
    for g in range(n_groups):
        units = (prep_units(g + 1) if g + 1 < n_groups else []) + (tail_units(g - 1) if g > 0 else [])
        interleave(group_items[g], units)
    for u in tail_units(n_groups - 1):
        u()
    for b in batches:
        state_ref[b] = st[b]


def _rwkv(cols3, lo_feat, p):
    B, S, _ = cols3.shape
    ts = _pick(S, (2 * RW_GROUP * RW_CHUNK, RW_GROUP * RW_CHUNK))
    nt = S // ts
    n_hp = RW_WIDTH // LANES
    cb = lambda col: col // LANES

    def colspec(col0):
        return pl.BlockSpec((B, ts, LANES), lambda h, t: (0, t, cb(col0) + h))

    vec = pl.BlockSpec((1, LANES), lambda h, t: (0, h))
    mat = lambda nrows: pl.BlockSpec((nrows, LANES), lambda h, t: (0, h))
    full = lambda shape: pl.BlockSpec(shape, lambda h, t: (0, 0))
    tri = jnp.asarray(np.tril(np.ones((RW_CHUNK, RW_CHUNK), np.float32)), dtype=BF16)
    return pl.pallas_call(
        _rwkv_kernel,
        grid=(n_hp, nt),
        in_specs=[
            colspec(COL_R), colspec(COL_K), colspec(COL_V),
            pl.BlockSpec((B, ts, RW_LO_FEAT), lambda h, t: (0, t, 0)),
            colspec(COL_GA),
            mat(3),
            vec, mat(RW_DECAY_RANK), vec, mat(RW_ICLR_RANK), mat(RW_GATE_RANK),
            vec, vec, vec, vec, vec,
            full(tri.shape),
        ],
        out_specs=pl.BlockSpec((B, ts, LANES), lambda h, t: (0, t, h)),
        out_shape=jax.ShapeDtypeStruct((B, S, RW_WIDTH), BF16),
        scratch_shapes=[pltpu.VMEM((B, LANES, LANES), F32), pltpu.VMEM((B, 8, LANES), F32)],
        compiler_params=_cparams(("parallel", "arbitrary")),
        name="rwkv7",
    )(cols3, cols3, cols3, lo_feat, cols3, p["mu_rkv"],
      p["w0"], p["w2"], p["a0"], p["a2"], p["g2"],
      p["k_k"], p["k_a"], p["r_k"], p["ln_w"], p["ln_b"], tri)


def _hg_level_matrix():
    c = HG_CHUNK
    m = np.zeros((HG_LEVELS + 1, c, c), np.float32)
    for l in range(HG_LEVELS):
        bs = c >> l
        half = bs // 2
        for t in range(c):
            mid = (t // bs) * bs + half
            if t % bs >= half:
                m[l, t, mid:t + 1] = 1.0
            else:
                m[l, t, t + 1:mid] = 1.0
    m[HG_LEVELS] = np.tril(np.ones((c, c), np.float32))
    return m.reshape((HG_LEVELS + 1) * c, c)


def _hgrn_kernel(q_ref, f_ref, i_ref, g_ref, gb_ref, lbl_ref, ng_ref, lvl_ref,
                 o_ref, state_ref):
    nb, ts, _ = q_ref.shape
    c = HG_CHUNK
    gl = HG_GROUP * c
    n_groups = ts // gl

    @pl.when(pl.program_id(1) == 0)
    def _():
        state_ref[...] = jnp.zeros_like(state_ref)

    logits = lbl_ref[...]
    mx = jnp.max(logits, axis=0, keepdims=True)
    ex = jnp.exp(logits - mx)
    lb = ex[0:1, :] / jnp.sum(ex, axis=0, keepdims=True)

    ri = lax.broadcasted_iota(jnp.int32, (c, c), 0)
    ci = lax.broadcasted_iota(jnp.int32, (c, c), 1)
    rowi = lax.broadcasted_iota(jnp.int32, (c, LANES), 0)
    lvl = lvl_ref[...]
    ng = ng_ref[...]
    second, valid = [], []
    for l in range(HG_LEVELS):
        bs = c >> l
        second.append((rowi & (bs - 1)) >= (bs // 2))
        valid.append(((ri // bs) == (ci // bs)) & ((ri & (bs - 1)) >= (bs // 2))
                     & ((ci & (bs - 1)) < (bs // 2)))

    pair = (ri == ci + 1) & ((ri & 1) == 1)
    pre, d, outs = {}, {}, {}

    def prologue(b, g):
        rs = slice(g * gl, (g + 1) * gl)
        f = lb + (1.0 - lb) * _sigmoid(f_ref[b, rs, :].astype(F32))
        qraw = q_ref[b, rs, :].astype(F32)
        pre[b, g] = dict(f=f, lf=LOG2E * jnp.log(f), kx=1.0 - f, qs=qraw * _sigmoid(qraw),
                         iv=i_ref[b, rs, :])

    def rows(name, it):
        b, ch = it
        lc = ch % HG_GROUP
        return pre[b, ch // HG_GROUP][name][lc * c:(lc + 1) * c]

    def s_part(its):
        for it in its:
            d["part", it] = _sums2(lvl, rows("lf", it))
            q_c, k_c = rows("qs", it), rows("kx", it)
            dsum = jnp.sum(q_c * k_c, axis=-1, keepdims=True)
            psum = jnp.sum(q_c * rows("f", it) * pltpu.roll(k_c, 1, axis=0), axis=-1, keepdims=True)
            d["sc", it] = jnp.where(ri == ci, dsum, jnp.where(pair, psum, 0.0))

    def s_level(l):
        def stage(its):
            for it in its:
                qk = (jnp.where(second[l], rows("qs", it), rows("kx", it))
                      * jnp.exp2(d["part", it][l * c:(l + 1) * c])).astype(BF16)
                d["sc", it] = jnp.where(valid[l], _dot_nt(qk, qk), d["sc", it])
        return stage

    def s_out(its):
        for it in its:
            bcum = d["part", it][HG_LEVELS * c:(HG_LEVELS + 1) * c]
            b_end = bcum[c - 1:c, :]
            d["o_intra", it] = _dot(d["sc", it], rows("iv", it))
            d["zc", it] = _dot_tn(rows("iv", it), rows("kx", it) * jnp.exp2(b_end - bcum))
            d["qb", it] = rows("qs", it) * jnp.exp2(bcum)
            d["dec", it] = jnp.exp2(b_end)

    stages = [s_part] + [s_level(l) for l in range(HG_LEVELS)] + [s_out]

    def tail(b, g, st):
        rs = slice(g * gl, (g + 1) * gl)
        os_ = []
        for ch in range(g * HG_GROUP, (g + 1) * HG_GROUP):
            it = (b, ch)
            os_.append(d["o_intra", it] + _dot_nt(d["qb", it], st[b]))
            st[b] = st[b] * d["dec", it] + d["zc", it]
        o = jnp.concatenate(os_, axis=0)
        o = o * lax.rsqrt(jnp.mean(o * o, axis=-1, keepdims=True) + NORM_EPS) * ng
        graw = g_ref[b, rs, :].astype(F32)
        o_ref[b, rs, :] = (o * (graw * _sigmoid(graw))
                           * _sigmoid(gb_ref[b, rs, :].astype(F32))).astype(o_ref.dtype)

    batches = list(range(nb))
    group_items = [[(b, ch) for ch in range(g * HG_GROUP, (g + 1) * HG_GROUP) for b in batches]
                   for g in range(n_groups)]
    st = {b: state_ref[b] for b in batches}

    def interleave(stage_items, units):
        per = -(-len(units) // len(stages)) if units else 0
        for n, stage in enumerate(stages):
            stage(stage_items)
            for u in units[n * per:(n + 1) * per]:
                u()

    for b in batches:
        prologue(b, 0)
    for g in range(n_groups):
        units = ([functools.partial(prologue, b, g + 1) for b in batches] if g + 1 < n_groups else [])
        units += ([functools.partial(tail, b, g - 1, st) for b in batches] if g > 0 else [])
        interleave(group_items[g], units)
    for b in batches:
        tail(b, n_groups - 1, st)
        state_ref[b] = st[b]


def _hgrn(cols3, lb_logits, norm_g):
    B, S, _ = cols3.shape
    ts = _pick(S, (2 * HG_GROUP * HG_CHUNK, HG_GROUP * HG_CHUNK))
    nt = S // ts
    cb = lambda col: col // LANES

    def colspec(col0):
        return pl.BlockSpec((B, ts, LANES), lambda h, t: (0, t, cb(col0) + h))

    n_slots = lb_logits.shape[0]
    lvl = jnp.asarray(_hg_level_matrix(), dtype=BF16)
    return pl.pallas_call(
        _hgrn_kernel,
        grid=(HG_HEADS, nt),
        in_specs=[
            colspec(COL_HQ), colspec(COL_HF), colspec(COL_HI), colspec(COL_HG), colspec(COL_GB),
            pl.BlockSpec((n_slots, LANES), lambda h, t: (0, h)),
            pl.BlockSpec((1, LANES), lambda h, t: (0, 0)),
            pl.BlockSpec(lvl.shape, lambda h, t: (0, 0)),
        ],
        out_specs=pl.BlockSpec((B, ts, LANES), lambda h, t: (0, t, h)),
        out_shape=jax.ShapeDtypeStruct((B, S, D_MODEL), BF16),
        scratch_shapes=[pltpu.VMEM((B, HG_KEY, HG_KEY), F32)],
        compiler_params=_cparams(("parallel", "arbitrary")),
        name="hgrn2",
    )(cols3, cols3, cols3, cols3, cols3, lb_logits, norm_g, lvl)


def _memkv_kernel(m_ref, g_ref, wk_ref, wv_ref, k_ref, v_ref):
    m = _rms(m_ref[...], g_ref[...]).astype(BF16)
    k_ref[...] = jnp.dot(m, wk_ref[...], preferred_element_type=F32).astype(BF16)
    v_ref[...] = jnp.dot(m, wv_ref[...], preferred_element_type=F32).astype(BF16)


def _memkv(mem2, g, wk, wv, B, n_mem):
    row = pl.BlockSpec((n_mem, D_MODEL), lambda b: (b, 0))
    wspec = pl.BlockSpec((D_MODEL, D_MODEL), lambda b: (0, 0))
    sds = jax.ShapeDtypeStruct((B * n_mem, D_MODEL), BF16)
    return pl.pallas_call(
        _memkv_kernel,
        grid=(B,),
        in_specs=[row, pl.BlockSpec((1, D_MODEL), lambda b: (0, 0)), wspec, wspec],
        out_specs=[row, row],
        out_shape=[sds, sds],
        compiler_params=_cparams(("parallel",)),
        name="memkv",
    )(mem2, g, wk, wv)


def _tail_kernel(x_ref, ya_ref, yb_ref, wout_ref, gx_ref, wq_ref, k_ref, v_ref, wo_ref,
                 gf_ref, w1_ref, w3_ref, w2_ref, gfin_ref, o_ref):
    h = x_ref[...] + _dot(ya_ref[...].astype(F32) + yb_ref[...].astype(F32), wout_ref[...])
    u = _rms(h, gx_ref[...]).astype(BF16)
    q = jnp.dot(u, wq_ref[...], preferred_element_type=F32)
    k = k_ref[...]
    v = v_ref[...]
    heads = []
    for hd in range(XA_HEADS):
        sl = slice(hd * XA_HEAD, (hd + 1) * XA_HEAD)
        s = _dot_nt(q[:, sl], k[:, sl]) * (XA_HEAD ** -0.5)
        s = s - jnp.max(s, axis=-1, keepdims=True)
        e = jnp.exp(s)
        p = e / jnp.sum(e, axis=-1, keepdims=True)
        heads.append(_dot(p, v[:, sl]))
    h = h + _dot(jnp.concatenate(heads, axis=-1), wo_ref[...])
    u = _rms(h, gf_ref[...]).astype(BF16)
    a = jnp.dot(u, w1_ref[...], preferred_element_type=F32)
    b = jnp.dot(u, w3_ref[...], preferred_element_type=F32)
    mid = (a * _sigmoid(a)) * b
    o_ref[...] = _rms(h + _dot(mid, w2_ref[...]), gfin_ref[...])


def _tail(x2, ya, yb, wout, gx, wq, km, vm, wo, gf, w1, w3, w2, gfin, B, S, n_mem):
    tm = _pick(S, (512, 256))
    nt = S // tm
    row = pl.BlockSpec((tm, D_MODEL), lambda b, t: (b * nt + t, 0))
    vec = pl.BlockSpec((1, D_MODEL), lambda b, t: (0, 0))
    kv = pl.BlockSpec((n_mem, D_MODEL), lambda b, t: (b, 0))
    resident = lambda shape: pl.BlockSpec(shape, lambda b, t: (0, 0), pipeline_mode=pl.Buffered(1))
    sq = resident((D_MODEL, D_MODEL))
    return pl.pallas_call(
        _tail_kernel,
        grid=(B, nt),
        in_specs=[row, row, row, sq, vec, sq, kv, kv, sq,
                  vec, resident((D_MODEL, D_FF)), resident((D_MODEL, D_FF)),
                  resident((D_FF, D_MODEL)), vec],
        out_specs=row,
        out_shape=jax.ShapeDtypeStruct((B * S, D_MODEL), F32),
        compiler_params=_cparams(("parallel", "parallel")),
        name="tail",
    )(x2, ya, yb, wout, gx, wq, km, vm, wo, gf, w1, w3, w2, gfin)


def _pack_in_weights(w_in):
    rw_cols = 3 * RW_WIDTH + RW_LO
    w_t = jnp.swapaxes(w_in, 0, 1).astype(BF16)
    pad = jnp.zeros((RW_LO_PAD - RW_LO, D_MODEL), BF16)
    return jnp.concatenate([w_t[:rw_cols], pad, w_t[rw_cols:]], axis=0)


def _pack_mu_lo(mu):
    lo = mu[3 * RW_WIDTH:]
    return jnp.concatenate([lo, jnp.zeros((RW_LO_PAD - RW_LO,), mu.dtype)])[None, :].astype(F32)


def kernel(x, mem, norm_mix_g, w_in, rw_mu, rw_w0, rw_w2, rw_a0, rw_a2, rw_g2, rw_k_k, rw_k_a, rw_r_k, rw_ln_w, rw_ln_b, hg_lb_logits, hg_norm_g, w_out, norm_xa_g, norm_mem_g, xa_wq, xa_wk, xa_wv, xa_wo, norm_ffn_g, ffn_w1, ffn_w3, ffn_w2, norm_final_g):
    B, S, _ = x.shape
    n_mem = mem.shape[1]
    depth = w_in.shape[0]
    assert depth == 1, "single-layer block"
    assert S % (RW_GROUP * RW_CHUNK) == 0 and S % (HG_GROUP * HG_CHUNK) == 0
    l = 0
    T = B * S
    row = lambda a: a.reshape(1, -1).astype(F32)

    x2 = x.reshape(T, D_MODEL)
    cols = _inproj(x2, row(norm_mix_g[l]), _pack_in_weights(w_in[l]))
    cols3 = cols.reshape(B, S, N_COLS)

    mu = rw_mu[l].astype(F32)
    lo_feat = _lofeat(cols3, _pack_mu_lo(mu))
    rw = dict(
        mu_rkv=mu[:3 * RW_WIDTH].reshape(3, RW_WIDTH),
        w0=row(rw_w0[l]), w2=rw_w2[l].astype(BF16), a0=row(rw_a0[l]), a2=rw_a2[l].astype(BF16),
        g2=rw_g2[l].astype(BF16), k_k=row(rw_k_k[l]), k_a=row(rw_k_a[l]), r_k=row(rw_r_k[l]),
        ln_w=row(rw_ln_w[l]), ln_b=row(rw_ln_b[l]))
    ya = _rwkv(cols3, lo_feat, rw).reshape(T, D_MODEL)
    yb = _hgrn(cols3, hg_lb_logits.astype(F32), row(hg_norm_g[l])).reshape(T, D_MODEL)

    km, vm = _memkv(mem.reshape(B * n_mem, D_MODEL), row(norm_mem_g[l]),
                    xa_wk[l].astype(BF16), xa_wv[l].astype(BF16), B, n_mem)
    out = _tail(x2, ya, yb, w_out[l].astype(BF16), row(norm_xa_g[l]), xa_wq[l].astype(BF16), km, vm,
                xa_wo[l].astype(BF16), row(norm_ffn_g[l]), ffn_w1[l].astype(BF16),
                ffn_w3[l].astype(BF16), ffn_w2[l].astype(BF16), row(norm_final_g), B, S, n_mem)
    return out.reshape(B, S, D_MODEL)
```

```python
import functools

import numpy as np
import jax
import jax.numpy as jnp
from jax import lax
from jax.experimental import pallas as pl
from jax.experimental.pallas import tpu as pltpu

F32 = jnp.float32
BF16 = jnp.bfloat16

D_MODEL = 1024
NORM_EPS = 1e-6
LANES = 128

RW_HEAD = 64
RW_WIDTH = D_MODEL
RW_DECAY_RANK = 64
RW_ICLR_RANK = 64
RW_GATE_RANK = 160
RW_LN_EPS = 64e-5
RW_LO = RW_DECAY_RANK + RW_ICLR_RANK + RW_GATE_RANK
RW_LO_PAD = 512
RW_LO_FEAT = 384
RW_CHUNK = 64
RW_SUB = 16
RW_GROUP = 4
LOG2E = 1.4426950408889634
EXP_M_HALF = 0.6065306597126334

HG_KEY = 128
HG_HEADS = D_MODEL // HG_KEY
HG_CHUNK = 64
HG_LEVELS = 5
HG_GROUP = 4

XA_HEADS = 4
XA_HEAD = D_MODEL // XA_HEADS
D_FF = 2816

COL_R = 0
COL_K = COL_R + RW_WIDTH
COL_V = COL_K + RW_WIDTH
COL_LO = COL_V + RW_WIDTH
COL_HQ = COL_LO + RW_LO_PAD
COL_HF = COL_HQ + D_MODEL
COL_HI = COL_HF + D_MODEL
COL_HG = COL_HI + D_MODEL
COL_GA = COL_HG + D_MODEL
COL_GB = COL_GA + D_MODEL
N_COLS = COL_GB + D_MODEL

COLS_DTYPE = BF16
PAD_SHIFT = RW_LO_PAD - RW_LO
IN_COL_CHUNKS = ((0, COL_HQ), (COL_HQ, COL_HQ + 2048), (COL_HQ + 2048, COL_HQ + 4096),
                 (COL_HQ + 4096, N_COLS))

VMEM_LIMIT = 56 * 1024 * 1024


def _cparams(sem):
    return pltpu.CompilerParams(dimension_semantics=sem, vmem_limit_bytes=VMEM_LIMIT)


def _sigmoid(x):
    return 1.0 / (1.0 + jnp.exp(-x))


def _rms(x, g):
    ms = jnp.mean(x * x, axis=-1, keepdims=True)
    return x * lax.rsqrt(ms + NORM_EPS) * g


def _dot(a, b):
    return jnp.dot(a.astype(BF16), b.astype(BF16), preferred_element_type=F32)


def _dot_nt(a, b):
    return lax.dot_general(a.astype(BF16), b.astype(BF16), (((1,), (1,)), ((), ())),
                           preferred_element_type=F32)


def _dot_tn(a, b):
    return lax.dot_general(a.astype(BF16), b.astype(BF16), (((0,), (0,)), ((), ())),
                           preferred_element_type=F32)


def _split3(x):
    hi = x.astype(BF16)
    r1 = x - hi.astype(F32)
    mid = r1.astype(BF16)
    lo = (r1 - mid.astype(F32)).astype(BF16)
    return jnp.concatenate([hi, mid, lo], axis=1)


def _split2(x):
    hi = x.astype(BF16)
    mid = (x - hi.astype(F32)).astype(BF16)
    return jnp.concatenate([hi, mid], axis=1)


def _sums2(pattern_bf16, x):
    y = jnp.dot(pattern_bf16, _split2(x), preferred_element_type=F32)
    return y[:, 0:LANES] + y[:, LANES:2 * LANES]


def _exact_sums(pattern_bf16, x):
    y = jnp.dot(pattern_bf16, _split3(x), preferred_element_type=F32)
    return y[:, 0:LANES] + y[:, LANES:2 * LANES] + y[:, 2 * LANES:3 * LANES]


def _pick(n, prefs):
    for p in prefs:
        if n % p == 0:
            return p
    return n


def _inproj_kernel(x_ref, g_ref, w_ref, o_ref):
    u = _rms(x_ref[...], g_ref[...]).astype(BF16)
    for c0, c1 in IN_COL_CHUNKS:
        r0 = c0 if c0 < COL_HQ else c0 - PAD_SHIFT
        o_ref[:, c0:c1] = lax.dot_general(u, w_ref[r0:r0 + (c1 - c0), :], (((1,), (1,)), ((), ())),
                                          preferred_element_type=F32).astype(o_ref.dtype)


def _inproj(x2, g, w_t):
    T = x2.shape[0]
    tm = _pick(T, (512, 256))
    return pl.pallas_call(
        _inproj_kernel,
        grid=(T // tm,),
        in_specs=[
            pl.BlockSpec((tm, D_MODEL), lambda i: (i, 0)),
            pl.BlockSpec((1, D_MODEL), lambda i: (0, 0)),
            pl.BlockSpec(w_t.shape, lambda i: (0, 0), pipeline_mode=pl.Buffered(1)),
        ],
        out_specs=pl.BlockSpec((tm, N_COLS), lambda i: (i, 0)),
        out_shape=jax.ShapeDtypeStruct((T, N_COLS), COLS_DTYPE),
        compiler_params=_cparams(("parallel",)),
        name="inproj",
    )(x2, g, w_t)


def _shift_mix(raw, carry_row, mu):
    rolled = pltpu.roll(raw, 1, axis=0)
    row = lax.broadcasted_iota(jnp.int32, raw.shape, 0)
    prev = jnp.where(row == 0, carry_row, rolled)
    return raw + mu * (prev - raw)


def _lofeat_kernel(lo_ref, mu_ref, o_ref, carry_ref):
    ts = lo_ref.shape[1]

    @pl.when(pl.program_id(1) == 0)
    def _():
        carry_ref[...] = jnp.zeros_like(carry_ref)

    raw = lo_ref[0].astype(F32)
    lo = _shift_mix(raw, carry_ref[0:1, :], mu_ref[...])
    carry_ref[0:1, :] = raw[ts - 1:ts, :]
    lane = lax.broadcasted_iota(jnp.int32, (ts, RW_LO_FEAT), 1)
    x = lo[:, 0:RW_LO_FEAT]
    feat = jnp.where(lane < RW_DECAY_RANK, jnp.tanh(x),
                     jnp.where(lane < LANES, x, _sigmoid(x)))
    o_ref[0] = feat.astype(o_ref.dtype)


def _lofeat(cols3, mu_lo):
    B, S, _ = cols3.shape
    ts = _pick(S, (1024, 512, 256, 128, 64))
    return pl.pallas_call(
        _lofeat_kernel,
        grid=(B, S // ts),
        in_specs=[
            pl.BlockSpec((1, ts, RW_LO_PAD), lambda b, t: (b, t, COL_LO // RW_LO_PAD)),
            pl.BlockSpec((1, RW_LO_PAD), lambda b, t: (0, 0)),
        ],
        out_specs=pl.BlockSpec((1, ts, RW_LO_FEAT), lambda b, t: (b, t, 0)),
        out_shape=jax.ShapeDtypeStruct((B, S, RW_LO_FEAT), BF16),
        scratch_shapes=[pltpu.VMEM((8, RW_LO_PAD), F32)],
        compiler_params=_cparams(("parallel", "arbitrary")),
        name="lofeat",
    )(cols3, mu_lo)


def _pair_sum(x):
    lane = lax.broadcasted_iota(jnp.int32, x.shape, 1)
    first = lane < RW_HEAD
    s0 = jnp.sum(jnp.where(first, x, 0.0), axis=-1, keepdims=True)
    s1 = jnp.sum(jnp.where(first, 0.0, x), axis=-1, keepdims=True)
    return jnp.where(first, s0, s1)


def _rwkv_kernel(r_ref, k_ref, v_ref, lo_ref, ga_ref, mu_ref,
                 w0_ref, w2_ref, a0_ref, a2_ref, g2_ref,
                 kk_ref, ka_ref, rk_ref, lnw_ref, lnb_ref, tri_ref,
                 o_ref, state_ref, carry_ref):
    nb, ts, _ = r_ref.shape
    c = RW_CHUNK
    n_chunks = ts // c

    @pl.when(pl.program_id(1) == 0)
    def _():
        state_ref[...] = jnp.zeros_like(state_ref)
        carry_ref[...] = jnp.zeros_like(carry_ref)

    t_i = lax.broadcasted_iota(jnp.int32, (c, LANES), 0)
    s_i = lax.broadcasted_iota(jnp.int32, (c, LANES), 1) & (RW_HEAD - 1)
    strict = t_i > s_i
    incl = t_i >= s_i
    eye = (t_i == s_i).astype(F32)
    same_sub = (t_i // RW_SUB) == (s_i // RW_SUB)
    rr = lax.broadcasted_iota(jnp.int32, (LANES, LANES), 0)
    cc = lax.broadcasted_iota(jnp.int32, (LANES, LANES), 1)
    bd_mask = (rr // RW_HEAD) == (cc // RW_HEAD)

    def bd(x):
        return jnp.where(bd_mask, jnp.concatenate([x, x], axis=0), 0.0).astype(BF16)

    def pmul(a, bmat):
        return _dot(a, bd(bmat))

    tri = tri_ref[...]
    pre, pc, d, ys = {}, {}, {}, {}

    gl = RW_GROUP * c
    n_groups = ts // gl
    last_raw = {}

    def prologue(b, g):
        rs = slice(g * gl, (g + 1) * gl)
        rkv = []
        for n, ref in enumerate((r_ref, k_ref, v_ref)):
            raw = ref[b, rs, :].astype(F32)
            carry = carry_ref[b, n:n + 1, :] if g == 0 else last_raw[b, n]
            rkv.append(_shift_mix(raw, carry, mu_ref[n:n + 1, :]))
            last_raw[b, n] = raw[gl - 1:gl, :]
            if g == n_groups - 1:
                carry_ref[b, n:n + 1, :] = last_raw[b, n]
        r, k, v = rkv
        lo = lo_ref[b, rs, :]
        wl = w0_ref[...] + _dot(lo[:, 0:RW_DECAY_RANK], w2_ref[...])
        lw = (-LOG2E * EXP_M_HALF) * _sigmoid(wl)
        a = _sigmoid(a0_ref[...] + _dot(lo[:, RW_DECAY_RANK:LANES], a2_ref[...]))
        gate = _dot(lo[:, LANES:LANES + RW_GATE_RANK], g2_ref[...])
        kk = k * kk_ref[...]
        kkn = kk * lax.rsqrt(jnp.maximum(_pair_sum(kk * kk), 1e-24))
        kmod = k * (1.0 + (a - 1.0) * ka_ref[...])
        pre[b, g] = dict(r=r, v=v, kmod=kmod, g=gate, lw=lw, kkn=kkn, beta=kkn * a)

    def rows(name, it):
        b, ch = it
        lc = ch % RW_GROUP
        return pre[b, ch // RW_GROUP][name][lc * c:(lc + 1) * c]

    def prep(it):
        lw_c = rows("lw", it)
        cum = _exact_sums(tri, lw_c)
        c_end = cum[c - 1:c, :]
        e_neg = jnp.exp2(-cum)
        e_end = jnp.exp2(c_end - cum)
        beta_c, kmod_c = rows("beta", it), rows("kmod", it)
        pc[it] = dict(
            rt=rows("r", it) * jnp.exp2(cum),
            at=-rows("kkn", it) * jnp.exp2(cum - lw_c),
            bt=beta_c * e_neg, kt=kmod_c * e_neg,
            bh=(beta_c * e_end).astype(BF16), kh=(kmod_c * e_end).astype(BF16),
            w_end=jnp.exp2(c_end))

    def s_amat(its):
        for it in its:
            d["lhs2", it] = jnp.concatenate([pc[it]["at"], pc[it]["rt"]], axis=0).astype(BF16)
            d["amat", it] = _dot_nt(d["lhs2", it], jnp.concatenate(
                [bd(pc[it]["bt"]), bd(pc[it]["kt"])], axis=0))

    def s_split(its):
        for it in its:
            am = d["amat", it]
            n_ab = jnp.where(strict, am[0:c, 0:LANES], 0.0)
            d["a_rb", it] = jnp.where(incl, am[c:2 * c, 0:LANES], 0.0).astype(BF16)
            avk = _dot(jnp.concatenate([jnp.where(strict, am[0:c, LANES:2 * LANES], 0.0),
                                        jnp.where(incl, am[c:2 * c, LANES:2 * LANES], 0.0)], axis=0),
                       bd(rows("v", it)))
            d["u0", it], d["yv", it] = avk[0:c], avk[c:2 * c]
            d["dg", it] = jnp.where(same_sub, n_ab, 0.0)
            d["off", it] = (n_ab - d["dg", it]).astype(BF16)

    def s_d2(its):
        for it in its:
            d["d2", it] = pmul(d["dg", it], d["dg", it])
            d["x", it] = eye + d["dg", it]

    def s_x(pw, nxt):
        def stage(its):
            for it in its:
                if nxt:
                    both = pmul(jnp.concatenate([d["x", it], d[pw, it]], axis=0), d[pw, it])
                    d["x", it] = d["x", it] + both[0:c]
                    d[nxt, it] = both[c:2 * c]
                else:
                    d["x", it] = d["x", it] + pmul(d["x", it], d[pw, it])
        return stage

    def s_r(its):
        for it in its:
            d["r", it] = pmul(d["off", it], d["x", it])

    def s_r2(its):
        for it in its:
            both = pmul(jnp.concatenate([d["x", it], d["r", it]], axis=0), d["r", it])
            d["z", it] = d["x", it] + both[0:c]
            d["r2", it] = both[c:2 * c]

    def s_tinv(its):
        for it in its:
            d["t_inv", it] = (d["z", it] + pmul(d["z", it], d["r2", it])).astype(BF16)

    def s_au(its):
        for it in its:
            au = _dot(d["t_inv", it], jnp.concatenate([bd(pc[it]["at"]), bd(d["u0", it])], axis=1))
            d["ahat", it], d["uu", it] = au[:, 0:LANES], au[:, LANES:2 * LANES]

    def s_trans(its):
        for it in its:
            lhs = jnp.concatenate(
                [jnp.concatenate([d["ahat", it], d["uu", it]], axis=1),
                 jnp.concatenate([jnp.zeros((c, LANES), BF16), rows("v", it).astype(BF16)], axis=1)],
                axis=0)
            mg = _dot_tn(lhs, jnp.concatenate([pc[it]["bh"], pc[it]["kh"]], axis=0))
            d["mmat", it] = jnp.where(bd_mask, mg[0:LANES], 0.0).astype(BF16)
            d["gmat", it] = jnp.where(bd_mask, mg[LANES:2 * LANES], 0.0)
            ru = _dot(d["a_rb", it], jnp.concatenate([bd(d["ahat", it]), bd(d["uu", it])], axis=1))
            d["rhat", it] = (pc[it]["rt"] + ru[:, 0:LANES]).astype(BF16)
            d["y0", it] = ru[:, LANES:2 * LANES] + d["yv", it]

    stages = [s_amat, s_split, s_d2, s_x("d2", "d4"), s_x("d4", "d8"), s_x("d8", None),
              s_r, s_r2, s_tinv, s_au, s_trans]

    def chain_step(bs, ch, st):
        st_bf = {b: st[b].astype(BF16) for b in bs}
        for b in bs:
            ys[b, ch] = d["y0", (b, ch)] + _dot_nt(d["rhat", (b, ch)], st_bf[b])
        for b in bs:
            st[b] = (st[b] * pc[b, ch]["w_end"] + _dot(st_bf[b], d["mmat", (b, ch)])
                     + d["gmat", (b, ch)])

    def epilogue(b, g):
        rs = slice(g * gl, (g + 1) * gl)
        y = jnp.concatenate([ys[b, ch] for ch in range(g * RW_GROUP, (g + 1) * RW_GROUP)], axis=0)
        pb = pre[b, g]
        mean = _pair_sum(y) * (1.0 / RW_HEAD)
        dlt = y - mean
        var = _pair_sum(dlt * dlt) * (1.0 / RW_HEAD)
        on = dlt * lax.rsqrt(var + RW_LN_EPS) * lnw_ref[...] + lnb_ref[...]
        bonus = _pair_sum(pb["r"] * pb["kmod"] * rk_ref[...]) * pb["v"]
        o_ref[b, rs, :] = (((on + bonus) * pb["g"])
                           * _sigmoid(ga_ref[b, rs, :].astype(F32))).astype(o_ref.dtype)

    batches = list(range(nb))
    group_items = [[(b, ch) for ch in range(g * RW_GROUP, (g + 1) * RW_GROUP) for b in batches]
                   for g in range(n_groups)]
    st = {}

    def prep_units(g):
        return ([functools.partial(prologue, b, g) for b in batches]
                + [functools.partial(prep, it) for it in group_items[g]])

    def tail_units(g):
        units = [functools.partial(chain_step, batches, ch, st)
                 for ch in range(g * RW_GROUP, (g + 1) * RW_GROUP)]
        return units + [functools.partial(epilogue, b, g) for b in batches]

    def interleave(stage_items, units):
        per = -(-len(units) // len(stages)) if units else 0
        for n, stage in enumerate(stages):
            stage(stage_items)
            for u in units[n * per:(n + 1) * per]:
                u()

    for b in batches:
        st[b] = state_ref[b]
    for u in prep_units(0):
        u()
    for g in range(n_groups):
        units = (prep_units(g + 1) if g + 1 < n_groups else []) + (tail_units(g - 1) if g > 0 else [])
        interleave(group_items[g], units)
    for u in tail_units(n_groups - 1):
        u()
    for b in batches:
        state_ref[b] = st[b]


def _rwkv(cols3, lo_feat, p):
    B, S, _ = cols3.shape
    ts = _pick(S, (2 * RW_GROUP * RW_CHUNK, RW_GROUP * RW_CHUNK))
    nt = S // ts
    n_hp = RW_WIDTH // LANES
    cb = lambda col: col // LANES

    def colspec(col0):
        return pl.BlockSpec((B, ts, LANES), lambda h, t: (0, t, cb(col0) + h))

    vec = pl.BlockSpec((1, LANES), lambda h, t: (0, h))
    mat = lambda nrows: pl.BlockSpec((nrows, LANES), lambda h, t: (0, h))
    full = lambda shape: pl.BlockSpec(shape, lambda h, t: (0, 0))
    tri = jnp.asarray(np.tril(np.ones((RW_CHUNK, RW_CHUNK), np.float32)), dtype=BF16)
    return pl.pallas_call(
        _rwkv_kernel,
        grid=(n_hp, nt),
        in_specs=[
            colspec(COL_R), colspec(COL_K), colspec(COL_V),
            pl.BlockSpec((B, ts, RW_LO_FEAT), lambda h, t: (0, t, 0)),
            colspec(COL_GA),
            mat(3),
            vec, mat(RW_DECAY_RANK), vec, mat(RW_ICLR_RANK), mat(RW_GATE_RANK),
            vec, vec, vec, vec, vec,
            full(tri.shape),
        ],
        out_specs=pl.BlockSpec((B, ts, LANES), lambda h, t: (0, t, h)),
        out_shape=jax.ShapeDtypeStruct((B, S, RW_WIDTH), BF16),
        scratch_shapes=[pltpu.VMEM((B, LANES, LANES), F32), pltpu.VMEM((B, 8, LANES), F32)],
        compiler_params=_cparams(("parallel", "arbitrary")),
        name="rwkv7",
    )(cols3, cols3, cols3, lo_feat, cols3, p["mu_rkv"],
      p["w0"], p["w2"], p["a0"], p["a2"], p["g2"],
      p["k_k"], p["k_a"], p["r_k"], p["ln_w"], p["ln_b"], tri)


def _hg_level_matrix():
    c = HG_CHUNK
    m = np.zeros((HG_LEVELS + 1, c, c), np.float32)
    for l in range(HG_LEVELS):
        bs = c >> l
        half = bs // 2
        for t in range(c):
            mid = (t // bs) * bs + half
            if t % bs >= half:
                m[l, t, mid:t + 1] = 1.0
            else:
                m[l, t, t + 1:mid] = 1.0
    m[HG_LEVELS] = np.tril(np.ones((c, c), np.float32))
    return m.reshape((HG_LEVELS + 1) * c, c)


def _hgrn_kernel(q_ref, f_ref, i_ref, g_ref, gb_ref, lbl_ref, ng_ref, lvl_ref,
                 o_ref, state_ref):
    nb, ts, _ = q_ref.shape
    c = HG_CHUNK
    gl = HG_GROUP * c
    n_groups = ts // gl

    @pl.when(pl.program_id(1) == 0)
    def _():
        state_ref[...] = jnp.zeros_like(state_ref)

    logits = lbl_ref[...]
    mx = jnp.max(logits, axis=0, keepdims=True)
    ex = jnp.exp(logits - mx)
    lb = ex[0:1, :] / jnp.sum(ex, axis=0, keepdims=True)

    ri = lax.broadcasted_iota(jnp.int32, (c, c), 0)
    ci = lax.broadcasted_iota(jnp.int32, (c, c), 1)
    rowi = lax.broadcasted_iota(jnp.int32, (c, LANES), 0)
    lvl = lvl_ref[...]
    ng = ng_ref[...]
    second, valid = [], []
    for l in range(HG_LEVELS):
        bs = c >> l
        second.append((rowi & (bs - 1)) >= (bs // 2))
        valid.append(((ri // bs) == (ci // bs)) & ((ri & (bs - 1)) >= (bs // 2))
                     & ((ci & (bs - 1)) < (bs // 2)))

    pair = (ri == ci + 1) & ((ri & 1) == 1)
    pre, d, outs = {}, {}, {}

    def prologue(b, g):
        rs = slice(g * gl, (g + 1) * gl)
        f = lb + (1.0 - lb) * _sigmoid(f_ref[b, rs, :].astype(F32))
        qraw = q_ref[b, rs, :].astype(F32)
        pre[b, g] = dict(f=f, lf=LOG2E * jnp.log(f), kx=1.0 - f, qs=qraw * _sigmoid(qraw),
                         iv=i_ref[b, rs, :])

    def rows(name, it):
        b, ch = it
        lc = ch % HG_GROUP
        return pre[b, ch // HG_GROUP][name][lc * c:(lc + 1) * c]

    def s_part(its):
        for it in its:
            d["part", it] = _sums2(lvl, rows("lf", it))
            q_c, k_c = rows("qs", it), rows("kx", it)
            dsum = jnp.sum(q_c * k_c, axis=-1, keepdims=True)
            psum = jnp.sum(q_c * rows("f", it) * pltpu.roll(k_c, 1, axis=0), axis=-1, keepdims=True)
            d["sc", it] = jnp.where(ri == ci, dsum, jnp.where(pair, psum, 0.0))

    def s_level(l):
        def stage(its):
            for it in its:
                qk = (jnp.where(second[l], rows("qs", it), rows("kx", it))
                      * jnp.exp2(d["part", it][l * c:(l + 1) * c])).astype(BF16)
                d["sc", it] = jnp.where(valid[l], _dot_nt(qk, qk), d["sc", it])
        return stage

    def s_out(its):
        for it in its:
            bcum = d["part", it][HG_LEVELS * c:(HG_LEVELS + 1) * c]
            b_end = bcum[c - 1:c, :]
            d["o_intra", it] = _dot(d["sc", it], rows("iv", it))
            d["zc", it] = _dot_tn(rows("iv", it), rows("kx", it) * jnp.exp2(b_end - bcum))
            d["qb", it] = rows("qs", it) * jnp.exp2(bcum)
            d["dec", it] = jnp.exp2(b_end)

    stages = [s_part] + [s_level(l) for l in range(HG_LEVELS)] + [s_out]

    def tail(b, g, st):
        rs = slice(g * gl, (g + 1) * gl)
        os_ = []
        for ch in range(g * HG_GROUP, (g + 1) * HG_GROUP):
            it = (b, ch)
            os_.append(d["o_intra", it] + _dot_nt(d["qb", it], st[b]))
            st[b] = st[b] * d["dec", it] + d["zc", it]
        o = jnp.concatenate(os_, axis=0)
        o = o * lax.rsqrt(jnp.mean(o * o, axis=-1, keepdims=True) + NORM_EPS) * ng
        graw = g_ref[b, rs, :].astype(F32)
        o_ref[b, rs, :] = (o * (graw * _sigmoid(graw))
                           * _sigmoid(gb_ref[b, rs, :].astype(F32))).astype(o_ref.dtype)

    batches = list(range(nb))
    group_items = [[(b, ch) for ch in range(g * HG_GROUP, (g + 1) * HG_GROUP) for b in batches]
                   for g in range(n_groups)]
    st = {b: state_ref[b] for b in batches}

    def interleave(stage_items, units):
        per = -(-len(units) // len(stages)) if units else 0
        for n, stage in enumerate(stages):
            stage(stage_items)
            for u in units[n * per:(n + 1) * per]:
                u()

    for b in batches:
        prologue(b, 0)
    for g in range(n_groups):
        units = ([functools.partial(prologue, b, g + 1) for b in batches] if g + 1 < n_groups else [])
        units += ([functools.partial(tail, b, g - 1, st) for b in batches] if g > 0 else [])
        interleave(group_items[g], units)
    for b in batches:
        tail(b, n_groups - 1, st)
        state_ref[b] = st[b]


def _hgrn(cols3, lb_logits, norm_g):
    B, S, _ = cols3.shape
    ts = _pick(S, (2 * HG_GROUP * HG_CHUNK, HG_GROUP * HG_CHUNK))
    nt = S // ts
    cb = lambda col: col // LANES

    def colspec(col0):
        return pl.BlockSpec((B, ts, LANES), lambda h, t: (0, t, cb(col0) + h))

    n_slots = lb_logits.shape[0]
    lvl = jnp.asarray(_hg_level_matrix(), dtype=BF16)
    return pl.pallas_call(
        _hgrn_kernel,
        grid=(HG_HEADS, nt),
        in_specs=[
            colspec(COL_HQ), colspec(COL_HF), colspec(COL_HI), colspec(COL_HG), colspec(COL_GB),
            pl.BlockSpec((n_slots, LANES), lambda h, t: (0, h)),
            pl.BlockSpec((1, LANES), lambda h, t: (0, 0)),
            pl.BlockSpec(lvl.shape, lambda h, t: (0, 0)),
        ],
        out_specs=pl.BlockSpec((B, ts, LANES), lambda h, t: (0, t, h)),
        out_shape=jax.ShapeDtypeStruct((B, S, D_MODEL), BF16),
        scratch_shapes=[pltpu.VMEM((B, HG_KEY, HG_KEY), F32)],
        compiler_params=_cparams(("parallel", "arbitrary")),
        name="hgrn2",
    )(cols3, cols3, cols3, cols3, cols3, lb_logits, norm_g, lvl)


def _memkv_kernel(m_ref, g_ref, wk_ref, wv_ref, k_ref, v_ref):
    m = _rms(m_ref[...], g_ref[...]).astype(BF16)
    k_ref[...] = jnp.dot(m, wk_ref[...], preferred_element_type=F32).astype(BF16)
    v_ref[...] = jnp.dot(m, wv_ref[...], preferred_element_type=F32).astype(BF16)


def _memkv(mem2, g, wk, wv, B, n_mem):
    row = pl.BlockSpec((n_mem, D_MODEL), lambda b: (b, 0))
    wspec = pl.BlockSpec((D_MODEL, D_MODEL), lambda b: (0, 0))
    sds = jax.ShapeDtypeStruct((B * n_mem, D_MODEL), BF16)
    return pl.pallas_call(
        _memkv_kernel,
        grid=(B,),
        in_specs=[row, pl.BlockSpec((1, D_MODEL), lambda b: (0, 0)), wspec, wspec],
        out_specs=[row, row],
        out_shape=[sds, sds],
        compiler_params=_cparams(("parallel",)),
        name="memkv",
    )(mem2, g, wk, wv)


def _tail_kernel(x_ref, ya_ref, yb_ref, wout_ref, gx_ref, wq_ref, k_ref, v_ref, wo_ref,
                 gf_ref, w1_ref, w3_ref, w2_ref, gfin_ref, o_ref):
    h = x_ref[...] + _dot(ya_ref[...].astype(F32) + yb_ref[...].astype(F32), wout_ref[...])
    u = _rms(h, gx_ref[...]).astype(BF16)
    q = jnp.dot(u, wq_ref[...], preferred_element_type=F32)
    k = k_ref[...]
    v = v_ref[...]
    heads = []
    for hd in range(XA_HEADS):
        sl = slice(hd * XA_HEAD, (hd + 1) * XA_HEAD)
        s = _dot_nt(q[:, sl], k[:, sl]) * (XA_HEAD ** -0.5)
        s = s - jnp.max(s, axis=-1, keepdims=True)
        e = jnp.exp(s)
        p = e / jnp.sum(e, axis=-1, keepdims=True)
        heads.append(_dot(p, v[:, sl]))
    h = h + _dot(jnp.concatenate(heads, axis=-1), wo_ref[...])
    u = _rms(h, gf_ref[...]).astype(BF16)
    a = jnp.dot(u, w1_ref[...], preferred_element_type=F32)
    b = jnp.dot(u, w3_ref[...], preferred_element_type=F32)
    mid = (a * _sigmoid(a)) * b
    o_ref[...] = _rms(h + _dot(mid, w2_ref[...]), gfin_ref[...])


def _tail(x2, ya, yb, wout, gx, wq, km, vm, wo, gf, w1, w3, w2, gfin, B, S, n_mem):
    tm = _pick(S, (512, 256))
    nt = S // tm
    row = pl.BlockSpec((tm, D_MODEL), lambda b, t: (b * nt + t, 0))
    vec = pl.BlockSpec((1, D_MODEL), lambda b, t: (0, 0))
    kv = pl.BlockSpec((n_mem, D_MODEL), lambda b, t: (b, 0))
    resident = lambda shape: pl.BlockSpec(shape, lambda b, t: (0, 0), pipeline_mode=pl.Buffered(1))
    sq = resident((D_MODEL, D_MODEL))
    return pl.pallas_call(
        _tail_kernel,
        grid=(B, nt),
        in_specs=[row, row, row, sq, vec, sq, kv, kv, sq,
                  vec, resident((D_MODEL, D_FF)), resident((D_MODEL, D_FF)),
                  resident((D_FF, D_MODEL)), vec],
        out_specs=row,
        out_shape=jax.ShapeDtypeStruct((B * S, D_MODEL), F32),
        compiler_params=_cparams(("parallel", "parallel")),
        name="tail",
    )(x2, ya, yb, wout, gx, wq, km, vm, wo, gf, w1, w3, w2, gfin)


def _pack_mu_lo(mu):
    lo = mu[3 * RW_WIDTH:]
    return jnp.concatenate([lo, jnp.zeros((RW_LO_PAD - RW_LO,), mu.dtype)])[None, :].astype(F32)


def kernel(x, mem, norm_mix_g, w_in, rw_mu, rw_w0, rw_w2, rw_a0, rw_a2, rw_g2, rw_k_k, rw_k_a, rw_r_k, rw_ln_w, rw_ln_b, hg_lb_logits, hg_norm_g, w_out, norm_xa_g, norm_mem_g, xa_wq, xa_wk, xa_wv, xa_wo, norm_ffn_g, ffn_w1, ffn_w3, ffn_w2, norm_final_g):
    B, S, _ = x.shape
    n_mem = mem.shape[1]
    depth = w_in.shape[0]
    assert depth == 1, "single-layer block"
    assert S % (RW_GROUP * RW_CHUNK) == 0 and S % (HG_GROUP * HG_CHUNK) == 0
    l = 0
    T = B * S
    row = lambda a: a.reshape(1, -1).astype(F32)

    x2 = x.reshape(T, D_MODEL)
    cols = _inproj(x2, row(norm_mix_g[l]), jnp.swapaxes(w_in[l], 0, 1).astype(BF16))
    cols3 = cols.reshape(B, S, N_COLS)

    mu = rw_mu[l].astype(F32)
    lo_feat = _lofeat(cols3, _pack_mu_lo(mu))
    rw = dict(
        mu_rkv=mu[:3 * RW_WIDTH].reshape(3, RW_WIDTH),
        w0=row(rw_w0[l]), w2=rw_w2[l].astype(BF16), a0=row(rw_a0[l]), a2=rw_a2[l].astype(BF16),
        g2=rw_g2[l].astype(BF16), k_k=row(rw_k_k[l]), k_a=row(rw_k_a[l]), r_k=row(rw_r_k[l]),
        ln_w=row(rw_ln_w[l]), ln_b=row(rw_ln_b[l]))
    ya = _rwkv(cols3, lo_feat, rw).reshape(T, D_MODEL)
    yb = _hgrn(cols3, hg_lb_logits.astype(F32), row(hg_norm_g[l])).reshape(T, D_MODEL)

    km, vm = _memkv(mem.reshape(B * n_mem, D_MODEL), row(norm_mem_g[l]),
                    xa_wk[l].astype(BF16), xa_wv[l].astype(BF16), B, n_mem)
    out = _tail(x2, ya, yb, w_out[l].astype(BF16), row(norm_xa_g[l]), xa_wq[l].astype(BF16), km, vm,
                xa_wo[l].astype(BF16), row(norm_ffn_g[l]), ffn_w1[l].astype(BF16),
                ffn_w3[l].astype(BF16), ffn_w2[l].astype(BF16), row(norm_final_g), B, S, n_mem)
    return out.reshape(B, S, D_MODEL)
```

```python
import functools

import numpy as np
import jax
import jax.numpy as jnp
from jax import lax
from jax.experimental import pallas as pl
from jax.experimental.pallas import tpu as pltpu

F32 = jnp.float32
BF16 = jnp.bfloat16

D_MODEL = 1024
NORM_EPS = 1e-6
LANES = 128

RW_HEAD = 64
RW_WIDTH = D_MODEL
RW_DECAY_RANK = 64
RW_ICLR_RANK = 64
RW_GATE_RANK = 160
RW_LN_EPS = 64e-5
RW_LO = RW_DECAY_RANK + RW_ICLR_RANK + RW_GATE_RANK
RW_LO_PAD = 512
RW_LO_FEAT = 384
RW_CHUNK = 64
RW_SUB = 16
RW_GROUP = 4
LOG2E = 1.4426950408889634
EXP_M_HALF = 0.6065306597126334

HG_KEY = 128
HG_HEADS = D_MODEL // HG_KEY
HG_CHUNK = 64
HG_LEVELS = 5
HG_GROUP = 4

XA_HEADS = 4
XA_HEAD = D_MODEL // XA_HEADS
D_FF = 2816

COL_R = 0
COL_K = COL_R + RW_WIDTH
COL_V = COL_K + RW_WIDTH
COL_LO = COL_V + RW_WIDTH
COL_HQ = COL_LO + RW_LO_PAD
COL_HF = COL_HQ + D_MODEL
COL_HI = COL_HF + D_MODEL
COL_HG = COL_HI + D_MODEL
COL_GA = COL_HG + D_MODEL
COL_GB = COL_GA + D_MODEL
N_COLS = COL_GB + D_MODEL

COLS_DTYPE = BF16
PAD_SHIFT = RW_LO_PAD - RW_LO
IN_COL_CHUNKS = ((0, COL_HQ), (COL_HQ, COL_HQ + 2048), (COL_HQ + 2048, COL_HQ + 4096),
                 (COL_HQ + 4096, N_COLS))

VMEM_LIMIT = 56 * 1024 * 1024


def _cparams(sem):
    return pltpu.CompilerParams(dimension_semantics=sem, vmem_limit_bytes=VMEM_LIMIT)


def _sigmoid(x):
    return 1.0 / (1.0 + jnp.exp(-x))


def _rms(x, g):
    ms = jnp.mean(x * x, axis=-1, keepdims=True)
    return x * lax.rsqrt(ms + NORM_EPS) * g


def _dot(a, b):
    return jnp.dot(a.astype(BF16), b.astype(BF16), preferred_element_type=F32)


def _dot_nt(a, b):
    return lax.dot_general(a.astype(BF16), b.astype(BF16), (((1,), (1,)), ((), ())),
                           preferred_element_type=F32)


def _dot_tn(a, b):
    return lax.dot_general(a.astype(BF16), b.astype(BF16), (((0,), (0,)), ((), ())),
                           preferred_element_type=F32)


def _split2(x):
    hi = x.astype(BF16)
    mid = (x - hi.astype(F32)).astype(BF16)
    return jnp.concatenate([hi, mid], axis=1)


def _sums2(pattern_bf16, x):
    y = jnp.dot(pattern_bf16, _split2(x), preferred_element_type=F32)
    return y[:, 0:LANES] + y[:, LANES:2 * LANES]


def _pick(n, prefs):
    for p in prefs:
        if n % p == 0:
            return p
    return n


def _shift_mix(raw, carry_row, mu):
    rolled = pltpu.roll(raw, 1, axis=0)
    row = lax.broadcasted_iota(jnp.int32, raw.shape, 0)
    prev = jnp.where(row == 0, carry_row, rolled)
    return raw + mu * (prev - raw)


def _inproj_kernel(tiles_per_seq, x_ref, g_ref, w_ref, mu_ref, o_ref, lo_ref, carry_ref):
    tm = x_ref.shape[0]
    u = _rms(x_ref[...], g_ref[...]).astype(BF16)

    @pl.when(pl.program_id(0) % tiles_per_seq == 0)
    def _():
        carry_ref[...] = jnp.zeros_like(carry_ref)

    for c0, c1 in IN_COL_CHUNKS:
        r0 = c0 if c0 < COL_HQ else c0 - PAD_SHIFT
        res = lax.dot_general(u, w_ref[r0:r0 + (c1 - c0), :], (((1,), (1,)), ((), ())),
                              preferred_element_type=F32)
        o_ref[:, c0:c1] = res.astype(o_ref.dtype)
        if c0 <= COL_LO and COL_LO + RW_LO_PAD <= c1:
            raw = res[:, COL_LO - c0:COL_LO - c0 + RW_LO_PAD]
            lo = _shift_mix(raw, carry_ref[0:1, :], mu_ref[...])
            carry_ref[0:1, :] = raw[tm - 1:tm, :]
            lane = lax.broadcasted_iota(jnp.int32, (tm, LANES), 1)
            lo_ref[:, 0:LANES] = jnp.where(lane < RW_DECAY_RANK, jnp.tanh(lo[:, 0:LANES]),
                                           lo[:, 0:LANES]).astype(lo_ref.dtype)
            lo_ref[:, LANES:RW_LO_FEAT] = _sigmoid(lo[:, LANES:RW_LO_FEAT]).astype(lo_ref.dtype)


def _inproj(x2, g, w_t, mu_lo, S):
    T = x2.shape[0]
    tm = _pick(S, (512, 256))
    return pl.pallas_call(
        functools.partial(_inproj_kernel, S // tm),
        grid=(T // tm,),
        in_specs=[
            pl.BlockSpec((tm, D_MODEL), lambda i: (i, 0)),
            pl.BlockSpec((1, D_MODEL), lambda i: (0, 0)),
            pl.BlockSpec(w_t.shape, lambda i: (0, 0), pipeline_mode=pl.Buffered(1)),
            pl.BlockSpec((1, RW_LO_PAD), lambda i: (0, 0)),
        ],
        out_specs=[pl.BlockSpec((tm, N_COLS), lambda i: (i, 0)),
                   pl.BlockSpec((tm, RW_LO_FEAT), lambda i: (i, 0))],
        out_shape=[jax.ShapeDtypeStruct((T, N_COLS), COLS_DTYPE),
                   jax.ShapeDtypeStruct((T, RW_LO_FEAT), BF16)],
        scratch_shapes=[pltpu.VMEM((8, RW_LO_PAD), F32)],
        compiler_params=_cparams(("arbitrary",)),
        name="inproj",
    )(x2, g, w_t, mu_lo)


def _pair_sum(x):
    lane = lax.broadcasted_iota(jnp.int32, x.shape, 1)
    first = lane < RW_HEAD
    s0 = jnp.sum(jnp.where(first, x, 0.0), axis=-1, keepdims=True)
    s1 = jnp.sum(jnp.where(first, 0.0, x), axis=-1, keepdims=True)
    return jnp.where(first, s0, s1)


def _rwkv_kernel(r_ref, k_ref, v_ref, lo_ref, ga_ref, mu_ref,
                 w0_ref, w2_ref, a0_ref, a2_ref, g2_ref,
                 kk_ref, ka_ref, rk_ref, lnw_ref, lnb_ref, tri_ref,
                 o_ref, state_ref, carry_ref):
    nb, ts, _ = r_ref.shape
    c = RW_CHUNK
    n_chunks = ts // c

    @pl.when(pl.program_id(1) == 0)
    def _():
        state_ref[...] = jnp.zeros_like(state_ref)
        carry_ref[...] = jnp.zeros_like(carry_ref)

    t_i = lax.broadcasted_iota(jnp.int32, (c, LANES), 0)
    s_i = lax.broadcasted_iota(jnp.int32, (c, LANES), 1) & (RW_HEAD - 1)
    strict = t_i > s_i
    incl = t_i >= s_i
    eye = (t_i == s_i).astype(F32)
    same_sub = (t_i // RW_SUB) == (s_i // RW_SUB)
    rr = lax.broadcasted_iota(jnp.int32, (LANES, LANES), 0)
    cc = lax.broadcasted_iota(jnp.int32, (LANES, LANES), 1)
    bd_mask = (rr // RW_HEAD) == (cc // RW_HEAD)

    def bd(x):
        return jnp.where(bd_mask, jnp.concatenate([x, x], axis=0), 0.0).astype(BF16)

    def pmul(a, bmat):
        return _dot(a, bd(bmat))

    tri = tri_ref[...]
    pre, pc, d, ys = {}, {}, {}, {}

    gl = RW_GROUP * c
    n_groups = ts // gl
    last_raw = {}

    def prologue(b, g):
        rs = slice(g * gl, (g + 1) * gl)
        rkv = []
        for n, ref in enumerate((r_ref, k_ref, v_ref)):
            raw = ref[b, rs, :].astype(F32)
            carry = carry_ref[b, n:n + 1, :] if g == 0 else last_raw[b, n]
            rkv.append(_shift_mix(raw, carry, mu_ref[n:n + 1, :]))
            last_raw[b, n] = raw[gl - 1:gl, :]
            if g == n_groups - 1:
                carry_ref[b, n:n + 1, :] = last_raw[b, n]
        r, k, v = rkv
        lo = lo_ref[b, rs, :]
        wl = w0_ref[...] + _dot(lo[:, 0:RW_DECAY_RANK], w2_ref[...])
        lw = (-LOG2E * EXP_M_HALF) * _sigmoid(wl)
        a = _sigmoid(a0_ref[...] + _dot(lo[:, RW_DECAY_RANK:LANES], a2_ref[...]))
        gate = _dot(lo[:, LANES:LANES + RW_GATE_RANK], g2_ref[...])
        kk = k * kk_ref[...]
        kkn = kk * lax.rsqrt(jnp.maximum(_pair_sum(kk * kk), 1e-24))
        kmod = k * (1.0 + (a - 1.0) * ka_ref[...])
        pre[b, g] = dict(r=r, v=v, kmod=kmod, g=gate, lw=lw, kkn=kkn, beta=kkn * a)

    def rows(name, it):
        b, ch = it
        lc = ch % RW_GROUP
        return pre[b, ch // RW_GROUP][name][lc * c:(lc + 1) * c]

    def prep(it):
        lw_c = rows("lw", it)
        cum = _sums2(tri, lw_c)
        c_end = cum[c - 1:c, :]
        e_neg = jnp.exp2(-cum)
        e_end = jnp.exp2(c_end - cum)
        beta_c, kmod_c = rows("beta", it), rows("kmod", it)
        pc[it] = dict(
            rt=rows("r", it) * jnp.exp2(cum),
            at=-rows("kkn", it) * jnp.exp2(cum - lw_c),
            bt=beta_c * e_neg, kt=kmod_c * e_neg,
            bh=(beta_c * e_end).astype(BF16), kh=(kmod_c * e_end).astype(BF16),
            w_end=jnp.exp2(c_end))

    def s_amat(its):
        for it in its:
            d["lhs2", it] = jnp.concatenate([pc[it]["at"], pc[it]["rt"]], axis=0).astype(BF16)
            d["amat", it] = _dot_nt(d["lhs2", it], jnp.concatenate(
                [bd(pc[it]["bt"]), bd(pc[it]["kt"])], axis=0))

    def s_split(its):
        for it in its:
            am = d["amat", it]
            n_ab = jnp.where(strict, am[0:c, 0:LANES], 0.0)
            d["a_rb", it] = jnp.where(incl, am[c:2 * c, 0:LANES], 0.0).astype(BF16)
            avk = _dot(jnp.concatenate([jnp.where(strict, am[0:c, LANES:2 * LANES], 0.0),
                                        jnp.where(incl, am[c:2 * c, LANES:2 * LANES], 0.0)], axis=0),
                       bd(rows("v", it)))
            d["u0", it], d["yv", it] = avk[0:c], avk[c:2 * c]
            d["dg", it] = jnp.where(same_sub, n_ab, 0.0)
            d["off", it] = (n_ab - d["dg", it]).astype(BF16)

    def s_d2(its):
        for it in its:
            d["d2", it] = pmul(d["dg", it], d["dg", it])
            d["x", it] = eye + d["dg", it]

    def s_x(pw, nxt):
        def stage(its):
            for it in its:
                if nxt:
                    both = pmul(jnp.concatenate([d["x", it], d[pw, it]], axis=0), d[pw, it])
                    d["x", it] = d["x", it] + both[0:c]
                    d[nxt, it] = both[c:2 * c]
                else:
                    d["x", it] = d["x", it] + pmul(d["x", it], d[pw, it])
        return stage

    def s_r(its):
        for it in its:
            d["r", it] = pmul(d["off", it], d["x", it])

    def s_r2(its):
        for it in its:
            both = pmul(jnp.concatenate([d["x", it], d["r", it]], axis=0), d["r", it])
            d["z", it] = d["x", it] + both[0:c]
            d["r2", it] = both[c:2 * c]

    def s_tinv(its):
        for it in its:
            d["t_inv", it] = (d["z", it] + pmul(d["z", it], d["r2", it])).astype(BF16)

    def s_au(its):
        for it in its:
            au = _dot(d["t_inv", it], jnp.concatenate([bd(pc[it]["at"]), bd(d["u0", it])], axis=1))
            d["ahat", it], d["uu", it] = au[:, 0:LANES], au[:, LANES:2 * LANES]

    def s_trans(its):
        for it in its:
            lhs = jnp.concatenate(
                [jnp.concatenate([d["ahat", it], d["uu", it]], axis=1),
                 jnp.concatenate([jnp.zeros((c, LANES), BF16), rows("v", it).astype(BF16)], axis=1)],
                axis=0)
            mg = _dot_tn(lhs, jnp.concatenate([pc[it]["bh"], pc[it]["kh"]], axis=0))
            d["mmat", it] = jnp.where(bd_mask, mg[0:LANES], 0.0).astype(BF16)
            d["gmat", it] = jnp.where(bd_mask, mg[LANES:2 * LANES], 0.0)
            ru = _dot(d["a_rb", it], jnp.concatenate([bd(d["ahat", it]), bd(d["uu", it])], axis=1))
            d["rhat", it] = (pc[it]["rt"] + ru[:, 0:LANES]).astype(BF16)
            d["y0", it] = ru[:, LANES:2 * LANES] + d["yv", it]

    stages = [s_amat, s_split, s_d2, s_x("d2", "d4"), s_x("d4", "d8"), s_x("d8", None),
              s_r, s_r2, s_tinv, s_au, s_trans]

    def chain_step(bs, ch, st):
        st_bf = {b: st[b].astype(BF16) for b in bs}
        for b in bs:
            ys[b, ch] = d["y0", (b, ch)] + _dot_nt(d["rhat", (b, ch)], st_bf[b])
        for b in bs:
            st[b] = (st[b] * pc[b, ch]["w_end"] + _dot(st_bf[b], d["mmat", (b, ch)])
                     + d["gmat", (b, ch)])

    def epilogue(b, g):
        rs = slice(g * gl, (g + 1) * gl)
        y = jnp.concatenate([ys[b, ch] for ch in range(g * RW_GROUP, (g + 1) * RW_GROUP)], axis=0)
        pb = pre[b, g]
        mean = _pair_sum(y) * (1.0 / RW_HEAD)
        dlt = y - mean
        var = _pair_sum(dlt * dlt) * (1.0 / RW_HEAD)
        on = dlt * lax.rsqrt(var + RW_LN_EPS) * lnw_ref[...] + lnb_ref[...]
        bonus = _pair_sum(pb["r"] * pb["kmod"] * rk_ref[...]) * pb["v"]
        o_ref[b, rs, :] = (((on + bonus) * pb["g"])
                           * _sigmoid(ga_ref[b, rs, :].astype(F32))).astype(o_ref.dtype)

    batches = list(range(nb))
    group_items = [[(b, ch) for ch in range(g * RW_GROUP, (g + 1) * RW_GROUP) for b in batches]
                   for g in range(n_groups)]
    st = {}

    def prep_units(g):
        return ([functools.partial(prologue, b, g) for b in batches]
                + [functools.partial(prep, it) for it in group_items[g]])

    def tail_units(g):
        units = [functools.partial(chain_step, batches, ch, st)
                 for ch in range(g * RW_GROUP, (g + 1) * RW_GROUP)]
        return units + [functools.partial(epilogue, b, g) for b in batches]

    def interleave(stage_items, units):
        per = -(-len(units) // len(stages)) if units else 0
        for n, stage in enumerate(stages):
            stage(stage_items)
            for u in units[n * per:(n + 1) * per]:
                u()

    for b in batches:
        st[b] = state_ref[b]
    for u in prep_units(0):
        u()
    for g in range(n_groups):
        units = (prep_units(g + 1) if g + 1 < n_groups else []) + (tail_units(g - 1) if g > 0 else [])
        interleave(group_items[g], units)
    for u in tail_units(n_groups - 1):
        u()
    for b in batches:
        state_ref[b] = st[b]


def _rwkv(cols3, lo_feat, p):
    B, S, _ = cols3.shape
    ts = _pick(S, (2 * RW_GROUP * RW_CHUNK, RW_GROUP * RW_CHUNK))
    nt = S // ts
    n_hp = RW_WIDTH // LANES
    cb = lambda col: col // LANES

    def colspec(col0):
        return pl.BlockSpec((B, ts, LANES), lambda h, t: (0, t, cb(col0) + h))

    vec = pl.BlockSpec((1, LANES), lambda h, t: (0, h))
    mat = lambda nrows: pl.BlockSpec((nrows, LANES), lambda h, t: (0, h))
    full = lambda shape: pl.BlockSpec(shape, lambda h, t: (0, 0))
    tri = jnp.asarray(np.tril(np.ones((RW_CHUNK, RW_CHUNK), np.float32)), dtype=BF16)
    return pl.pallas_call(
        _rwkv_kernel,
        grid=(n_hp, nt),
        in_specs=[
            colspec(COL_R), colspec(COL_K), colspec(COL_V),
            pl.BlockSpec((B, ts, RW_LO_FEAT), lambda h, t: (0, t, 0)),
            colspec(COL_GA),
            mat(3),
            vec, mat(RW_DECAY_RANK), vec, mat(RW_ICLR_RANK), mat(RW_GATE_RANK),
            vec, vec, vec, vec, vec,
            full(tri.shape),
        ],
        out_specs=pl.BlockSpec((B, ts, LANES), lambda h, t: (0, t, h)),
        out_shape=jax.ShapeDtypeStruct((B, S, RW_WIDTH), BF16),
        scratch_shapes=[pltpu.VMEM((B, LANES, LANES), F32), pltpu.VMEM((B, 8, LANES), F32)],
        compiler_params=_cparams(("parallel", "arbitrary")),
        name="rwkv7",
    )(cols3, cols3, cols3, lo_feat, cols3, p["mu_rkv"],
      p["w0"], p["w2"], p["a0"], p["a2"], p["g2"],
      p["k_k"], p["k_a"], p["r_k"], p["ln_w"], p["ln_b"], tri)


def _hg_level_matrix():
    c = HG_CHUNK
    m = np.zeros((HG_LEVELS + 1, c, c), np.float32)
    for l in range(HG_LEVELS):
        bs = c >> l
        half = bs // 2
        for t in range(c):
            mid = (t // bs) * bs + half
            if t % bs >= half:
                m[l, t, mid:t + 1] = 1.0
            else:
                m[l, t, t + 1:mid] = 1.0
    m[HG_LEVELS] = np.tril(np.ones((c, c), np.float32))
    return m.reshape((HG_LEVELS + 1) * c, c)


def _hgrn_kernel(q_ref, f_ref, i_ref, g_ref, gb_ref, lbl_ref, ng_ref, lvl_ref,
                 o_ref, state_ref):
    nb, ts, _ = q_ref.shape
    c = HG_CHUNK
    gl = HG_GROUP * c
    n_groups = ts // gl

    @pl.when(pl.program_id(1) == 0)
    def _():
        state_ref[...] = jnp.zeros_like(state_ref)

    logits = lbl_ref[...]
    mx = jnp.max(logits, axis=0, keepdims=True)
    ex = jnp.exp(logits - mx)
    lb = ex[0:1, :] / jnp.sum(ex, axis=0, keepdims=True)

    ri = lax.broadcasted_iota(jnp.int32, (c, c), 0)
    ci = lax.broadcasted_iota(jnp.int32, (c, c), 1)
    rowi = lax.broadcasted_iota(jnp.int32, (c, LANES), 0)
    lvl = lvl_ref[...]
    ng = ng_ref[...]
    second, valid = [], []
    for l in range(HG_LEVELS):
        bs = c >> l
        second.append((rowi & (bs - 1)) >= (bs // 2))
        valid.append(((ri // bs) == (ci // bs)) & ((ri & (bs - 1)) >= (bs // 2))
                     & ((ci & (bs - 1)) < (bs // 2)))

    pair = (ri == ci + 1) & ((ri & 1) == 1)
    pre, d, outs = {}, {}, {}

    def prologue(b, g):
        rs = slice(g * gl, (g + 1) * gl)
        f = lb + (1.0 - lb) * _sigmoid(f_ref[b, rs, :].astype(F32))
        qraw = q_ref[b, rs, :].astype(F32)
        pre[b, g] = dict(f=f, lf=LOG2E * jnp.log(f), kx=1.0 - f, qs=qraw * _sigmoid(qraw),
                         iv=i_ref[b, rs, :])

    def rows(name, it):
        b, ch = it
        lc = ch % HG_GROUP
        return pre[b, ch // HG_GROUP][name][lc * c:(lc + 1) * c]

    def s_part(its):
        for it in its:
            d["part", it] = _sums2(lvl, rows("lf", it))
            q_c, k_c = rows("qs", it), rows("kx", it)
            dsum = jnp.sum(q_c * k_c, axis=-1, keepdims=True)
            psum = jnp.sum(q_c * rows("f", it) * pltpu.roll(k_c, 1, axis=0), axis=-1, keepdims=True)
            d["sc", it] = jnp.where(ri == ci, dsum, jnp.where(pair, psum, 0.0))

    def s_level(l):
        def stage(its):
            for it in its:
                qk = (jnp.where(second[l], rows("qs", it), rows("kx", it))
                      * jnp.exp2(d["part", it][l * c:(l + 1) * c])).astype(BF16)
                d["sc", it] = jnp.where(valid[l], _dot_nt(qk, qk), d["sc", it])
        return stage

    def s_out(its):
        for it in its:
            bcum = d["part", it][HG_LEVELS * c:(HG_LEVELS + 1) * c]
            b_end = bcum[c - 1:c, :]
            d["o_intra", it] = _dot(d["sc", it], rows("iv", it))
            d["zc", it] = _dot_tn(rows("iv", it), rows("kx", it) * jnp.exp2(b_end - bcum))
            d["qb", it] = rows("qs", it) * jnp.exp2(bcum)
            d["dec", it] = jnp.exp2(b_end)

    stages = [s_part] + [s_level(l) for l in range(HG_LEVELS)] + [s_out]

    def tail(b, g, st):
        rs = slice(g * gl, (g + 1) * gl)
        os_ = []
        for ch in range(g * HG_GROUP, (g + 1) * HG_GROUP):
            it = (b, ch)
            os_.append(d["o_intra", it] + _dot_nt(d["qb", it], st[b]))
            st[b] = st[b] * d["dec", it] + d["zc", it]
        o = jnp.concatenate(os_, axis=0)
        o = o * lax.rsqrt(jnp.mean(o * o, axis=-1, keepdims=True) + NORM_EPS) * ng
        graw = g_ref[b, rs, :].astype(F32)
        o_ref[b, rs, :] = (o * (graw * _sigmoid(graw))
                           * _sigmoid(gb_ref[b, rs, :].astype(F32))).astype(o_ref.dtype)

    batches = list(range(nb))
    group_items = [[(b, ch) for ch in range(g * HG_GROUP, (g + 1) * HG_GROUP) for b in batches]
                   for g in range(n_groups)]
    st = {b: state_ref[b] for b in batches}

    def interleave(stage_items, units):
        per = -(-len(units) // len(stages)) if units else 0
        for n, stage in enumerate(stages):
            stage(stage_items)
            for u in units[n * per:(n + 1) * per]:
                u()

    for b in batches:
        prologue(b, 0)
    for g in range(n_groups):
        units = ([functools.partial(prologue, b, g + 1) for b in batches] if g + 1 < n_groups else [])
        units += ([functools.partial(tail, b, g - 1, st) for b in batches] if g > 0 else [])
        interleave(group_items[g], units)
    for b in batches:
        tail(b, n_groups - 1, st)
        state_ref[b] = st[b]


def _hgrn(cols3, lb_logits, norm_g):
    B, S, _ = cols3.shape
    ts = _pick(S, (2 * HG_GROUP * HG_CHUNK, HG_GROUP * HG_CHUNK))
    nt = S // ts
    cb = lambda col: col // LANES

    def colspec(col0):
        return pl.BlockSpec((B, ts, LANES), lambda h, t: (0, t, cb(col0) + h))

    n_slots = lb_logits.shape[0]
    lvl = jnp.asarray(_hg_level_matrix(), dtype=BF16)
    return pl.pallas_call(
        _hgrn_kernel,
        grid=(HG_HEADS, nt),
        in_specs=[
            colspec(COL_HQ), colspec(COL_HF), colspec(COL_HI), colspec(COL_HG), colspec(COL_GB),
            pl.BlockSpec((n_slots, LANES), lambda h, t: (0, h)),
            pl.BlockSpec((1, LANES), lambda h, t: (0, 0)),
            pl.BlockSpec(lvl.shape, lambda h, t: (0, 0)),
        ],
        out_specs=pl.BlockSpec((B, ts, LANES), lambda h, t: (0, t, h)),
        out_shape=jax.ShapeDtypeStruct((B, S, D_MODEL), BF16),
        scratch_shapes=[pltpu.VMEM((B, HG_KEY, HG_KEY), F32)],
        compiler_params=_cparams(("parallel", "arbitrary")),
        name="hgrn2",
    )(cols3, cols3, cols3, cols3, cols3, lb_logits, norm_g, lvl)


def _memkv_kernel(m_ref, g_ref, wk_ref, wv_ref, k_ref, v_ref):
    m = _rms(m_ref[...], g_ref[...]).astype(BF16)
    k_ref[...] = jnp.dot(m, wk_ref[...], preferred_element_type=F32).astype(BF16)
    v_ref[...] = jnp.dot(m, wv_ref[...], preferred_element_type=F32).astype(BF16)


def _memkv(mem2, g, wk, wv, B, n_mem):
    row = pl.BlockSpec((n_mem, D_MODEL), lambda b: (b, 0))
    wspec = pl.BlockSpec((D_MODEL, D_MODEL), lambda b: (0, 0))
    sds = jax.ShapeDtypeStruct((B * n_mem, D_MODEL), BF16)
    return pl.pallas_call(
        _memkv_kernel,
        grid=(B,),
        in_specs=[row, pl.BlockSpec((1, D_MODEL), lambda b: (0, 0)), wspec, wspec],
        out_specs=[row, row],
        out_shape=[sds, sds],
        compiler_params=_cparams(("parallel",)),
        name="memkv",
    )(mem2, g, wk, wv)


def _tail_kernel(x_ref, ya_ref, yb_ref, wout_ref, gx_ref, wq_ref, k_ref, v_ref, wo_ref,
                 gf_ref, w1_ref, w3_ref, w2_ref, gfin_ref, o_ref):
    h = x_ref[...] + _dot(ya_ref[...].astype(F32) + yb_ref[...].astype(F32), wout_ref[...])
    u = _rms(h, gx_ref[...]).astype(BF16)
    q = jnp.dot(u, wq_ref[...], preferred_element_type=F32)
    k = k_ref[...]
    v = v_ref[...]
    heads = []
    for hd in range(XA_HEADS):
        sl = slice(hd * XA_HEAD, (hd + 1) * XA_HEAD)
        s = _dot_nt(q[:, sl], k[:, sl]) * (XA_HEAD ** -0.5)
        s = s - jnp.max(s, axis=-1, keepdims=True)
        e = jnp.exp(s)
        p = e / jnp.sum(e, axis=-1, keepdims=True)
        heads.append(_dot(p, v[:, sl]))
    h = h + _dot(jnp.concatenate(heads, axis=-1), wo_ref[...])
    u = _rms(h, gf_ref[...]).astype(BF16)
    a = jnp.dot(u, w1_ref[...], preferred_element_type=F32)
    b = jnp.dot(u, w3_ref[...], preferred_element_type=F32)
    mid = (a * _sigmoid(a)) * b
    o_ref[...] = _rms(h + _dot(mid, w2_ref[...]), gfin_ref[...])


def _tail(x2, ya, yb, wout, gx, wq, km, vm, wo, gf, w1, w3, w2, gfin, B, S, n_mem):
    tm = _pick(S, (512, 256))
    nt = S // tm
    row = pl.BlockSpec((tm, D_MODEL), lambda b, t: (b * nt + t, 0))
    vec = pl.BlockSpec((1, D_MODEL), lambda b, t: (0, 0))
    kv = pl.BlockSpec((n_mem, D_MODEL), lambda b, t: (b, 0))
    resident = lambda shape: pl.BlockSpec(shape, lambda b, t: (0, 0), pipeline_mode=pl.Buffered(1))
    sq = resident((D_MODEL, D_MODEL))
    return pl.pallas_call(
        _tail_kernel,
        grid=(B, nt),
        in_specs=[row, row, row, sq, vec, sq, kv, kv, sq,
                  vec, resident((D_MODEL, D_FF)), resident((D_MODEL, D_FF)),
                  resident((D_FF, D_MODEL)), vec],
        out_specs=row,
        out_shape=jax.ShapeDtypeStruct((B * S, D_MODEL), F32),
        compiler_params=_cparams(("parallel", "parallel")),
        name="tail",
    )(x2, ya, yb, wout, gx, wq, km, vm, wo, gf, w1, w3, w2, gfin)


def _pack_mu_lo(mu):
    lo = mu[3 * RW_WIDTH:]
    return jnp.concatenate([lo, jnp.zeros((RW_LO_PAD - RW_LO,), mu.dtype)])[None, :].astype(F32)


def kernel(x, mem, norm_mix_g, w_in, rw_mu, rw_w0, rw_w2, rw_a0, rw_a2, rw_g2, rw_k_k, rw_k_a, rw_r_k, rw_ln_w, rw_ln_b, hg_lb_logits, hg_norm_g, w_out, norm_xa_g, norm_mem_g, xa_wq, xa_wk, xa_wv, xa_wo, norm_ffn_g, ffn_w1, ffn_w3, ffn_w2, norm_final_g):
    B, S, _ = x.shape
    n_mem = mem.shape[1]
    depth = w_in.shape[0]
    assert depth == 1, "single-layer block"
    assert S % (RW_GROUP * RW_CHUNK) == 0 and S % (HG_GROUP * HG_CHUNK) == 0
    l = 0
    T = B * S
    row = lambda a: a.reshape(1, -1).astype(F32)

    x2 = x.reshape(T, D_MODEL)
    mu = rw_mu[l].astype(F32)
    cols, lo_feat = _inproj(x2, row(norm_mix_g[l]), jnp.swapaxes(w_in[l], 0, 1).astype(BF16),
                            _pack_mu_lo(mu), S)
    cols3 = cols.reshape(B, S, N_COLS)
    lo_feat = lo_feat.reshape(B, S, RW_LO_FEAT)
    rw = dict(
        mu_rkv=mu[:3 * RW_WIDTH].reshape(3, RW_WIDTH),
        w0=row(rw_w0[l]), w2=rw_w2[l].astype(BF16), a0=row(rw_a0[l]), a2=rw_a2[l].astype(BF16),
        g2=rw_g2[l].astype(BF16), k_k=row(rw_k_k[l]), k_a=row(rw_k_a[l]), r_k=row(rw_r_k[l]),
        ln_w=row(rw_ln_w[l]), ln_b=row(rw_ln_b[l]))
    ya = _rwkv(cols3, lo_feat, rw).reshape(T, D_MODEL)
    yb = _hgrn(cols3, hg_lb_logits.astype(F32), row(hg_norm_g[l])).reshape(T, D_MODEL)

    km, vm = _memkv(mem.reshape(B * n_mem, D_MODEL), row(norm_mem_g[l]),
                    xa_wk[l].astype(BF16), xa_wv[l].astype(BF16), B, n_mem)
    out = _tail(x2, ya, yb, w_out[l].astype(BF16), row(norm_xa_g[l]), xa_wq[l].astype(BF16), km, vm,
                xa_wo[l].astype(BF16), row(norm_ffn_g[l]), ffn_w1[l].astype(BF16),
                ffn_w3[l].astype(BF16), ffn_w2[l].astype(BF16), row(norm_final_g), B, S, n_mem)
    return out.reshape(B, S, D_MODEL)
```

```python
import functools

import numpy as np
import jax
import jax.numpy as jnp
from jax import lax
from jax.experimental import pallas as pl
from jax.experimental.pallas import tpu as pltpu

F32 = jnp.float32
BF16 = jnp.bfloat16

D_MODEL = 1024
NORM_EPS = 1e-6
LANES = 128

RW_HEAD = 64
RW_WIDTH = D_MODEL
RW_DECAY_RANK = 64
RW_ICLR_RANK = 64
RW_GATE_RANK = 160
RW_LN_EPS = 64e-5
RW_LO = RW_DECAY_RANK + RW_ICLR_RANK + RW_GATE_RANK
RW_LO_PAD = 512
RW_LO_FEAT = 384
RW_CHUNK = 64
RW_SUB = 16
RW_GROUP = 4
LOG2E = 1.4426950408889634
EXP_M_HALF = 0.6065306597126334

HG_KEY = 128
HG_HEADS = D_MODEL // HG_KEY
HG_CHUNK = 64
HG_LEVELS = 5
HG_GROUP = 4

XA_HEADS = 4
XA_HEAD = D_MODEL // XA_HEADS
D_FF = 2816

COL_R = 0
COL_K = COL_R + RW_WIDTH
COL_V = COL_K + RW_WIDTH
COL_LO = COL_V + RW_WIDTH
COL_HQ = COL_LO + RW_LO_PAD
COL_HF = COL_HQ + D_MODEL
COL_HI = COL_HF + D_MODEL
COL_HG = COL_HI + D_MODEL
COL_GA = COL_HG + D_MODEL
COL_GB = COL_GA + D_MODEL
N_COLS = COL_GB + D_MODEL

COLS_DTYPE = BF16
PAD_SHIFT = RW_LO_PAD - RW_LO
IN_COL_CHUNKS = ((0, COL_HQ), (COL_HQ, COL_HQ + 2048), (COL_HQ + 2048, COL_HQ + 4096),
                 (COL_HQ + 4096, N_COLS))

VMEM_LIMIT = 56 * 1024 * 1024


def _cparams(sem):
    return pltpu.CompilerParams(dimension_semantics=sem, vmem_limit_bytes=VMEM_LIMIT)


def _sigmoid(x):
    return 1.0 / (1.0 + jnp.exp(-x))


def _rms(x, g):
    ms = jnp.mean(x * x, axis=-1, keepdims=True)
    return x * lax.rsqrt(ms + NORM_EPS) * g


def _dot(a, b):
    return jnp.dot(a.astype(BF16), b.astype(BF16), preferred_element_type=F32)


def _dot_nt(a, b):
    return lax.dot_general(a.astype(BF16), b.astype(BF16), (((1,), (1,)), ((), ())),
                           preferred_element_type=F32)


def _dot_tn(a, b):
    return lax.dot_general(a.astype(BF16), b.astype(BF16), (((0,), (0,)), ((), ())),
                           preferred_element_type=F32)


def _split2(x):
    hi = x.astype(BF16)
    mid = (x - hi.astype(F32)).astype(BF16)
    return jnp.concatenate([hi, mid], axis=1)


def _sums2(pattern_bf16, x):
    y = jnp.dot(pattern_bf16, _split2(x), preferred_element_type=F32)
    return y[:, 0:LANES] + y[:, LANES:2 * LANES]


def _pick(n, prefs):
    for p in prefs:
        if n % p == 0:
            return p
    return n


def _shift_mix(raw, carry_row, mu):
    rolled = pltpu.roll(raw, 1, axis=0)
    row = lax.broadcasted_iota(jnp.int32, raw.shape, 0)
    prev = jnp.where(row == 0, carry_row, rolled)
    return raw + mu * (prev - raw)


def _inproj_kernel(tiles_per_seq, x_ref, g_ref, w_ref, mu_ref, o_ref, lo_ref, carry_ref):
    tm = x_ref.shape[0]
    u = _rms(x_ref[...], g_ref[...]).astype(BF16)

    @pl.when(pl.program_id(0) % tiles_per_seq == 0)
    def _():
        carry_ref[...] = jnp.zeros_like(carry_ref)

    for c0, c1 in IN_COL_CHUNKS:
        r0 = c0 if c0 < COL_HQ else c0 - PAD_SHIFT
        res = lax.dot_general(u, w_ref[r0:r0 + (c1 - c0), :], (((1,), (1,)), ((), ())),
                              preferred_element_type=F32)
        o_ref[:, c0:c1] = res.astype(o_ref.dtype)
        if c0 <= COL_LO and COL_LO + RW_LO_PAD <= c1:
            raw = res[:, COL_LO - c0:COL_LO - c0 + RW_LO_PAD]
            lo = _shift_mix(raw, carry_ref[0:1, :], mu_ref[...])
            carry_ref[0:1, :] = raw[tm - 1:tm, :]
            lane = lax.broadcasted_iota(jnp.int32, (tm, LANES), 1)
            lo_ref[:, 0:LANES] = jnp.where(lane < RW_DECAY_RANK, jnp.tanh(lo[:, 0:LANES]),
                                           lo[:, 0:LANES]).astype(lo_ref.dtype)
            lo_ref[:, LANES:RW_LO_FEAT] = _sigmoid(lo[:, LANES:RW_LO_FEAT]).astype(lo_ref.dtype)


def _inproj(x2, g, w_t, mu_lo, S):
    T = x2.shape[0]
    tm = _pick(S, (512, 256))
    return pl.pallas_call(
        functools.partial(_inproj_kernel, S // tm),
        grid=(T // tm,),
        in_specs=[
            pl.BlockSpec((tm, D_MODEL), lambda i: (i, 0)),
            pl.BlockSpec((1, D_MODEL), lambda i: (0, 0)),
            pl.BlockSpec(w_t.shape, lambda i: (0, 0), pipeline_mode=pl.Buffered(1)),
            pl.BlockSpec((1, RW_LO_PAD), lambda i: (0, 0)),
        ],
        out_specs=[pl.BlockSpec((tm, N_COLS), lambda i: (i, 0)),
                   pl.BlockSpec((tm, RW_LO_FEAT), lambda i: (i, 0))],
        out_shape=[jax.ShapeDtypeStruct((T, N_COLS), COLS_DTYPE),
                   jax.ShapeDtypeStruct((T, RW_LO_FEAT), BF16)],
        scratch_shapes=[pltpu.VMEM((8, RW_LO_PAD), F32)],
        compiler_params=_cparams(("arbitrary",)),
        name="inproj",
    )(x2, g, w_t, mu_lo)


def _pair_sum(x):
    lane = lax.broadcasted_iota(jnp.int32, x.shape, 1)
    first = lane < RW_HEAD
    s0 = jnp.sum(jnp.where(first, x, 0.0), axis=-1, keepdims=True)
    s1 = jnp.sum(jnp.where(first, 0.0, x), axis=-1, keepdims=True)
    return jnp.where(first, s0, s1)


def _rwkv_kernel(r_ref, k_ref, v_ref, lo_ref, ga_ref, mu_ref,
                 w0_ref, w2_ref, a0_ref, a2_ref, g2_ref,
                 kk_ref, ka_ref, rk_ref, lnw_ref, lnb_ref, tri_ref,
                 o_ref, state_ref, carry_ref):
    nb, ts, _ = r_ref.shape
    c = RW_CHUNK
    n_chunks = ts // c

    @pl.when(pl.program_id(1) == 0)
    def _():
        state_ref[...] = jnp.zeros_like(state_ref)
        carry_ref[...] = jnp.zeros_like(carry_ref)

    t_i = lax.broadcasted_iota(jnp.int32, (c, LANES), 0)
    s_i = lax.broadcasted_iota(jnp.int32, (c, LANES), 1) & (RW_HEAD - 1)
    strict = t_i > s_i
    incl = t_i >= s_i
    eye = (t_i == s_i).astype(F32)
    same_sub = (t_i // RW_SUB) == (s_i // RW_SUB)
    rr = lax.broadcasted_iota(jnp.int32, (LANES, LANES), 0)
    cc = lax.broadcasted_iota(jnp.int32, (LANES, LANES), 1)
    bd_mask = (rr // RW_HEAD) == (cc // RW_HEAD)

    def bd(x):
        return jnp.where(bd_mask, jnp.concatenate([x, x], axis=0), 0.0).astype(BF16)

    def pmul(a, bmat):
        return _dot(a, bd(bmat))

    tri = tri_ref[...]
    pre, pc, d, ys = {}, {}, {}, {}

    gl = RW_GROUP * c
    n_groups = ts // gl
    last_raw = {}

    def prologue(b, g):
        rs = slice(g * gl, (g + 1) * gl)
        rkv = []
        for n, ref in enumerate((r_ref, k_ref, v_ref)):
            raw = ref[b, rs, :].astype(F32)
            carry = carry_ref[b, n:n + 1, :] if g == 0 else last_raw[b, n]
            rkv.append(_shift_mix(raw, carry, mu_ref[n:n + 1, :]))
            last_raw[b, n] = raw[gl - 1:gl, :]
            if g == n_groups - 1:
                carry_ref[b, n:n + 1, :] = last_raw[b, n]
        r, k, v = rkv
        lo = lo_ref[b, rs, :]
        wl = w0_ref[...] + _dot(lo[:, 0:RW_DECAY_RANK], w2_ref[...])
        lw = (-LOG2E * EXP_M_HALF) * _sigmoid(wl)
        a = _sigmoid(a0_ref[...] + _dot(lo[:, RW_DECAY_RANK:LANES], a2_ref[...]))
        gate = _dot(lo[:, LANES:LANES + RW_GATE_RANK], g2_ref[...])
        kk = k * kk_ref[...]
        kkn = kk * lax.rsqrt(jnp.maximum(_pair_sum(kk * kk), 1e-24))
        kmod = k * (1.0 + (a - 1.0) * ka_ref[...])
        pre[b, g] = dict(r=r, v=v, kmod=kmod, g=gate, lw=lw, kkn=kkn, beta=kkn * a)

    def rows(name, it):
        b, ch = it
        lc = ch % RW_GROUP
        return pre[b, ch // RW_GROUP][name][lc * c:(lc + 1) * c]

    def prep(it):
        lw_c = rows("lw", it)
        cum = _sums2(tri, lw_c)
        c_end = cum[c - 1:c, :]
        e_neg = jnp.exp2(-cum)
        e_end = jnp.exp2(c_end - cum)
        beta_c, kmod_c = rows("beta", it), rows("kmod", it)
        pc[it] = dict(
            rt=rows("r", it) * jnp.exp2(cum),
            at=-rows("kkn", it) * jnp.exp2(cum - lw_c),
            bt=beta_c * e_neg, kt=kmod_c * e_neg,
            bh=(beta_c * e_end).astype(BF16), kh=(kmod_c * e_end).astype(BF16),
            w_end=jnp.exp2(c_end))

    def s_amat(its):
        for it in its:
            d["lhs2", it] = jnp.concatenate([pc[it]["at"], pc[it]["rt"]], axis=0).astype(BF16)
            d["amat", it] = _dot_nt(d["lhs2", it], jnp.concatenate(
                [bd(pc[it]["bt"]), bd(pc[it]["kt"])], axis=0))

    def s_split(its):
        for it in its:
            am = d["amat", it]
            n_ab = jnp.where(strict, am[0:c, 0:LANES], 0.0)
            d["a_rb", it] = jnp.where(incl, am[c:2 * c, 0:LANES], 0.0).astype(BF16)
            avk = _dot(jnp.concatenate([jnp.where(strict, am[0:c, LANES:2 * LANES], 0.0),
                                        jnp.where(incl, am[c:2 * c, LANES:2 * LANES], 0.0)], axis=0),
                       bd(rows("v", it)))
            d["u0", it], d["yv", it] = avk[0:c], avk[c:2 * c]
            d["dg", it] = jnp.where(same_sub, n_ab, 0.0)
            d["off", it] = (n_ab - d["dg", it]).astype(BF16)

    def s_d2(its):
        for it in its:
            d["d2", it] = pmul(d["dg", it], d["dg", it])
            d["x", it] = eye + d["dg", it]

    def s_x(pw, nxt):
        def stage(its):
            for it in its:
                if nxt:
                    both = pmul(jnp.concatenate([d["x", it], d[pw, it]], axis=0), d[pw, it])
                    d["x", it] = d["x", it] + both[0:c]
                    d[nxt, it] = both[c:2 * c]
                else:
                    d["x", it] = d["x", it] + pmul(d["x", it], d[pw, it])
        return stage

    def s_r(its):
        for it in its:
            d["r", it] = pmul(d["off", it], d["x", it])

    def s_r2(its):
        for it in its:
            both = pmul(jnp.concatenate([d["x", it], d["r", it]], axis=0), d["r", it])
            d["z", it] = d["x", it] + both[0:c]
            d["r2", it] = both[c:2 * c]

    def s_tinv(its):
        for it in its:
            d["t_inv", it] = (d["z", it] + pmul(d["z", it], d["r2", it])).astype(BF16)

    def s_au(its):
        for it in its:
            au = _dot(d["t_inv", it], jnp.concatenate([bd(pc[it]["at"]), bd(d["u0", it])], axis=1))
            d["ahat", it], d["uu", it] = au[:, 0:LANES], au[:, LANES:2 * LANES]

    def s_trans(its):
        for it in its:
            lhs = jnp.concatenate(
                [jnp.concatenate([d["ahat", it], d["uu", it]], axis=1),
                 jnp.concatenate([jnp.zeros((c, LANES), BF16), rows("v", it).astype(BF16)], axis=1)],
                axis=0)
            mg = _dot_tn(lhs, jnp.concatenate([pc[it]["bh"], pc[it]["kh"]], axis=0))
            d["mmat", it] = jnp.where(bd_mask, mg[0:LANES], 0.0).astype(BF16)
            d["gmat", it] = jnp.where(bd_mask, mg[LANES:2 * LANES], 0.0)
            ru = _dot(d["a_rb", it], jnp.concatenate([bd(d["ahat", it]), bd(d["uu", it])], axis=1))
            d["rhat", it] = (pc[it]["rt"] + ru[:, 0:LANES]).astype(BF16)
            d["y0", it] = ru[:, LANES:2 * LANES] + d["yv", it]

    stages = [s_amat, s_split, s_d2, s_x("d2", "d4"), s_x("d4", "d8"), s_x("d8", None),
              s_r, s_r2, s_tinv, s_au, s_trans]

    def chain_step(bs, ch, st):
        st_bf = {b: st[b].astype(BF16) for b in bs}
        for b in bs:
            ys[b, ch] = d["y0", (b, ch)] + _dot_nt(d["rhat", (b, ch)], st_bf[b])
        for b in bs:
            st[b] = (st[b] * pc[b, ch]["w_end"] + _dot(st_bf[b], d["mmat", (b, ch)])
                     + d["gmat", (b, ch)])

    def epilogue(b, g):
        rs = slice(g * gl, (g + 1) * gl)
        y = jnp.concatenate([ys[b, ch] for ch in range(g * RW_GROUP, (g + 1) * RW_GROUP)], axis=0)
        pb = pre[b, g]
        mean = _pair_sum(y) * (1.0 / RW_HEAD)
        dlt = y - mean
        var = _pair_sum(dlt * dlt) * (1.0 / RW_HEAD)
        on = dlt * lax.rsqrt(var + RW_LN_EPS) * lnw_ref[...] + lnb_ref[...]
        bonus = _pair_sum(pb["r"] * pb["kmod"] * rk_ref[...]) * pb["v"]
        o_ref[b, rs, :] = (((on + bonus) * pb["g"])
                           * _sigmoid(ga_ref[b, rs, :].astype(F32))).astype(o_ref.dtype)

    batches = list(range(nb))
    group_items = [[(b, ch) for ch in range(g * RW_GROUP, (g + 1) * RW_GROUP) for b in batches]
                   for g in range(n_groups)]
    st = {}

    def prep_units(g):
        return ([functools.partial(prologue, b, g) for b in batches]
                + [functools.partial(prep, it) for it in group_items[g]])

    def tail_units(g):
        units = [functools.partial(chain_step, batches, ch, st)
                 for ch in range(g * RW_GROUP, (g + 1) * RW_GROUP)]
        return units + [functools.partial(epilogue, b, g) for b in batches]

    def interleave(stage_items, units):
        per = -(-len(units) // len(stages)) if units else 0
        for n, stage in enumerate(stages):
            stage(stage_items)
            for u in units[n * per:(n + 1) * per]:
                u()

    for b in batches:
        st[b] = state_ref[b]
    for u in prep_units(0):
        u()
    for g in range(n_groups):
        units = (prep_units(g + 1) if g + 1 < n_groups else []) + (tail_units(g - 1) if g > 0 else [])
        interleave(group_items[g], units)
    for u in tail_units(n_groups - 1):
        u()
    for b in batches:
        state_ref[b] = st[b]


def _rwkv(cols3, lo_feat, p):
    B, S, _ = cols3.shape
    ts = _pick(S, (2 * RW_GROUP * RW_CHUNK, RW_GROUP * RW_CHUNK))
    nt = S // ts
    n_hp = RW_WIDTH // LANES
    cb = lambda col: col // LANES

    def colspec(col0):
        return pl.BlockSpec((B, ts, LANES), lambda h, t: (0, t, cb(col0) + h))

    vec = pl.BlockSpec((1, LANES), lambda h, t: (0, h))
    mat = lambda nrows: pl.BlockSpec((nrows, LANES), lambda h, t: (0, h))
    full = lambda shape: pl.BlockSpec(shape, lambda h, t: (0, 0))
    tri = jnp.asarray(np.tril(np.ones((RW_CHUNK, RW_CHUNK), np.float32)), dtype=BF16)
    return pl.pallas_call(
        _rwkv_kernel,
        grid=(n_hp, nt),
        in_specs=[
            colspec(COL_R), colspec(COL_K), colspec(COL_V),
            pl.BlockSpec((B, ts, RW_LO_FEAT), lambda h, t: (0, t, 0)),
            colspec(COL_GA),
            mat(3),
            vec, mat(RW_DECAY_RANK), vec, mat(RW_ICLR_RANK), mat(RW_GATE_RANK),
            vec, vec, vec, vec, vec,
            full(tri.shape),
        ],
        out_specs=pl.BlockSpec((B, ts, LANES), lambda h, t: (0, t, h)),
        out_shape=jax.ShapeDtypeStruct((B, S, RW_WIDTH), BF16),
        scratch_shapes=[pltpu.VMEM((B, LANES, LANES), F32), pltpu.VMEM((B, 8, LANES), F32)],
        compiler_params=_cparams(("parallel", "arbitrary")),
        name="rwkv7",
    )(cols3, cols3, cols3, lo_feat, cols3, p["mu_rkv"],
      p["w0"], p["w2"], p["a0"], p["a2"], p["g2"],
      p["k_k"], p["k_a"], p["r_k"], p["ln_w"], p["ln_b"], tri)


def _hg_level_matrix():
    c = HG_CHUNK
    m = np.zeros((HG_LEVELS + 1, c, c), np.float32)
    for l in range(HG_LEVELS):
        bs = c >> l
        half = bs // 2
        for t in range(c):
            mid = (t // bs) * bs + half
            if t % bs >= half:
                m[l, t, mid:t + 1] = 1.0
            else:
                m[l, t, t + 1:mid] = 1.0
    m[HG_LEVELS] = np.tril(np.ones((c, c), np.float32))
    return m.reshape((HG_LEVELS + 1) * c, c)


def _hgrn_kernel(q_ref, f_ref, i_ref, g_ref, gb_ref, lbl_ref, ng_ref, lvl_ref,
                 o_ref, state_ref):
    nb, ts, _ = q_ref.shape
    c = HG_CHUNK
    gl = HG_GROUP * c
    n_groups = ts // gl

    @pl.when(pl.program_id(1) == 0)
    def _():
        state_ref[...] = jnp.zeros_like(state_ref)

    logits = lbl_ref[...]
    mx = jnp.max(logits, axis=0, keepdims=True)
    ex = jnp.exp(logits - mx)
    lb = ex[0:1, :] / jnp.sum(ex, axis=0, keepdims=True)

    ri = lax.broadcasted_iota(jnp.int32, (c, c), 0)
    ci = lax.broadcasted_iota(jnp.int32, (c, c), 1)
    rowi = lax.broadcasted_iota(jnp.int32, (c, LANES), 0)
    lvl = lvl_ref[...]
    ng = ng_ref[...]
    second, valid = [], []
    for l in range(HG_LEVELS):
        bs = c >> l
        second.append((rowi & (bs - 1)) >= (bs // 2))
        valid.append(((ri // bs) == (ci // bs)) & ((ri & (bs - 1)) >= (bs // 2))
                     & ((ci & (bs - 1)) < (bs // 2)))

    pair = (ri == ci + 1) & ((ri & 1) == 1)
    pre, d, outs = {}, {}, {}

    def prologue(b, g):
        rs = slice(g * gl, (g + 1) * gl)
        f = lb + (1.0 - lb) * _sigmoid(f_ref[b, rs, :].astype(F32))
        qraw = q_ref[b, rs, :].astype(F32)
        pre[b, g] = dict(f=f, lf=LOG2E * jnp.log(f), kx=1.0 - f, qs=qraw * _sigmoid(qraw),
                         iv=i_ref[b, rs, :])

    def rows(name, it):
        b, ch = it
        lc = ch % HG_GROUP
        return pre[b, ch // HG_GROUP][name][lc * c:(lc + 1) * c]

    def s_part(its):
        for it in its:
            d["part", it] = _sums2(lvl, rows("lf", it))
            q_c, k_c = rows("qs", it), rows("kx", it)
            dsum = jnp.sum(q_c * k_c, axis=-1, keepdims=True)
            psum = jnp.sum(q_c * rows("f", it) * pltpu.roll(k_c, 1, axis=0), axis=-1, keepdims=True)
            d["sc", it] = jnp.where(ri == ci, dsum, jnp.where(pair, psum, 0.0))

    def s_level(l):
        def stage(its):
            for it in its:
                qk = (jnp.where(second[l], rows("qs", it), rows("kx", it))
                      * jnp.exp2(d["part", it][l * c:(l + 1) * c])).astype(BF16)
                d["sc", it] = jnp.where(valid[l], _dot_nt(qk, qk), d["sc", it])
        return stage

    def s_out(its):
        for it in its:
            bcum = d["part", it][HG_LEVELS * c:(HG_LEVELS + 1) * c]
            b_end = bcum[c - 1:c, :]
            d["o_intra", it] = _dot(d["sc", it], rows("iv", it))
            d["zc", it] = _dot_tn(rows("iv", it), rows("kx", it) * jnp.exp2(b_end - bcum))
            d["qb", it] = rows("qs", it) * jnp.exp2(bcum)
            d["dec", it] = jnp.exp2(b_end)

    stages = [s_part] + [s_level(l) for l in range(HG_LEVELS)] + [s_out]

    def tail(b, g, st):
        rs = slice(g * gl, (g + 1) * gl)
        os_ = []
        for ch in range(g * HG_GROUP, (g + 1) * HG_GROUP):
            it = (b, ch)
            os_.append(d["o_intra", it] + _dot_nt(d["qb", it], st[b]))
            st[b] = st[b] * d["dec", it] + d["zc", it]
        o = jnp.concatenate(os_, axis=0)
        o = o * lax.rsqrt(jnp.mean(o * o, axis=-1, keepdims=True) + NORM_EPS) * ng
        graw = g_ref[b, rs, :].astype(F32)
        o_ref[b, rs, :] = (o * (graw * _sigmoid(graw))
                           * _sigmoid(gb_ref[b, rs, :].astype(F32))).astype(o_ref.dtype)

    batches = list(range(nb))
    group_items = [[(b, ch) for ch in range(g * HG_GROUP, (g + 1) * HG_GROUP) for b in batches]
                   for g in range(n_groups)]
    st = {b: state_ref[b] for b in batches}

    def interleave(stage_items, units):
        per = -(-len(units) // len(stages)) if units else 0
        for n, stage in enumerate(stages):
            stage(stage_items)
            for u in units[n * per:(n + 1) * per]:
                u()

    for b in batches:
        prologue(b, 0)
    for g in range(n_groups):
        units = ([functools.partial(prologue, b, g + 1) for b in batches] if g + 1 < n_groups else [])
        units += ([functools.partial(tail, b, g - 1, st) for b in batches] if g > 0 else [])
        interleave(group_items[g], units)
    for b in batches:
        tail(b, n_groups - 1, st)
        state_ref[b] = st[b]


def _hgrn(cols3, lb_logits, norm_g):
    B, S, _ = cols3.shape
    ts = _pick(S, (2 * HG_GROUP * HG_CHUNK, HG_GROUP * HG_CHUNK))
    nt = S // ts
    cb = lambda col: col // LANES

    def colspec(col0):
        return pl.BlockSpec((B, ts, LANES), lambda h, t: (0, t, cb(col0) + h))

    n_slots = lb_logits.shape[0]
    lvl = jnp.asarray(_hg_level_matrix(), dtype=BF16)
    return pl.pallas_call(
        _hgrn_kernel,
        grid=(HG_HEADS, nt),
        in_specs=[
            colspec(COL_HQ), colspec(COL_HF), colspec(COL_HI), colspec(COL_HG), colspec(COL_GB),
            pl.BlockSpec((n_slots, LANES), lambda h, t: (0, h)),
            pl.BlockSpec((1, LANES), lambda h, t: (0, 0)),
            pl.BlockSpec(lvl.shape, lambda h, t: (0, 0)),
        ],
        out_specs=pl.BlockSpec((B, ts, LANES), lambda h, t: (0, t, h)),
        out_shape=jax.ShapeDtypeStruct((B, S, D_MODEL), BF16),
        scratch_shapes=[pltpu.VMEM((B, HG_KEY, HG_KEY), F32)],
        compiler_params=_cparams(("parallel", "arbitrary")),
        name="hgrn2",
    )(cols3, cols3, cols3, cols3, cols3, lb_logits, norm_g, lvl)


def _memkv_kernel(m_ref, g_ref, wk_ref, wv_ref, k_ref, v_ref):
    m = _rms(m_ref[...], g_ref[...]).astype(BF16)
    k_ref[...] = jnp.dot(m, wk_ref[...], preferred_element_type=F32).astype(BF16)
    v_ref[...] = jnp.dot(m, wv_ref[...], preferred_element_type=F32).astype(BF16)


def _memkv(mem2, g, wk, wv, B, n_mem):
    row = pl.BlockSpec((n_mem, D_MODEL), lambda b: (b, 0))
    wspec = pl.BlockSpec((D_MODEL, D_MODEL), lambda b: (0, 0))
    sds = jax.ShapeDtypeStruct((B * n_mem, D_MODEL), BF16)
    return pl.pallas_call(
        _memkv_kernel,
        grid=(B,),
        in_specs=[row, pl.BlockSpec((1, D_MODEL), lambda b: (0, 0)), wspec, wspec],
        out_specs=[row, row],
        out_shape=[sds, sds],
        compiler_params=_cparams(("parallel",)),
        name="memkv",
    )(mem2, g, wk, wv)


def _tail_kernel(x_ref, ya_ref, yb_ref, wout_ref, gx_ref, wq_ref, k_ref, v_ref, wo_ref,
                 gf_ref, w1_ref, w3_ref, w2_ref, gfin_ref, o_ref):
    h = x_ref[...] + _dot(ya_ref[...].astype(F32) + yb_ref[...].astype(F32), wout_ref[...])
    u = _rms(h, gx_ref[...]).astype(BF16)
    q = jnp.dot(u, wq_ref[...], preferred_element_type=F32)
    k = k_ref[...]
    v = v_ref[...]
    sls = [slice(hd * XA_HEAD, (hd + 1) * XA_HEAD) for hd in range(XA_HEADS)]
    sc, pr, heads = {}, {}, {}

    def scores(hd):
        sc[hd] = _dot_nt(q[:, sls[hd]], k[:, sls[hd]]) * (XA_HEAD ** -0.5)

    def softmax(hd):
        s = sc[hd] - jnp.max(sc[hd], axis=-1, keepdims=True)
        e = jnp.exp(s)
        pr[hd] = e / jnp.sum(e, axis=-1, keepdims=True)

    def values(hd):
        heads[hd] = _dot(pr[hd], v[:, sls[hd]])

    for step in range(XA_HEADS + 2):
        if step < XA_HEADS:
            scores(step)
        if 0 <= step - 1 < XA_HEADS:
            softmax(step - 1)
        if 0 <= step - 2 < XA_HEADS:
            values(step - 2)
    h = h + _dot(jnp.concatenate([heads[hd] for hd in range(XA_HEADS)], axis=-1), wo_ref[...])
    u = _rms(h, gf_ref[...]).astype(BF16)
    a = jnp.dot(u, w1_ref[...], preferred_element_type=F32)
    b = jnp.dot(u, w3_ref[...], preferred_element_type=F32)
    mid = (a * _sigmoid(a)) * b
    o_ref[...] = _rms(h + _dot(mid, w2_ref[...]), gfin_ref[...])


def _tail(x2, ya, yb, wout, gx, wq, km, vm, wo, gf, w1, w3, w2, gfin, B, S, n_mem):
    tm = _pick(S, (512, 256))
    nt = S // tm
    row = pl.BlockSpec((tm, D_MODEL), lambda b, t: (b * nt + t, 0))
    vec = pl.BlockSpec((1, D_MODEL), lambda b, t: (0, 0))
    kv = pl.BlockSpec((n_mem, D_MODEL), lambda b, t: (b, 0))
    resident = lambda shape: pl.BlockSpec(shape, lambda b, t: (0, 0), pipeline_mode=pl.Buffered(1))
    sq = resident((D_MODEL, D_MODEL))
    return pl.pallas_call(
        _tail_kernel,
        grid=(B, nt),
        in_specs=[row, row, row, sq, vec, sq, kv, kv, sq,
                  vec, resident((D_MODEL, D_FF)), resident((D_MODEL, D_FF)),
                  resident((D_FF, D_MODEL)), vec],
        out_specs=row,
        out_shape=jax.ShapeDtypeStruct((B * S, D_MODEL), F32),
        compiler_params=_cparams(("parallel", "parallel")),
        name="tail",
    )(x2, ya, yb, wout, gx, wq, km, vm, wo, gf, w1, w3, w2, gfin)


def _pack_mu_lo(mu):
    lo = mu[3 * RW_WIDTH:]
    return jnp.concatenate([lo, jnp.zeros((RW_LO_PAD - RW_LO,), mu.dtype)])[None, :].astype(F32)


def kernel(x, mem, norm_mix_g, w_in, rw_mu, rw_w0, rw_w2, rw_a0, rw_a2, rw_g2, rw_k_k, rw_k_a, rw_r_k, rw_ln_w, rw_ln_b, hg_lb_logits, hg_norm_g, w_out, norm_xa_g, norm_mem_g, xa_wq, xa_wk, xa_wv, xa_wo, norm_ffn_g, ffn_w1, ffn_w3, ffn_w2, norm_final_g):
    B, S, _ = x.shape
    n_mem = mem.shape[1]
    depth = w_in.shape[0]
    assert depth == 1, "single-layer block"
    assert S % (RW_GROUP * RW_CHUNK) == 0 and S % (HG_GROUP * HG_CHUNK) == 0
    l = 0
    T = B * S
    row = lambda a: a.reshape(1, -1).astype(F32)

    x2 = x.reshape(T, D_MODEL)
    mu = rw_mu[l].astype(F32)
    cols, lo_feat = _inproj(x2, row(norm_mix_g[l]), jnp.swapaxes(w_in[l], 0, 1).astype(BF16),
                            _pack_mu_lo(mu), S)
    cols3 = cols.reshape(B, S, N_COLS)
    lo_feat = lo_feat.reshape(B, S, RW_LO_FEAT)
    rw = dict(
        mu_rkv=mu[:3 * RW_WIDTH].reshape(3, RW_WIDTH),
        w0=row(rw_w0[l]), w2=rw_w2[l].astype(BF16), a0=row(rw_a0[l]), a2=rw_a2[l].astype(BF16),
        g2=rw_g2[l].astype(BF16), k_k=row(rw_k_k[l]), k_a=row(rw_k_a[l]), r_k=row(rw_r_k[l]),
        ln_w=row(rw_ln_w[l]), ln_b=row(rw_ln_b[l]))
    ya = _rwkv(cols3, lo_feat, rw).reshape(T, D_MODEL)
    yb = _hgrn(cols3, hg_lb_logits.astype(F32), row(hg_norm_g[l])).reshape(T, D_MODEL)

    km, vm = _memkv(mem.reshape(B * n_mem, D_MODEL), row(norm_mem_g[l]),
                    xa_wk[l].astype(BF16), xa_wv[l].astype(BF16), B, n_mem)
    out = _tail(x2, ya, yb, w_out[l].astype(BF16), row(norm_xa_g[l]), xa_wq[l].astype(BF16), km, vm,
                xa_wo[l].astype(BF16), row(norm_ffn_g[l]), ffn_w1[l].astype(BF16),
                ffn_w3[l].astype(BF16), ffn_w2[l].astype(BF16), row(norm_final_g), B, S, n_mem)
    return out.reshape(B, S, D_MODEL)
```

```python
import functools

import numpy as np
import jax
import jax.numpy as jnp
from jax import lax
from jax.experimental import pallas as pl
from jax.experimental.pallas import tpu as pltpu

F32 = jnp.float32
BF16 = jnp.bfloat16

D_MODEL = 1024
NORM_EPS = 1e-6
LANES = 128

RW_HEAD = 64
RW_WIDTH = D_MODEL
RW_DECAY_RANK = 64
RW_ICLR_RANK = 64
RW_GATE_RANK = 160
RW_LN_EPS = 64e-5
RW_LO = RW_DECAY_RANK + RW_ICLR_RANK + RW_GATE_RANK
RW_LO_PAD = 512
RW_LO_FEAT = 384
RW_CHUNK = 64
RW_SUB = 16
RW_GROUP = 4
LOG2E = 1.4426950408889634
EXP_M_HALF = 0.6065306597126334

HG_KEY = 128
HG_HEADS = D_MODEL // HG_KEY
HG_CHUNK = 64
HG_LEVELS = 5
HG_GROUP = 4

XA_HEADS = 4
XA_HEAD = D_MODEL // XA_HEADS
D_FF = 2816

COL_R = 0
COL_K = COL_R + RW_WIDTH
COL_V = COL_K + RW_WIDTH
COL_LO = COL_V + RW_WIDTH
COL_HQ = COL_LO + RW_LO_PAD
COL_HF = COL_HQ + D_MODEL
COL_HI = COL_HF + D_MODEL
COL_HG = COL_HI + D_MODEL
COL_GA = COL_HG + D_MODEL
COL_GB = COL_GA + D_MODEL
N_COLS = COL_GB + D_MODEL

COLS_DTYPE = BF16
PAD_SHIFT = RW_LO_PAD - RW_LO
IN_COL_CHUNKS = ((0, COL_HQ), (COL_HQ, COL_HQ + 2048), (COL_HQ + 2048, COL_HQ + 4096),
                 (COL_HQ + 4096, N_COLS))

VMEM_LIMIT = 56 * 1024 * 1024


def _cparams(sem):
    return pltpu.CompilerParams(dimension_semantics=sem, vmem_limit_bytes=VMEM_LIMIT)


def _sigmoid(x):
    return 1.0 / (1.0 + jnp.exp(-x))


def _rms(x, g):
    ms = jnp.mean(x * x, axis=-1, keepdims=True)
    return x * lax.rsqrt(ms + NORM_EPS) * g


def _dot(a, b):
    return jnp.dot(a.astype(BF16), b.astype(BF16), preferred_element_type=F32)


def _dot_nt(a, b):
    return lax.dot_general(a.astype(BF16), b.astype(BF16), (((1,), (1,)), ((), ())),
                           preferred_element_type=F32)


def _dot_tn(a, b):
    return lax.dot_general(a.astype(BF16), b.astype(BF16), (((0,), (0,)), ((), ())),
                           preferred_element_type=F32)


def _split2(x):
    hi = x.astype(BF16)
    mid = (x - hi.astype(F32)).astype(BF16)
    return jnp.concatenate([hi, mid], axis=1)


def _sums2(pattern_bf16, x):
    y = jnp.dot(pattern_bf16, _split2(x), preferred_element_type=F32)
    return y[:, 0:LANES] + y[:, LANES:2 * LANES]


def _pick(n, prefs):
    for p in prefs:
        if n % p == 0:
            return p
    return n


def _shift_mix(raw, carry_row, mu):
    rolled = pltpu.roll(raw, 1, axis=0)
    row = lax.broadcasted_iota(jnp.int32, raw.shape, 0)
    prev = jnp.where(row == 0, carry_row, rolled)
    return raw + mu * (prev - raw)


def _inproj_kernel(tiles_per_seq, x_ref, xn_ref, g_ref, w_ref, mu_ref, o_ref, lo_ref, carry_ref, u_ref):
    tm = x_ref.shape[0]
    i = pl.program_id(0)
    cur = i % 2

    @pl.when(i == 0)
    def _():
        u_ref[0] = _rms(x_ref[...], g_ref[...]).astype(BF16)

    u = u_ref[cur]

    @pl.when(pl.program_id(0) % tiles_per_seq == 0)
    def _():
        carry_ref[...] = jnp.zeros_like(carry_ref)

    for c0, c1 in IN_COL_CHUNKS:
        r0 = c0 if c0 < COL_HQ else c0 - PAD_SHIFT
        res = lax.dot_general(u, w_ref[r0:r0 + (c1 - c0), :], (((1,), (1,)), ((), ())),
                              preferred_element_type=F32)
        o_ref[:, c0:c1] = res.astype(o_ref.dtype)
        if c0 == 0:
            u_ref[1 - cur] = _rms(xn_ref[...], g_ref[...]).astype(BF16)
        if c0 <= COL_LO and COL_LO + RW_LO_PAD <= c1:
            raw = res[:, COL_LO - c0:COL_LO - c0 + RW_LO_PAD]
            lo = _shift_mix(raw, carry_ref[0:1, :], mu_ref[...])
            carry_ref[0:1, :] = raw[tm - 1:tm, :]
            lane = lax.broadcasted_iota(jnp.int32, (tm, LANES), 1)
            lo_ref[:, 0:LANES] = jnp.where(lane < RW_DECAY_RANK, jnp.tanh(lo[:, 0:LANES]),
                                           lo[:, 0:LANES]).astype(lo_ref.dtype)
            lo_ref[:, LANES:RW_LO_FEAT] = _sigmoid(lo[:, LANES:RW_LO_FEAT]).astype(lo_ref.dtype)


def _inproj(x2, g, w_t, mu_lo, S):
    T = x2.shape[0]
    tm = _pick(S, (512, 256))
    n_tiles = T // tm
    return pl.pallas_call(
        functools.partial(_inproj_kernel, S // tm),
        grid=(T // tm,),
        in_specs=[
            pl.BlockSpec((tm, D_MODEL), lambda i: (i, 0)),
            pl.BlockSpec((tm, D_MODEL), lambda i: (jnp.minimum(i + 1, n_tiles - 1), 0)),
            pl.BlockSpec((1, D_MODEL), lambda i: (0, 0)),
            pl.BlockSpec(w_t.shape, lambda i: (0, 0), pipeline_mode=pl.Buffered(1)),
            pl.BlockSpec((1, RW_LO_PAD), lambda i: (0, 0)),
        ],
        out_specs=[pl.BlockSpec((tm, N_COLS), lambda i: (i, 0)),
                   pl.BlockSpec((tm, RW_LO_FEAT), lambda i: (i, 0))],
        out_shape=[jax.ShapeDtypeStruct((T, N_COLS), COLS_DTYPE),
                   jax.ShapeDtypeStruct((T, RW_LO_FEAT), BF16)],
        scratch_shapes=[pltpu.VMEM((8, RW_LO_PAD), F32), pltpu.VMEM((2, tm, D_MODEL), BF16)],
        compiler_params=_cparams(("arbitrary",)),
        name="inproj",
    )(x2, x2, g, w_t, mu_lo)


def _pair_sum(x):
    lane = lax.broadcasted_iota(jnp.int32, x.shape, 1)
    first = lane < RW_HEAD
    s0 = jnp.sum(jnp.where(first, x, 0.0), axis=-1, keepdims=True)
    s1 = jnp.sum(jnp.where(first, 0.0, x), axis=-1, keepdims=True)
    return jnp.where(first, s0, s1)


def _rwkv_kernel(r_ref, k_ref, v_ref, lo_ref, ga_ref, mu_ref,
                 w0_ref, w2_ref, a0_ref, a2_ref, g2_ref,
                 kk_ref, ka_ref, rk_ref, lnw_ref, lnb_ref, tri_ref,
                 o_ref, state_ref, carry_ref):
    nb, ts, _ = r_ref.shape
    c = RW_CHUNK
    n_chunks = ts // c

    @pl.when(pl.program_id(1) == 0)
    def _():
        state_ref[...] = jnp.zeros_like(state_ref)
        carry_ref[...] = jnp.zeros_like(carry_ref)

    t_i = lax.broadcasted_iota(jnp.int32, (c, LANES), 0)
    s_i = lax.broadcasted_iota(jnp.int32, (c, LANES), 1) & (RW_HEAD - 1)
    strict = t_i > s_i
    incl = t_i >= s_i
    eye = (t_i == s_i).astype(F32)
    same_sub = (t_i // RW_SUB) == (s_i // RW_SUB)
    rr = lax.broadcasted_iota(jnp.int32, (LANES, LANES), 0)
    cc = lax.broadcasted_iota(jnp.int32, (LANES, LANES), 1)
    bd_mask = (rr // RW_HEAD) == (cc // RW_HEAD)

    def bd(x):
        return jnp.where(bd_mask, jnp.concatenate([x, x], axis=0), 0.0).astype(BF16)

    def pmul(a, bmat):
        return _dot(a, bd(bmat))

    tri = tri_ref[...]
    pre, pc, d, ys = {}, {}, {}, {}

    gl = RW_GROUP * c
    n_groups = ts // gl
    last_raw = {}

    def prologue(b, g):
        rs = slice(g * gl, (g + 1) * gl)
        rkv = []
        for n, ref in enumerate((r_ref, k_ref, v_ref)):
            raw = ref[b, rs, :].astype(F32)
            carry = carry_ref[b, n:n + 1, :] if g == 0 else last_raw[b, n]
            rkv.append(_shift_mix(raw, carry, mu_ref[n:n + 1, :]))
            last_raw[b, n] = raw[gl - 1:gl, :]
            if g == n_groups - 1:
                carry_ref[b, n:n + 1, :] = last_raw[b, n]
        r, k, v = rkv
        lo = lo_ref[b, rs, :]
        wl = w0_ref[...] + _dot(lo[:, 0:RW_DECAY_RANK], w2_ref[...])
        lw = (-LOG2E * EXP_M_HALF) * _sigmoid(wl)
        a = _sigmoid(a0_ref[...] + _dot(lo[:, RW_DECAY_RANK:LANES], a2_ref[...]))
        gate = _dot(lo[:, LANES:LANES + RW_GATE_RANK], g2_ref[...])
        kk = k * kk_ref[...]
        kkn = kk * lax.rsqrt(jnp.maximum(_pair_sum(kk * kk), 1e-24))
        kmod = k * (1.0 + (a - 1.0) * ka_ref[...])
        pre[b, g] = dict(r=r, v=v, kmod=kmod, g=gate, lw=lw, kkn=kkn, beta=kkn * a)

    def rows(name, it):
        b, ch = it
        lc = ch % RW_GROUP
        return pre[b, ch // RW_GROUP][name][lc * c:(lc + 1) * c]

    def prep(it):
        lw_c = rows("lw", it)
        cum = _sums2(tri, lw_c)
        c_end = cum[c - 1:c, :]
        e_neg = jnp.exp2(-cum)
        e_end = jnp.exp2(c_end - cum)
        beta_c, kmod_c = rows("beta", it), rows("kmod", it)
        pc[it] = dict(
            rt=rows("r", it) * jnp.exp2(cum),
            at=-rows("kkn", it) * jnp.exp2(cum - lw_c),
            bt=beta_c * e_neg, kt=kmod_c * e_neg,
            bh=(beta_c * e_end).astype(BF16), kh=(kmod_c * e_end).astype(BF16),
            w_end=jnp.exp2(c_end))

    def s_amat(its):
        for it in its:
            d["lhs2", it] = jnp.concatenate([pc[it]["at"], pc[it]["rt"]], axis=0).astype(BF16)
            d["amat", it] = _dot_nt(d["lhs2", it], jnp.concatenate(
                [bd(pc[it]["bt"]), bd(pc[it]["kt"])], axis=0))

    def s_split(its):
        for it in its:
            am = d["amat", it]
            n_ab = jnp.where(strict, am[0:c, 0:LANES], 0.0)
            d["a_rb", it] = jnp.where(incl, am[c:2 * c, 0:LANES], 0.0).astype(BF16)
            avk = _dot(jnp.concatenate([jnp.where(strict, am[0:c, LANES:2 * LANES], 0.0),
                                        jnp.where(incl, am[c:2 * c, LANES:2 * LANES], 0.0)], axis=0),
                       bd(rows("v", it)))
            d["u0", it], d["yv", it] = avk[0:c], avk[c:2 * c]
            d["dg", it] = jnp.where(same_sub, n_ab, 0.0)
            d["off", it] = (n_ab - d["dg", it]).astype(BF16)

    def s_d2(its):
        for it in its:
            d["d2", it] = pmul(d["dg", it], d["dg", it])
            d["x", it] = eye + d["dg", it]

    def s_x(pw, nxt):
        def stage(its):
            for it in its:
                if nxt:
                    both = pmul(jnp.concatenate([d["x", it], d[pw, it]], axis=0), d[pw, it])
                    d["x", it] = d["x", it] + both[0:c]
                    d[nxt, it] = both[c:2 * c]
                else:
                    d["x", it] = d["x", it] + pmul(d["x", it], d[pw, it])
        return stage

    def s_r(its):
        for it in its:
            d["r", it] = pmul(d["off", it], d["x", it])

    def s_r2(its):
        for it in its:
            both = pmul(jnp.concatenate([d["x", it], d["r", it]], axis=0), d["r", it])
            d["z", it] = d["x", it] + both[0:c]
            d["r2", it] = both[c:2 * c]

    def s_tinv(its):
        for it in its:
            d["t_inv", it] = (d["z", it] + pmul(d["z", it], d["r2", it])).astype(BF16)

    def s_au(its):
        for it in its:
            au = _dot(d["t_inv", it], jnp.concatenate([bd(pc[it]["at"]), bd(d["u0", it])], axis=1))
            d["ahat", it], d["uu", it] = au[:, 0:LANES], au[:, LANES:2 * LANES]

    def s_trans(its):
        for it in its:
            lhs = jnp.concatenate(
                [jnp.concatenate([d["ahat", it], d["uu", it]], axis=1),
                 jnp.concatenate([jnp.zeros((c, LANES), BF16), rows("v", it).astype(BF16)], axis=1)],
                axis=0)
            mg = _dot_tn(lhs, jnp.concatenate([pc[it]["bh"], pc[it]["kh"]], axis=0))
            d["mmat", it] = jnp.where(bd_mask, mg[0:LANES], 0.0).astype(BF16)
            d["gmat", it] = jnp.where(bd_mask, mg[LANES:2 * LANES], 0.0)
            ru = _dot(d["a_rb", it], jnp.concatenate([bd(d["ahat", it]), bd(d["uu", it])], axis=1))
            d["rhat", it] = (pc[it]["rt"] + ru[:, 0:LANES]).astype(BF16)
            d["y0", it] = ru[:, LANES:2 * LANES] + d["yv", it]

    stages = [s_amat, s_split, s_d2, s_x("d2", "d4"), s_x("d4", "d8"), s_x("d8", None),
              s_r, s_r2, s_tinv, s_au, s_trans]

    def chain_step(bs, ch, st):
        st_bf = {b: st[b].astype(BF16) for b in bs}
        for b in bs:
            ys[b, ch] = d["y0", (b, ch)] + _dot_nt(d["rhat", (b, ch)], st_bf[b])
        for b in bs:
            st[b] = (st[b] * pc[b, ch]["w_end"] + _dot(st_bf[b], d["mmat", (b, ch)])
                     + d["gmat", (b, ch)])

    def epilogue(b, g):
        rs = slice(g * gl, (g + 1) * gl)
        y = jnp.concatenate([ys[b, ch] for ch in range(g * RW_GROUP, (g + 1) * RW_GROUP)], axis=0)
        pb = pre[b, g]
        mean = _pair_sum(y) * (1.0 / RW_HEAD)
        dlt = y - mean
        var = _pair_sum(dlt * dlt) * (1.0 / RW_HEAD)
        on = dlt * lax.rsqrt(var + RW_LN_EPS) * lnw_ref[...] + lnb_ref[...]
        bonus = _pair_sum(pb["r"] * pb["kmod"] * rk_ref[...]) * pb["v"]
        o_ref[b, rs, :] = (((on + bonus) * pb["g"])
                           * _sigmoid(ga_ref[b, rs, :].astype(F32))).astype(o_ref.dtype)

    batches = list(range(nb))
    group_items = [[(b, ch) for ch in range(g * RW_GROUP, (g + 1) * RW_GROUP) for b in batches]
                   for g in range(n_groups)]
    st = {}

    def prep_units(g):
        return ([functools.partial(prologue, b, g) for b in batches]
                + [functools.partial(prep, it) for it in group_items[g]])

    def tail_units(g):
        units = [functools.partial(chain_step, batches, ch, st)
                 for ch in range(g * RW_GROUP, (g + 1) * RW_GROUP)]
        return units + [functools.partial(epilogue, b, g) for b in batches]

    def interleave(stage_items, units):
        per = -(-len(units) // len(stages)) if units else 0
        for n, stage in enumerate(stages):
            stage(stage_items)
            for u in units[n * per:(n + 1) * per]:
                u()

    for b in batches:
        st[b] = state_ref[b]
    for u in prep_units(0):
        u()
    for g in range(n_groups):
        units = (prep_units(g + 1) if g + 1 < n_groups else []) + (tail_units(g - 1) if g > 0 else [])
        interleave(group_items[g], units)
    for u in tail_units(n_groups - 1):
        u()
    for b in batches:
        state_ref[b] = st[b]


def _rwkv(cols3, lo_feat, p):
    B, S, _ = cols3.shape
    ts = _pick(S, (2 * RW_GROUP * RW_CHUNK, RW_GROUP * RW_CHUNK))
    nt = S // ts
    n_hp = RW_WIDTH // LANES
    cb = lambda col: col // LANES

    def colspec(col0):
        return pl.BlockSpec((B, ts, LANES), lambda h, t: (0, t, cb(col0) + h))

    vec = pl.BlockSpec((1, LANES), lambda h, t: (0, h))
    mat = lambda nrows: pl.BlockSpec((nrows, LANES), lambda h, t: (0, h))
    full = lambda shape: pl.BlockSpec(shape, lambda h, t: (0, 0))
    tri = jnp.asarray(np.tril(np.ones((RW_CHUNK, RW_CHUNK), np.float32)), dtype=BF16)
    return pl.pallas_call(
        _rwkv_kernel,
        grid=(n_hp, nt),
        in_specs=[
            colspec(COL_R), colspec(COL_K), colspec(COL_V),
            pl.BlockSpec((B, ts, RW_LO_FEAT), lambda h, t: (0, t, 0)),
            colspec(COL_GA),
            mat(3),
            vec, mat(RW_DECAY_RANK), vec, mat(RW_ICLR_RANK), mat(RW_GATE_RANK),
            vec, vec, vec, vec, vec,
            full(tri.shape),
        ],
        out_specs=pl.BlockSpec((B, ts, LANES), lambda h, t: (0, t, h)),
        out_shape=jax.ShapeDtypeStruct((B, S, RW_WIDTH), BF16),
        scratch_shapes=[pltpu.VMEM((B, LANES, LANES), F32), pltpu.VMEM((B, 8, LANES), F32)],
        compiler_params=_cparams(("parallel", "arbitrary")),
        name="rwkv7",
    )(cols3, cols3, cols3, lo_feat, cols3, p["mu_rkv"],
      p["w0"], p["w2"], p["a0"], p["a2"], p["g2"],
      p["k_k"], p["k_a"], p["r_k"], p["ln_w"], p["ln_b"], tri)


def _hg_level_matrix():
    c = HG_CHUNK
    m = np.zeros((HG_LEVELS + 1, c, c), np.float32)
    for l in range(HG_LEVELS):
        bs = c >> l
        half = bs // 2
        for t in range(c):
            mid = (t // bs) * bs + half
            if t % bs >= half:
                m[l, t, mid:t + 1] = 1.0
            else:
                m[l, t, t + 1:mid] = 1.0
    m[HG_LEVELS] = np.tril(np.ones((c, c), np.float32))
    return m.reshape((HG_LEVELS + 1) * c, c)


def _hgrn_kernel(q_ref, f_ref, i_ref, g_ref, gb_ref, lbl_ref, ng_ref, lvl_ref,
                 o_ref, state_ref):
    nb, ts, _ = q_ref.shape
    c = HG_CHUNK
    gl = HG_GROUP * c
    n_groups = ts // gl

    @pl.when(pl.program_id(1) == 0)
    def _():
        state_ref[...] = jnp.zeros_like(state_ref)

    logits = lbl_ref[...]
    mx = jnp.max(logits, axis=0, keepdims=True)
    ex = jnp.exp(logits - mx)
    lb = ex[0:1, :] / jnp.sum(ex, axis=0, keepdims=True)

    ri = lax.broadcasted_iota(jnp.int32, (c, c), 0)
    ci = lax.broadcasted_iota(jnp.int32, (c, c), 1)
    rowi = lax.broadcasted_iota(jnp.int32, (c, LANES), 0)
    lvl = lvl_ref[...]
    ng = ng_ref[...]
    second, valid = [], []
    for l in range(HG_LEVELS):
        bs = c >> l
        second.append((rowi & (bs - 1)) >= (bs // 2))
        valid.append(((ri // bs) == (ci // bs)) & ((ri & (bs - 1)) >= (bs // 2))
                     & ((ci & (bs - 1)) < (bs // 2)))

    pair = (ri == ci + 1) & ((ri & 1) == 1)
    pre, d, outs = {}, {}, {}

    def prologue(b, g):
        rs = slice(g * gl, (g + 1) * gl)
        f = lb + (1.0 - lb) * _sigmoid(f_ref[b, rs, :].astype(F32))
        qraw = q_ref[b, rs, :].astype(F32)
        pre[b, g] = dict(f=f, lf=LOG2E * jnp.log(f), kx=1.0 - f, qs=qraw * _sigmoid(qraw),
                         iv=i_ref[b, rs, :])

    def rows(name, it):
        b, ch = it
        lc = ch % HG_GROUP
        return pre[b, ch // HG_GROUP][name][lc * c:(lc + 1) * c]

    def s_part(its):
        for it in its:
            d["part", it] = _sums2(lvl, rows("lf", it))
            q_c, k_c = rows("qs", it), rows("kx", it)
            dsum = jnp.sum(q_c * k_c, axis=-1, keepdims=True)
            psum = jnp.sum(q_c * rows("f", it) * pltpu.roll(k_c, 1, axis=0), axis=-1, keepdims=True)
            d["sc", it] = jnp.where(ri == ci, dsum, jnp.where(pair, psum, 0.0))

    def s_level(l):
        def stage(its):
            for it in its:
                qk = (jnp.where(second[l], rows("qs", it), rows("kx", it))
                      * jnp.exp2(d["part", it][l * c:(l + 1) * c])).astype(BF16)
                d["sc", it] = jnp.where(valid[l], _dot_nt(qk, qk), d["sc", it])
        return stage

    def s_out(its):
        for it in its:
            bcum = d["part", it][HG_LEVELS * c:(HG_LEVELS + 1) * c]
            b_end = bcum[c - 1:c, :]
            d["o_intra", it] = _dot(d["sc", it], rows("iv", it))
            d["zc", it] = _dot_tn(rows("iv", it), rows("kx", it) * jnp.exp2(b_end - bcum))
            d["qb", it] = rows("qs", it) * jnp.exp2(bcum)
            d["dec", it] = jnp.exp2(b_end)

    stages = [s_part] + [s_level(l) for l in range(HG_LEVELS)] + [s_out]

    def tail(b, g, st):
        rs = slice(g * gl, (g + 1) * gl)
        os_ = []
        for ch in range(g * HG_GROUP, (g + 1) * HG_GROUP):
            it = (b, ch)
            os_.append(d["o_intra", it] + _dot_nt(d["qb", it], st[b]))
            st[b] = st[b] * d["dec", it] + d["zc", it]
        o = jnp.concatenate(os_, axis=0)
        o = o * lax.rsqrt(jnp.mean(o * o, axis=-1, keepdims=True) + NORM_EPS) * ng
        graw = g_ref[b, rs, :].astype(F32)
        o_ref[b, rs, :] = (o * (graw * _sigmoid(graw))
                           * _sigmoid(gb_ref[b, rs, :].astype(F32))).astype(o_ref.dtype)

    batches = list(range(nb))
    group_items = [[(b, ch) for ch in range(g * HG_GROUP, (g + 1) * HG_GROUP) for b in batches]
                   for g in range(n_groups)]
    st = {b: state_ref[b] for b in batches}

    def interleave(stage_items, units):
        per = -(-len(units) // len(stages)) if units else 0
        for n, stage in enumerate(stages):
            stage(stage_items)
            for u in units[n * per:(n + 1) * per]:
                u()

    for b in batches:
        prologue(b, 0)
    for g in range(n_groups):
        units = ([functools.partial(prologue, b, g + 1) for b in batches] if g + 1 < n_groups else [])
        units += ([functools.partial(tail, b, g - 1, st) for b in batches] if g > 0 else [])
        interleave(group_items[g], units)
    for b in batches:
        tail(b, n_groups - 1, st)
        state_ref[b] = st[b]


def _hgrn(cols3, lb_logits, norm_g):
    B, S, _ = cols3.shape
    ts = _pick(S, (2 * HG_GROUP * HG_CHUNK, HG_GROUP * HG_CHUNK))
    nt = S // ts
    cb = lambda col: col // LANES

    def colspec(col0):
        return pl.BlockSpec((B, ts, LANES), lambda h, t: (0, t, cb(col0) + h))

    n_slots = lb_logits.shape[0]
    lvl = jnp.asarray(_hg_level_matrix(), dtype=BF16)
    return pl.pallas_call(
        _hgrn_kernel,
        grid=(HG_HEADS, nt),
        in_specs=[
            colspec(COL_HQ), colspec(COL_HF), colspec(COL_HI), colspec(COL_HG), colspec(COL_GB),
            pl.BlockSpec((n_slots, LANES), lambda h, t: (0, h)),
            pl.BlockSpec((1, LANES), lambda h, t: (0, 0)),
            pl.BlockSpec(lvl.shape, lambda h, t: (0, 0)),
        ],
        out_specs=pl.BlockSpec((B, ts, LANES), lambda h, t: (0, t, h)),
        out_shape=jax.ShapeDtypeStruct((B, S, D_MODEL), BF16),
        scratch_shapes=[pltpu.VMEM((B, HG_KEY, HG_KEY), F32)],
        compiler_params=_cparams(("parallel", "arbitrary")),
        name="hgrn2",
    )(cols3, cols3, cols3, cols3, cols3, lb_logits, norm_g, lvl)


def _memkv_kernel(m_ref, g_ref, wk_ref, wv_ref, k_ref, v_ref):
    m = _rms(m_ref[...], g_ref[...]).astype(BF16)
    k_ref[...] = jnp.dot(m, wk_ref[...], preferred_element_type=F32).astype(BF16)
    v_ref[...] = jnp.dot(m, wv_ref[...], preferred_element_type=F32).astype(BF16)


def _memkv(mem2, g, wk, wv, B, n_mem):
    row = pl.BlockSpec((n_mem, D_MODEL), lambda b: (b, 0))
    wspec = pl.BlockSpec((D_MODEL, D_MODEL), lambda b: (0, 0))
    sds = jax.ShapeDtypeStruct((B * n_mem, D_MODEL), BF16)
    return pl.pallas_call(
        _memkv_kernel,
        grid=(B,),
        in_specs=[row, pl.BlockSpec((1, D_MODEL), lambda b: (0, 0)), wspec, wspec],
        out_specs=[row, row],
        out_shape=[sds, sds],
        compiler_params=_cparams(("parallel",)),
        name="memkv",
    )(mem2, g, wk, wv)


def _tail_kernel(x_ref, ya_ref, yb_ref, wout_ref, gx_ref, wq_ref, k_ref, v_ref, wo_ref,
                 gf_ref, w1_ref, w3_ref, w2_ref, gfin_ref, o_ref):
    h = x_ref[...] + _dot(ya_ref[...].astype(F32) + yb_ref[...].astype(F32), wout_ref[...])
    u = _rms(h, gx_ref[...]).astype(BF16)
    q = jnp.dot(u, wq_ref[...], preferred_element_type=F32)
    k = k_ref[...]
    v = v_ref[...]
    sls = [slice(hd * XA_HEAD, (hd + 1) * XA_HEAD) for hd in range(XA_HEADS)]
    sc, pr, heads = {}, {}, {}

    def scores(hd):
        sc[hd] = _dot_nt(q[:, sls[hd]], k[:, sls[hd]]) * (XA_HEAD ** -0.5)

    def softmax(hd):
        s = sc[hd] - jnp.max(sc[hd], axis=-1, keepdims=True)
        e = jnp.exp(s)
        pr[hd] = e / jnp.sum(e, axis=-1, keepdims=True)

    def values(hd):
        heads[hd] = _dot(pr[hd], v[:, sls[hd]])

    for step in range(XA_HEADS + 2):
        if step < XA_HEADS:
            scores(step)
        if 0 <= step - 1 < XA_HEADS:
            softmax(step - 1)
        if 0 <= step - 2 < XA_HEADS:
            values(step - 2)
    h = h + _dot(jnp.concatenate([heads[hd] for hd in range(XA_HEADS)], axis=-1), wo_ref[...])
    u = _rms(h, gf_ref[...]).astype(BF16)
    a = jnp.dot(u, w1_ref[...], preferred_element_type=F32)
    b = jnp.dot(u, w3_ref[...], preferred_element_type=F32)
    mid = (a * _sigmoid(a)) * b
    o_ref[...] = _rms(h + _dot(mid, w2_ref[...]), gfin_ref[...])


def _tail(x2, ya, yb, wout, gx, wq, km, vm, wo, gf, w1, w3, w2, gfin, B, S, n_mem):
    tm = _pick(S, (512, 256))
    nt = S // tm
    row = pl.BlockSpec((tm, D_MODEL), lambda b, t: (b * nt + t, 0))
    vec = pl.BlockSpec((1, D_MODEL), lambda b, t: (0, 0))
    kv = pl.BlockSpec((n_mem, D_MODEL), lambda b, t: (b, 0))
    resident = lambda shape: pl.BlockSpec(shape, lambda b, t: (0, 0), pipeline_mode=pl.Buffered(1))
    sq = resident((D_MODEL, D_MODEL))
    return pl.pallas_call(
        _tail_kernel,
        grid=(B, nt),
        in_specs=[row, row, row, sq, vec, sq, kv, kv, sq,
                  vec, resident((D_MODEL, D_FF)), resident((D_MODEL, D_FF)),
                  resident((D_FF, D_MODEL)), vec],
        out_specs=row,
        out_shape=jax.ShapeDtypeStruct((B * S, D_MODEL), F32),
        compiler_params=_cparams(("parallel", "parallel")),
        name="tail",
    )(x2, ya, yb, wout, gx, wq, km, vm, wo, gf, w1, w3, w2, gfin)


def _pack_mu_lo(mu):
    lo = mu[3 * RW_WIDTH:]
    return jnp.concatenate([lo, jnp.zeros((RW_LO_PAD - RW_LO,), mu.dtype)])[None, :].astype(F32)


def kernel(x, mem, norm_mix_g, w_in, rw_mu, rw_w0, rw_w2, rw_a0, rw_a2, rw_g2, rw_k_k, rw_k_a, rw_r_k, rw_ln_w, rw_ln_b, hg_lb_logits, hg_norm_g, w_out, norm_xa_g, norm_mem_g, xa_wq, xa_wk, xa_wv, xa_wo, norm_ffn_g, ffn_w1, ffn_w3, ffn_w2, norm_final_g):
    B, S, _ = x.shape
    n_mem = mem.shape[1]
    depth = w_in.shape[0]
    assert depth == 1, "single-layer block"
    assert S % (RW_GROUP * RW_CHUNK) == 0 and S % (HG_GROUP * HG_CHUNK) == 0
    l = 0
    T = B * S
    row = lambda a: a.reshape(1, -1).astype(F32)

    x2 = x.reshape(T, D_MODEL)
    mu = rw_mu[l].astype(F32)
    cols, lo_feat = _inproj(x2, row(norm_mix_g[l]), jnp.swapaxes(w_in[l], 0, 1).astype(BF16),
                            _pack_mu_lo(mu), S)
    cols3 = cols.reshape(B, S, N_COLS)
    lo_feat = lo_feat.reshape(B, S, RW_LO_FEAT)
    rw = dict(
        mu_rkv=mu[:3 * RW_WIDTH].reshape(3, RW_WIDTH),
        w0=row(rw_w0[l]), w2=rw_w2[l].astype(BF16), a0=row(rw_a0[l]), a2=rw_a2[l].astype(BF16),
        g2=rw_g2[l].astype(BF16), k_k=row(rw_k_k[l]), k_a=row(rw_k_a[l]), r_k=row(rw_r_k[l]),
        ln_w=row(rw_ln_w[l]), ln_b=row(rw_ln_b[l]))
    ya = _rwkv(cols3, lo_feat, rw).reshape(T, D_MODEL)
    yb = _hgrn(cols3, hg_lb_logits.astype(F32), row(hg_norm_g[l])).reshape(T, D_MODEL)

    km, vm = _memkv(mem.reshape(B * n_mem, D_MODEL), row(norm_mem_g[l]),
                    xa_wk[l].astype(BF16), xa_wv[l].astype(BF16), B, n_mem)
    out = _tail(x2, ya, yb, w_out[l].astype(BF16), row(norm_xa_g[l]), xa_wq[l].astype(BF16), km, vm,
                xa_wo[l].astype(BF16), row(norm_ffn_g[l]), ffn_w1[l].astype(BF16),
                ffn_w3[l].astype(BF16), ffn_w2[l].astype(BF16), row(norm_final_g), B, S, n_mem)
    return out.reshape(B, S, D_MODEL)
```
